```python
import jax, jax.numpy as jnp
from jax import lax
import numpy as np

D_MODEL = 4096
BATCH = 4
SEQ = 2048
DEPTH = 4
DEC_BATCH = 32
DEC_SEQ = 4
PAST_LEN = 8192
PAGE_SIZE = 128

DN_HEADS = 16
DN_DK = 128
DN_DV = 128
DN_CONV = 4
DN_CHUNK = 64
DN_QK = DN_HEADS * DN_DK
DN_VW = DN_HEADS * DN_DV
DN_CONV_WIDTH = 2 * DN_QK + DN_VW
POOL_WINDOWS = (2, 4, 8, 16)
N_POOL_GROUPS = 4
POOL_GROUP = 256
POOL_WIDTH = N_POOL_GROUPS * POOL_GROUP
POOL_HIST = 15
SWA_HEADS = 16
SWA_KV_HEADS = 4
SWA_GROUP = SWA_HEADS // SWA_KV_HEADS
SWA_HD = 64
WINDOW = 128
D_FF = 11008
FFN_CONV = 3
EPS = 1e-6
NEG = -1e30
IN_SIZES = (DN_QK, DN_QK, DN_VW, DN_VW, DN_HEADS, DN_HEADS, POOL_WIDTH,
            SWA_HEADS * SWA_HD, SWA_KV_HEADS * SWA_HD, SWA_KV_HEADS * SWA_HD, 3 * D_MODEL)
IN_WIDTH = sum(IN_SIZES)

kernel_name = 'hybrid_deltanet_pool_swa_decoder_step'


def _offsets(sizes):
    return [int(o) for o in np.cumsum(sizes)[:-1]]


def rmsnorm(x, w):
    xf = x.astype(jnp.float32)
    y = xf * lax.rsqrt(jnp.mean(xf * xf, axis=-1, keepdims=True) + EPS)
    return (y * w.astype(jnp.float32)).astype(x.dtype)


def _l2norm(x):
    xf = x.astype(jnp.float32)
    return xf * lax.rsqrt(jnp.sum(xf * xf, axis=-1, keepdims=True) + EPS)


def causal_conv(hist, x, w):
    K = w.shape[0]
    L = x.shape[1]
    xx = jnp.concatenate([hist.astype(x.dtype), x], axis=1)
    y = sum(xx[:, j:j + L] * w[j].astype(x.dtype) for j in range(K))
    return y, xx[:, L:]


def gated_delta_rule(q, k, v, g, beta, s0):
    b, L, h, dk = q.shape
    dv = v.shape[-1]
    c = DN_CHUNK if L % DN_CHUNK == 0 else L
    n = L // c
    f32 = jnp.float32

    def blk(t):
        t = t.astype(f32).reshape((b, n, c, h) + t.shape[3:])
        return jnp.moveaxis(t, 3, 1)

    q, k, v, g, beta = blk(q), blk(k), blk(v), blk(g), blk(beta)
    gam = jnp.cumsum(g, axis=-1)
    idx = jnp.arange(c)
    incl = idx[:, None] >= idx[None, :]
    strict = idx[:, None] > idx[None, :]
    diff = gam[..., :, None] - gam[..., None, :]
    decay = jnp.where(incl, jnp.exp(jnp.where(incl, diff, 0.0)), 0.0)
    kk = jnp.einsum('bhnid,bhnjd->bhnij', k, k)
    lhs = jnp.where(strict, beta[..., :, None] * kk * decay, 0.0) + jnp.eye(c, dtype=f32)
    rhs = jnp.concatenate([(beta * jnp.exp(gam))[..., None] * k, beta[..., None] * v], axis=-1)
    sol = lax.linalg.triangular_solve(lhs, rhs, left_side=True, lower=True, unit_diagonal=True)
    w_k, u_base = sol[..., :dk], sol[..., dk:]
    p_mat = jnp.einsum('bhnid,bhnjd->bhnij', q, k) * decay
    q_dec = jnp.exp(gam)[..., None] * q
    k_dec = jnp.exp(gam[..., -1:] - gam)[..., None] * k
    tot = jnp.exp(gam[..., -1])

    def step(s, xs):
        wk_c, ub_c, p_c, qd_c, kd_c, tot_c = xs
        u = ub_c - jnp.einsum('bhcd,bhde->bhce', wk_c, s)
        o = jnp.einsum('bhcd,bhde->bhce', qd_c, s) + jnp.einsum('bhij,bhje->bhie', p_c, u)
        s = tot_c[..., None, None] * s + jnp.einsum('bhcd,bhce->bhde', kd_c, u)
        return s, o

    xs = tuple(jnp.moveaxis(t, 2, 0) for t in (w_k, u_base, p_mat, q_dec, k_dec, tot))
    s_final, o = lax.scan(step, s0.astype(f32), xs)
    o = jnp.transpose(o, (1, 0, 3, 2, 4)).reshape(b, L, h, dv)
    return o, s_final


def delta_mixer(dq, dk, dv, dz, dbeta, dalpha, conv_h, s0, conv_w, a_log, dt_bias, norm_w):
    bsz, L, _ = dq.shape
    f32 = jnp.float32
    qkv, conv_new = causal_conv(conv_h, jnp.concatenate([dq, dk, dv], axis=-1), conv_w)
    qkv = jax.nn.silu(qkv)
    q, k, v = jnp.split(qkv, [DN_QK, 2 * DN_QK], axis=-1)
    q = _l2norm(q.reshape(bsz, L, DN_HEADS, DN_DK)) * (DN_DK ** -0.5)
    k = _l2norm(k.reshape(bsz, L, DN_HEADS, DN_DK))
    v = v.reshape(bsz, L, DN_HEADS, DN_DV)
    beta = jax.nn.sigmoid(dbeta.astype(f32))
    g = -jnp.exp(a_log.astype(f32)) * jax.nn.softplus(dalpha.astype(f32) + dt_bias.astype(f32))
    o, s_new = gated_delta_rule(q, k, v, g, beta, s0)
    o = rmsnorm(o, norm_w) * jax.nn.silu(dz.reshape(bsz, L, DN_HEADS, DN_DV).astype(f32))
    return o.reshape(bsz, L, DN_VW).astype(dz.dtype), conv_new, s_new


def pool_mixer(p, hist, hist_valid, w_pool, scale):
    bsz, L, _ = p.shape
    f32 = jnp.float32
    xx = jnp.concatenate([hist.astype(p.dtype), p], axis=1)
    valid = jnp.concatenate([jnp.full((POOL_HIST,), hist_valid, f32), jnp.ones((L,), f32)])
    cs = jnp.pad(jnp.cumsum(xx.astype(f32) * valid[None, :, None], axis=1), ((0, 0), (1, 0), (0, 0)))
    cn = jnp.pad(jnp.cumsum(valid), (1, 0))
    hi = POOL_HIST + 1
    groups = []
    for gi, w in enumerate(POOL_WINDOWS):
        c0 = gi * POOL_GROUP
        ssum = cs[:, hi:hi + L, c0:c0 + POOL_GROUP] - cs[:, hi - w:hi - w + L, c0:c0 + POOL_GROUP]
        cnt = cn[hi:hi + L] - cn[hi - w:hi - w + L]
        groups.append(ssum / cnt[None, :, None])
    pooled = jnp.stack(groups, axis=2) - p.astype(f32).reshape(bsz, L, N_POOL_GROUPS, POOL_GROUP)
    y = jnp.einsum('blgc,gcd->blgd', pooled, w_pool.astype(f32))
    y = y * scale.astype(f32).reshape(N_POOL_GROUPS, POOL_GROUP)
    return y.reshape(bsz, L, POOL_WIDTH).astype(p.dtype), xx[:, L:]


def alibi_slopes():
    return jnp.exp2(-8.0 * jnp.arange(1, SWA_HEADS + 1, dtype=jnp.float32) / SWA_HEADS)


def _sink_attention(q, k, v, qpos, kpos, sinks):
    f32 = jnp.float32
    s = jnp.einsum('bnqkgd,bnskd->bnkgqs', q, k, preferred_element_type=f32) * (SWA_HD ** -0.5)
    dist = qpos[:, :, None] - kpos[:, None, :]
    valid = (dist >= 0) & (dist <= WINDOW) & (kpos[:, None, :] >= 0)
    slopes = alibi_slopes().reshape(1, 1, SWA_KV_HEADS, SWA_GROUP, 1, 1)
    s = s - slopes * dist.astype(f32)[None, :, None, None]
    s = jnp.where(valid[None, :, None, None], s, NEG)
    sk = sinks.astype(f32).reshape(1, 1, SWA_KV_HEADS, SWA_GROUP, 1, 1)
    m = jnp.maximum(jnp.max(s, axis=-1, keepdims=True), sk)
    e = jnp.exp(s - m)
    probs = (e / (jnp.sum(e, axis=-1, keepdims=True) + jnp.exp(sk - m))).astype(v.dtype)
    return jnp.einsum('bnkgqs,bnskd->bnqkgd', probs, v)


def swa_prompt(q, k, v, sinks):
    b, L = q.shape[:2]
    nb = L // WINDOW
    qb = q.reshape(b, nb, WINDOW, SWA_KV_HEADS, SWA_GROUP, SWA_HD)

    def band(t):
        tp = jnp.pad(t, ((0, 0), (WINDOW, 0), (0, 0), (0, 0))).reshape(b, nb + 1, WINDOW, SWA_KV_HEADS, SWA_HD)
        return jnp.concatenate([tp[:, :-1], tp[:, 1:]], axis=2)

    qpos = jnp.arange(L).reshape(nb, WINDOW)
    kpos = (jnp.arange(nb) * WINDOW - WINDOW)[:, None] + jnp.arange(2 * WINDOW)[None, :]
    o = _sink_attention(qb, band(k), band(v), qpos, kpos, sinks)
    keep = min(WINDOW, L)
    return o.reshape(b, L, SWA_HEADS * SWA_HD), k[:, L - keep:], v[:, L - keep:]


def swa_sample(q, k, v, k_hist, v_hist, sinks):
    b, L = q.shape[:2]
    wb = k_hist.shape[1]
    kk = jnp.concatenate([k_hist.astype(k.dtype), k], axis=1)
    vv = jnp.concatenate([v_hist.astype(v.dtype), v], axis=1)
    qpos = (PAST_LEN + jnp.arange(L))[None, :]
    kpos = (PAST_LEN - wb + jnp.arange(wb + L))[None, :]
    o = _sink_attention(q.reshape(b, 1, L, SWA_KV_HEADS, SWA_GROUP, SWA_HD), kk[:, None], vv[:, None],
                        qpos, kpos, sinks)
    return o.reshape(b, L, SWA_HEADS * SWA_HD), kk[:, -wb:], vv[:, -wb:]


def conv_ffn(x, hist, w_up, conv_w, conv_b, w_down):
    gate, val = jnp.split(x @ w_up, [D_FF], axis=-1)
    gate, hist_new = causal_conv(hist, gate, conv_w)
    return (jax.nn.silu(gate + conv_b.astype(gate.dtype)) * val) @ w_down, hist_new


def trunk_layer(x, conv_h, s0, pool_h, pool_valid, k_h, v_h, ffn_h, wts):
    (n1, w_in, dconv_w, a_log, dt_bias, dn_norm_w, pool_w, pool_scale, q_norm_w, k_norm_w, sinks,
     w_ba, w_bb, w_bc, w_out, n2, w_up, fconv_w, fconv_b, w_down) = wts
    bsz, L, _ = x.shape
    h = rmsnorm(x, n1)
    (dq, dk, dv, dz, dbeta, dalpha, pin, sq, sk, sv, gates) = jnp.split(h @ w_in, _offsets(IN_SIZES), axis=-1)
    ya, conv_new, s_new = delta_mixer(dq, dk, dv, dz, dbeta, dalpha, conv_h, s0, dconv_w, a_log, dt_bias, dn_norm_w)
    yb, pool_new = pool_mixer(pin, pool_h, pool_valid, pool_w, pool_scale)
    q = rmsnorm(sq.reshape(bsz, L, SWA_HEADS, SWA_HD), q_norm_w)
    k = rmsnorm(sk.reshape(bsz, L, SWA_KV_HEADS, SWA_HD), k_norm_w)
    v = sv.reshape(bsz, L, SWA_KV_HEADS, SWA_HD)
    if k_h is None:
        yc, k_new, v_new = swa_prompt(q, k, v, sinks)
    else:
        yc, k_new, v_new = swa_sample(q, k, v, k_h, v_h, sinks)
    ga, gb, gc = jnp.split(jax.nn.sigmoid(gates), 3, axis=-1)
    merged = ga * (ya @ w_ba) + gb * (yb @ w_bb) + gc * (yc @ w_bc)
    x = x + merged @ w_out
    f, ffn_new = conv_ffn(rmsnorm(x, n2), ffn_h, w_up, fconv_w, fconv_b, w_down)
    x = x + f
    return x, (s_new, conv_new, pool_new, k_new, v_new, ffn_new)


def setup_inputs(seed: int = 0) -> dict:
    key = jax.random.key(seed)
    ks = iter(jax.random.split(key, 32))
    f32 = jnp.float32

    def nrm(shape, scale):
        return jax.random.normal(next(ks), shape, f32) * scale

    win = min(WINDOW, PAST_LEN)
    return {
        'x_prompt': nrm((BATCH, SEQ, D_MODEL), 1.0),
        'x_sample': nrm((DEC_BATCH, DEC_SEQ, D_MODEL), 1.0),
        'state_delta': nrm((DEPTH, DEC_BATCH, DN_HEADS, DN_DK, DN_DV), 0.05),
        'state_dconv': nrm((DEPTH, DEC_BATCH, DN_CONV - 1, DN_CONV_WIDTH), 1.0),
        'state_pool': nrm((DEPTH, DEC_BATCH, POOL_HIST, POOL_WIDTH), 1.0),
        'cache_swa_k': nrm((DEPTH, DEC_BATCH, win, SWA_KV_HEADS, SWA_HD), 1.0),
        'cache_swa_v': nrm((DEPTH, DEC_BATCH, win, SWA_KV_HEADS, SWA_HD), 1.0),
        'state_ffn_conv': nrm((DEPTH, DEC_BATCH, FFN_CONV - 1, D_FF), 1.0),
        'norm1_w': 1.0 + nrm((DEPTH, D_MODEL), 0.02),
        'w_in': nrm((DEPTH, D_MODEL, IN_WIDTH), D_MODEL ** -0.5),
        'dconv_w': nrm((DEPTH, DN_CONV, DN_CONV_WIDTH), DN_CONV ** -0.5),
        'dn_a_log': jnp.log(jax.random.uniform(next(ks), (DEPTH, DN_HEADS), f32, 1.0, 16.0)),
        'dn_dt_bias': 1.0 + nrm((DEPTH, DN_HEADS), 0.1),
        'dn_norm_w': 1.0 + nrm((DEPTH, DN_DV), 0.02),
        'pool_w': nrm((DEPTH, N_POOL_GROUPS, POOL_GROUP, POOL_GROUP), POOL_GROUP ** -0.5),
        'pool_scale': 1.0 + nrm((DEPTH, POOL_WIDTH), 0.02),
        'q_norm_w': 1.0 + nrm((DEPTH, SWA_HD), 0.02),
        'k_norm_w': 1.0 + nrm((DEPTH, SWA_HD), 0.02),
        'sinks': nrm((DEPTH, SWA_HEADS), 0.5),
        'w_branch_a': nrm((DEPTH, DN_VW, D_MODEL), DN_VW ** -0.5),
        'w_branch_b': nrm((DEPTH, POOL_WIDTH, D_MODEL), POOL_WIDTH ** -0.5),
        'w_branch_c': nrm((DEPTH, SWA_HEADS * SWA_HD, D_MODEL), (SWA_HEADS * SWA_HD) ** -0.5),
        'w_out': nrm((DEPTH, D_MODEL, D_MODEL), D_MODEL ** -0.5),
        'norm2_w': 1.0 + nrm((DEPTH, D_MODEL), 0.02),
        'w_up': nrm((DEPTH, D_MODEL, 2 * D_FF), D_MODEL ** -0.5),
        'ffn_conv_w': nrm((DEPTH, FFN_CONV, D_FF), FFN_CONV ** -0.5),
        'ffn_conv_b': nrm((DEPTH, D_FF), 0.01),
        'w_down': nrm((DEPTH, D_FF, D_MODEL), D_FF ** -0.5),
    }


def reference(x_prompt, x_sample, state_delta, state_dconv, state_pool, cache_swa_k, cache_swa_v, state_ffn_conv,
              norm1_w, w_in, dconv_w, dn_a_log, dn_dt_bias, dn_norm_w, pool_w, pool_scale, q_norm_w, k_norm_w,
              sinks, w_branch_a, w_branch_b, w_branch_c, w_out, norm2_w, w_up, ffn_conv_w, ffn_conv_b, w_down):
    dt = x_prompt.dtype
    bp = x_prompt.shape[0]
    xp, xs = x_prompt, x_sample
    outs_p = [[] for _ in range(6)]
    outs_s = [[] for _ in range(6)]
    for l in range(DEPTH):
        wts = (norm1_w[l], w_in[l], dconv_w[l], dn_a_log[l], dn_dt_bias[l], dn_norm_w[l], pool_w[l],
               pool_scale[l], q_norm_w[l], k_norm_w[l], sinks[l], w_branch_a[l], w_branch_b[l],
               w_branch_c[l], w_out[l], norm2_w[l], w_up[l], ffn_conv_w[l], ffn_conv_b[l], w_down[l])
        xp, st_p = trunk_layer(xp,
                               jnp.zeros((bp, DN_CONV - 1, DN_CONV_WIDTH), dt),
                               jnp.zeros((bp, DN_HEADS, DN_DK, DN_DV), jnp.float32),
                               jnp.zeros((bp, POOL_HIST, POOL_WIDTH), dt), 0.0,
                               None, None,
                               jnp.zeros((bp, FFN_CONV - 1, D_FF), dt), wts)
        xs, st_s = trunk_layer(xs, state_dconv[l], state_delta[l], state_pool[l], 1.0,
                               cache_swa_k[l], cache_swa_v[l], state_ffn_conv[l], wts)
        for lst, a in zip(outs_p, st_p):
            lst.append(a)
        for lst, a in zip(outs_s, st_s):
            lst.append(a)
    y_prompt = xp
    y_sample = xs
    p_delta = jnp.stack(outs_p[0]).astype(state_delta.dtype)
    p_dconv = jnp.stack(outs_p[1])
    p_pool = jnp.stack(outs_p[2])
    p_swa_k = jnp.stack(outs_p[3])
    p_swa_v = jnp.stack(outs_p[4])
    p_ffn_conv = jnp.stack(outs_p[5])
    s_delta = jnp.stack(outs_s[0]).astype(state_delta.dtype)
    s_dconv = jnp.stack(outs_s[1])
    s_pool = jnp.stack(outs_s[2])
    s_swa_k = jnp.stack(outs_s[3])
    s_swa_v = jnp.stack(outs_s[4])
    s_ffn_conv = jnp.stack(outs_s[5])
    return (y_prompt, y_sample, p_delta, p_dconv, p_pool, p_swa_k, p_swa_v, p_ffn_conv,
            s_delta, s_dconv, s_pool, s_swa_k, s_swa_v, s_ffn_conv)
```

```python
import functools
import math

import jax
import jax.numpy as jnp
from jax import lax
from jax.experimental import pallas as pl
from jax.experimental.pallas import tpu as pltpu

F32 = jnp.float32
BF16 = jnp.bfloat16
EPS = 1e-6
NEG = -1e30

LANES = 128
SUBLANES = 8
VMEM_LIMIT = 56 * 1024 * 1024

DN_HEADS = 16
DN_D = 128
DN_W = DN_HEADS * DN_D
DN_CONV = 4
POOL_WINDOWS = (2, 4, 8, 16)
POOL_GROUP = 256
POOL_W = 1024
POOL_HIST = 15
SWA_HEADS = 16
SWA_KV = 4
SWA_HD = 64
SWA_W = SWA_HEADS * SWA_HD
SWA_KW = SWA_KV * SWA_HD
WINDOW = 128
FFN_CONV = 3

DELTA_CHUNK = 128
SAMPLE_SEQ_ROWS = 32

SDS = jax.ShapeDtypeStruct


def _cparams(sem, vmem=VMEM_LIMIT):
    return pltpu.CompilerParams(dimension_semantics=sem, vmem_limit_bytes=vmem)


def _pick(n, target, mult=16):
    best = None
    for d in range(mult, min(n, target) + 1, mult):
        if n % d == 0:
            best = d
    assert best is not None, (n, target, mult)
    return best


def _sigmoid(x):
    return 1.0 / (1.0 + jnp.exp(-x))


def _silu(x):
    return x * _sigmoid(x)


def _dot(a, b):
    return jnp.dot(a.astype(BF16), b.astype(BF16), preferred_element_type=F32)


def _dot_nt(a, b):
    return lax.dot_general(a.astype(BF16), b.astype(BF16), (((1,), (1,)), ((), ())),
                           preferred_element_type=F32)


def _dot_tn(a, b):
    return lax.dot_general(a.astype(BF16), b.astype(BF16), (((0,), (0,)), ((), ())),
                           preferred_element_type=F32)


def _dot_exact(a, b):
    a1 = a.astype(BF16)
    r1 = a - a1.astype(F32)
    a2 = r1.astype(BF16)
    a3 = (r1 - a2.astype(F32)).astype(BF16)
    bb = b.astype(BF16)
    out = jnp.dot(a3, bb, preferred_element_type=F32)
    out = out + jnp.dot(a2, bb, preferred_element_type=F32)
    return out + jnp.dot(a1, bb, preferred_element_type=F32)


def _rmsnorm_kernel(x_ref, w_ref, o_ref):
    x = x_ref[...]
    ms = jnp.mean(x * x, axis=-1, keepdims=True)
    o_ref[...] = (x * lax.rsqrt(ms + EPS) * w_ref[...]).astype(o_ref.dtype)


def rmsnorm_bf16(x, w):
    rows, d = x.shape
    br = _pick(rows, 256, SUBLANES)
    return pl.pallas_call(
        _rmsnorm_kernel,
        out_shape=SDS((rows, d), BF16),
        grid=(rows // br,),
        in_specs=[pl.BlockSpec((br, d), lambda i: (i, 0)), pl.BlockSpec((1, d), lambda i: (0, 0))],
        out_specs=pl.BlockSpec((br, d), lambda i: (i, 0)),
        compiler_params=_cparams(("parallel",)),
        name="rmsnorm",
    )(x, w.reshape(1, d))


def _mm_kernel(*refs, nk, has_res):
    if has_res:
        a_ref, w_ref, r_ref, o_ref = refs
    else:
        a_ref, w_ref, o_ref = refs
        r_ref = None
    acc = jnp.dot(a_ref[...], w_ref[...], preferred_element_type=F32)
    if nk == 1:
        if has_res:
            acc = acc + r_ref[...]
        o_ref[...] = acc.astype(o_ref.dtype)
    else:
        k = pl.program_id(2)

        @pl.when(k == 0)
        def _():
            o_ref[...] = (acc + r_ref[...]) if has_res else acc

        @pl.when(k > 0)
        def _():
            o_ref[...] += acc


def matmul(a, w, *, bm, bn, bk=None, residual=None, out_dtype=F32, name="matmul"):
    m, kdim = a.shape
    n = w.shape[1]
    bk = kdim if bk is None else bk
    nk = kdim // bk
    assert m % bm == 0 and n % bn == 0 and kdim % bk == 0
    assert nk == 1 or out_dtype == F32
    in_specs = [pl.BlockSpec((bm, bk), lambda i, j, k: (i, k)),
                pl.BlockSpec((bk, bn), lambda i, j, k: (k, j))]
    args = [a, w]
    if residual is not None:
        in_specs.append(pl.BlockSpec((bm, bn), lambda i, j, k: (i, j)))
        args.append(residual)
    return pl.pallas_call(
        functools.partial(_mm_kernel, nk=nk, has_res=residual is not None),
        out_shape=SDS((m, n), out_dtype),
        grid=(m // bm, n // bn, nk),
        in_specs=in_specs,
        out_specs=pl.BlockSpec((bm, bn), lambda i, j, k: (i, j)),
        compiler_params=_cparams(("parallel", "parallel", "arbitrary")),
        name=name,
    )(*args)


def _dprep_kernel(xq_ref, xk_ref, xv_ref, hq_ref, hk_ref, hv_ref, d_ref, cq_ref, ck_ref, cv_ref, aneg_ref, dtb_ref,
                  q_ref, k_ref, v_ref, g_ref, b_ref, xx_scr, gam_scr, beta_scr, *, tb, ls, chunk, valid_from):
    i = pl.program_id(0)
    h = pl.program_id(1)
    row = lax.broadcasted_iota(jnp.int32, (tb, LANES), 0)
    lane = lax.broadcasted_iota(jnp.int32, (tb, LANES), 1)
    pos = (i * tb + row) % ls
    valid = pos >= valid_from

    @pl.when(h == 0)
    def _():
        d = d_ref[...]
        beta = jnp.where(valid, jnp.where(lane < DN_HEADS, _sigmoid(d), 0.0), 0.0)
        sp = d + dtb_ref[...]
        sp = jnp.maximum(sp, 0.0) + jnp.log(1.0 + jnp.exp(-jnp.abs(sp)))
        is_alpha = jnp.where(lane >= DN_HEADS, jnp.where(lane < 2 * DN_HEADS, 1, 0), 0)
        g = jnp.where(valid, jnp.where(is_alpha == 1, aneg_ref[...] * sp, 0.0), 0.0)
        pc = pos % chunk
        s = 1
        while s < chunk:
            g = g + jnp.where(pc >= s, pltpu.roll(g, s, 0), 0.0)
            s *= 2
        gam_scr[...] = g
        beta_scr[...] = beta

    def conv_silu(x_ref, halo_ref, cw_ref):
        xx_scr[0:SUBLANES, :] = halo_ref[...]
        xx_scr[SUBLANES:SUBLANES + tb, :] = x_ref[...]
        acc = x_ref[...] * cw_ref[DN_CONV - 1:DN_CONV, :]
        for s in range(1, DN_CONV):
            tap = xx_scr[SUBLANES - s:SUBLANES - s + tb, :]
            acc = acc + jnp.where(pos >= s, tap, 0.0) * cw_ref[DN_CONV - 1 - s:DN_CONV - s, :]
        return _silu(acc)

    q = conv_silu(xq_ref, hq_ref, cq_ref)
    q = q * lax.rsqrt(jnp.sum(q * q, axis=-1, keepdims=True) + EPS) * (DN_D ** -0.5)
    q_ref[...] = jnp.where(valid, q, 0.0)
    k = conv_silu(xk_ref, hk_ref, ck_ref)
    k = k * lax.rsqrt(jnp.sum(k * k, axis=-1, keepdims=True) + EPS)
    k_ref[...] = jnp.where(valid, k, 0.0)
    v_ref[...] = jnp.where(valid, conv_silu(xv_ref, hv_ref, cv_ref), 0.0)

    sel_r = lax.broadcasted_iota(jnp.int32, (LANES, LANES), 0)
    b_ref[...] = _dot_exact(beta_scr[...], jnp.where(sel_r == h, 1.0, 0.0))
    g_ref[...] = _dot_exact(gam_scr[...], jnp.where(sel_r == h + DN_HEADS, 1.0, 0.0))


def delta_prep(x, xcol0, dt, dcol, conv_w, a_log, dt_bias, *, rows, ls, chunk, valid_from, tb):
    assert rows % tb == 0 and tb % chunk == 0 and tb % SUBLANES == 0
    hb = tb // SUBLANES
    aneg = jnp.zeros((1, LANES), F32).at[0, DN_HEADS:2 * DN_HEADS].set(-jnp.exp(a_log.astype(F32)))
    dtb = jnp.zeros((1, LANES), F32).at[0, DN_HEADS:2 * DN_HEADS].set(dt_bias.astype(F32))

    def xspec(off):
        return pl.BlockSpec((tb, LANES), lambda i, h: (i, xcol0 + off + h))

    def hspec(off):
        return pl.BlockSpec((SUBLANES, LANES), lambda i, h: (jnp.maximum(i * hb - 1, 0), xcol0 + off + h))

    def cspec(off):
        return pl.BlockSpec((DN_CONV, LANES), lambda i, h: (0, off + h))

    ospec = pl.BlockSpec((tb, LANES), lambda i, h: (i, h))
    out = SDS((rows, DN_W), F32)
    return pl.pallas_call(
        functools.partial(_dprep_kernel, tb=tb, ls=ls, chunk=chunk, valid_from=valid_from),
        out_shape=(out,) * 5,
        grid=(rows // tb, DN_HEADS),
        in_specs=[xspec(0), xspec(DN_HEADS), xspec(2 * DN_HEADS), hspec(0), hspec(DN_HEADS), hspec(2 * DN_HEADS),
                  pl.BlockSpec((tb, LANES), lambda i, h: (i, dcol)),
                  cspec(0), cspec(DN_HEADS), cspec(2 * DN_HEADS),
                  pl.BlockSpec((1, LANES), lambda i, h: (0, 0)), pl.BlockSpec((1, LANES), lambda i, h: (0, 0))],
        out_specs=(ospec,) * 5,
        scratch_shapes=[pltpu.VMEM((tb + SUBLANES, LANES), F32), pltpu.VMEM((tb, LANES), F32),
                        pltpu.VMEM((tb, LANES), F32)],
        compiler_params=_cparams(("parallel", "arbitrary")),
        name="delta_prep",
    )(x, x, x, x, x, x, dt, conv_w, conv_w, conv_w, aneg, dtb)


def _drec_kernel(q_ref, k_ref, v_ref, g_ref, b_ref, z_ref, s0_ref, nw_ref, y_ref, sf_ref, s_scr, *, chunk, heads, nsteps):
    n = pl.program_id(2)

    @pl.when(n == 0)
    def _():
        s_scr[...] = s0_ref[0]

    ri = lax.broadcasted_iota(jnp.int32, (chunk, chunk), 0)
    ci = lax.broadcasted_iota(jnp.int32, (chunk, chunk), 1)
    incl = ri >= ci
    strict = ri > ci
    eye = jnp.where(ri == ci, 1.0, 0.0)
    levels = int(math.log2(chunk)) - 1
    for j in range(heads):
        sl = slice(j * DN_D, (j + 1) * DN_D)
        q = q_ref[:, sl]
        k = k_ref[:, sl]
        v = v_ref[:, sl]
        gb = g_ref[:, sl]
        bb = b_ref[:, sl]
        s = s_scr[j]
        eg = jnp.exp(gb)
        glast = gb[chunk - 1:chunk, :]
        ek = jnp.exp(glast - gb)
        tot = jnp.exp(glast)
        gc = gb[:, :chunk]
        bc = bb[:, :chunk]
        diff = gc - gc.T
        decay = jnp.where(incl, jnp.exp(jnp.where(incl, diff, 0.0)), 0.0)
        kk = _dot_nt(k, k)
        qk = _dot_nt(q, k)
        x = jnp.where(strict, -(bc * kk * decay), 0.0)
        p = eye + x
        for _ in range(levels):
            x = _dot(x, x)
            p = p + _dot(p, x)
        rhs = jnp.concatenate([bb * eg * k, bb * v], axis=1)
        sol = _dot(p, rhs)
        wk = sol[:, :DN_D]
        ub = sol[:, DN_D:]
        u = ub - _dot(wk, s)
        o = _dot(eg * q, s) + _dot(qk * decay, u)
        s_scr[j] = tot * s + _dot_tn(ek * k, u)
        ms = jnp.mean(o * o, axis=-1, keepdims=True)
        z = z_ref[:, sl]
        y_ref[:, sl] = (o * lax.rsqrt(ms + EPS) * nw_ref[...] * _silu(z)).astype(y_ref.dtype)

    @pl.when(n == nsteps - 1)
    def _():
        sf_ref[0] = s_scr[...]


def delta_recurrence(q, k, v, g, b, z, zcol0, s0, norm_w, *, nseq, ls, chunk, heads):
    rows = nseq * ls
    nsteps = ls // chunk
    hw = heads * DN_D
    ispec = pl.BlockSpec((chunk, hw), lambda bi, hg, n: (bi * nsteps + n, hg))
    zc = zcol0 // heads
    assert zcol0 % heads == 0
    sspec = pl.BlockSpec((1, heads, DN_D, DN_D), lambda bi, hg, n: (bi, hg, 0, 0))
    return pl.pallas_call(
        functools.partial(_drec_kernel, chunk=chunk, heads=heads, nsteps=nsteps),
        out_shape=(SDS((rows, DN_W), BF16), SDS((nseq, DN_HEADS, DN_D, DN_D), F32)),
        grid=(nseq, DN_HEADS // heads, nsteps),
        in_specs=[ispec, ispec, ispec, ispec, ispec,
                  pl.BlockSpec((chunk, hw), lambda bi, hg, n: (bi * nsteps + n, zc + hg)),
                  sspec, pl.BlockSpec((1, DN_D), lambda bi, hg, n: (0, 0))],
        out_specs=(ispec, sspec),
        scratch_shapes=[pltpu.VMEM((heads, DN_D, DN_D), F32)],
        compiler_params=_cparams(("parallel", "parallel", "arbitrary")),
        name="delta_recurrence",
    )(q, k, v, g, b, z, s0, norm_w.reshape(1, DN_D))


def _pool_kernel(x_ref, halo_ref, w_ref, sc_ref, y_ref, xx_scr, *, tb, ls):
    i = pl.program_id(0)
    hist = 2 * SUBLANES
    row = lax.broadcasted_iota(jnp.int32, (tb, POOL_GROUP), 0)
    pos = (i * tb + row) % ls
    xx_scr[0:hist, :] = halo_ref[...]
    xx_scr[hist:hist + tb, :] = x_ref[...]
    for gi, w in enumerate(POOL_WINDOWS):
        cs = slice(gi * POOL_GROUP, (gi + 1) * POOL_GROUP)
        x = x_ref[:, cs]
        tot = x
        for s in range(1, w):
            tap = xx_scr[hist - s:hist - s + tb, cs]
            tot = tot + jnp.where(pos >= s, tap, 0.0)
        cnt = jnp.minimum(pos + 1, w).astype(F32)
        pooled = tot / cnt - x
        y = _dot(pooled, w_ref[gi]) * sc_ref[:, cs]
        y_ref[:, cs] = y.astype(y_ref.dtype)


def pool_mixer(x, xcol, w_pool, scale, *, rows, ls, tb):
    assert rows % tb == 0 and tb % (2 * SUBLANES) == 0
    hb = tb // (2 * SUBLANES)
    return pl.pallas_call(
        functools.partial(_pool_kernel, tb=tb, ls=ls),
        out_shape=SDS((rows, POOL_W), BF16),
        grid=(rows // tb,),
        in_specs=[pl.BlockSpec((tb, POOL_W), lambda i: (i, xcol)),
                  pl.BlockSpec((2 * SUBLANES, POOL_W), lambda i: (jnp.maximum(i * hb - 1, 0), xcol)),
                  pl.BlockSpec((len(POOL_WINDOWS), POOL_GROUP, POOL_GROUP), lambda i: (0, 0, 0)),
                  pl.BlockSpec((1, POOL_W), lambda i: (0, 0))],
        out_specs=pl.BlockSpec((tb, POOL_W), lambda i: (i, 0)),
        scratch_shapes=[pltpu.VMEM((tb + 2 * SUBLANES, POOL_W), F32)],
        compiler_params=_cparams(("arbitrary",)),
        name="pool_mixer",
    )(x, x, w_pool.astype(BF16), scale.reshape(1, POOL_W).astype(F32))


def _group_rms(x, w, width):
    rows = x.shape[0]
    lane = lax.broadcasted_iota(jnp.int32, (rows, LANES), 1)
    lo = lane < SWA_HD
    outs = []
    for c in range(width // LANES):
        t = x[:, c * LANES:(c + 1) * LANES]
        t2 = t * t
        tot = jnp.sum(t2, axis=-1, keepdims=True)
        slo = jnp.sum(jnp.where(lo, t2, 0.0), axis=-1, keepdims=True)
        ms = jnp.where(lo, slo, tot - slo) * (1.0 / SWA_HD)
        outs.append(t * lax.rsqrt(ms + EPS))
    return jnp.concatenate(outs, axis=1) * w


def _swa_prep_kernel(q_ref, k_ref, qw_ref, kw_ref, qo_ref, ko_ref):
    qo_ref[...] = _group_rms(q_ref[...], qw_ref[...], SWA_W).astype(qo_ref.dtype)
    ko_ref[...] = _group_rms(k_ref[...], kw_ref[...], SWA_KW)


def swa_prep(x, qcol, kcol, q_norm_w, k_norm_w, *, rows, tb):
    qw = (jnp.tile(q_norm_w.astype(F32), SWA_HEADS) * (SWA_HD ** -0.5)).reshape(1, SWA_W)
    kw = jnp.tile(k_norm_w.astype(F32), SWA_KV).reshape(1, SWA_KW)
    return pl.pallas_call(
        _swa_prep_kernel,
        out_shape=(SDS((rows, SWA_W), BF16), SDS((rows, SWA_KW), F32)),
        grid=(rows // tb,),
        in_specs=[pl.BlockSpec((tb, SWA_W), lambda i: (i, qcol)), pl.BlockSpec((tb, SWA_KW), lambda i: (i, kcol)),
                  pl.BlockSpec((1, SWA_W), lambda i: (0, 0)), pl.BlockSpec((1, SWA_KW), lambda i: (0, 0))],
        out_specs=(pl.BlockSpec((tb, SWA_W), lambda i: (i, 0)), pl.BlockSpec((tb, SWA_KW), lambda i: (i, 0))),
        compiler_params=_cparams(("parallel",)),
        name="swa_prep",
    )(x, x, qw, kw)


def _swa_kernel(sink_ref, q_ref, kp_ref, kc_ref, vp_ref, vc_ref, o_ref, *, qb, first_has_prev):
    g = pl.program_id(1)
    i = pl.program_id(2)
    r = lax.broadcasted_iota(jnp.int32, (qb, WINDOW), 0)
    c = lax.broadcasted_iota(jnp.int32, (qb, WINDOW), 1)
    dist_p = WINDOW + r - c
    dist_c = r - c
    ok_p = dist_p <= WINDOW
    if not first_has_prev:
        ok_p = jnp.logical_and(ok_p, i > 0)
    ok_c = dist_c >= 0
    dpf = dist_p.astype(F32)
    dcf = dist_c.astype(F32)
    for pair in range(2):
        q2 = q_ref[:, pair * LANES:(pair + 1) * LANES]
        acc = jnp.zeros((qb, LANES), F32)
        for e in range(2):
            hl = 2 * pair + e
            hidx = g * (SWA_HEADS // SWA_KV) + hl
            sink = sink_ref[hidx]
            slope = sink_ref[SWA_HEADS + hidx]
            ksl = slice(e * LANES, (e + 1) * LANES)
            sp = lax.dot_general(q2, kp_ref[:, ksl], (((1,), (1,)), ((), ())), preferred_element_type=F32)
            sc = lax.dot_general(q2, kc_ref[:, ksl], (((1,), (1,)), ((), ())), preferred_element_type=F32)
            sp = jnp.where(ok_p, sp - slope * dpf, NEG)
            sc = jnp.where(ok_c, sc - slope * dcf, NEG)
            m = jnp.maximum(jnp.maximum(jnp.max(sp, axis=-1, keepdims=True), jnp.max(sc, axis=-1, keepdims=True)), sink)
            ep = jnp.exp(sp - m)
            ec = jnp.exp(sc - m)
            den = jnp.sum(ep, axis=-1, keepdims=True) + jnp.sum(ec, axis=-1, keepdims=True) + jnp.exp(sink - m)
            inv = 1.0 / den
            acc = acc + jnp.dot((ep * inv).astype(BF16), vp_ref[:, ksl], preferred_element_type=F32)
            acc = acc + jnp.dot((ec * inv).astype(BF16), vc_ref[:, ksl], preferred_element_type=F32)
        o_ref[:, pair * LANES:(pair + 1) * LANES] = acc.astype(o_ref.dtype)


def swa_attention(q, kx_prev, kx_cur, vx_prev, vx_cur, sinks, *, nseq, nq, qb, prev_shift, first_has_prev):
    gw = (SWA_HEADS // SWA_KV) * SWA_HD
    slopes = jnp.exp2(-8.0 * jnp.arange(1, SWA_HEADS + 1, dtype=F32) / SWA_HEADS)
    qspec = pl.BlockSpec((qb, gw), lambda b, g, i, s: (b * nq + i, g))
    pspec = pl.BlockSpec((WINDOW, gw), lambda b, g, i, s: (jnp.maximum(b * nq + i - prev_shift, 0), g))
    cspec = pl.BlockSpec((WINDOW, gw), lambda b, g, i, s: (b * nq + i, g))
    grid_spec = pltpu.PrefetchScalarGridSpec(
        num_scalar_prefetch=1,
        grid=(nseq, SWA_KV, nq),
        in_specs=[qspec, pspec, cspec, pspec, cspec],
        out_specs=qspec,
    )
    return pl.pallas_call(
        functools.partial(_swa_kernel, qb=qb, first_has_prev=first_has_prev),
        out_shape=SDS((nseq * nq * qb, SWA_W), BF16),
        grid_spec=grid_spec,
        compiler_params=_cparams(("parallel", "parallel", "arbitrary")),
        name="swa_attention",
    )(jnp.concatenate([sinks.astype(F32), slopes]), q, kx_prev, kx_cur, vx_prev, vx_cur)


def _two_slot(t):
    rows = t.shape[0]
    t = t.astype(BF16).reshape(rows, SWA_KV, SWA_HD)
    z = jnp.zeros_like(t)
    return jnp.concatenate([t, z, z, t], axis=-1).reshape(rows, SWA_KV * 4 * SWA_HD)


def _merge_kernel(ya_ref, yb_ref, yc_ref, wa_ref, wb_ref, wc_ref, ga_ref, gb_ref, gc_ref, o_ref):
    a = jnp.dot(ya_ref[...], wa_ref[...], preferred_element_type=F32)
    b = jnp.dot(yb_ref[...], wb_ref[...], preferred_element_type=F32)
    c = jnp.dot(yc_ref[...], wc_ref[...], preferred_element_type=F32)
    o = _sigmoid(ga_ref[...]) * a + _sigmoid(gb_ref[...]) * b + _sigmoid(gc_ref[...]) * c
    o_ref[...] = o.astype(o_ref.dtype)


def merge_branches(ya, yb, yc, wa, wb, wc, proj, gcol0, d_model, *, bm, bn):
    m = ya.shape[0]
    gb0 = gcol0 // bn
    gstep = d_model // bn
    assert gcol0 % bn == 0 and d_model % bn == 0 and m % bm == 0

    def gspec(t):
        return pl.BlockSpec((bm, bn), lambda i, j: (i, gb0 + t * gstep + j))

    return pl.pallas_call(
        _merge_kernel,
        out_shape=SDS((m, d_model), BF16),
        grid=(m // bm, d_model // bn),
        in_specs=[pl.BlockSpec((bm, DN_W), lambda i, j: (i, 0)), pl.BlockSpec((bm, POOL_W), lambda i, j: (i, 0)),
                  pl.BlockSpec((bm, SWA_W), lambda i, j: (i, 0)),
                  pl.BlockSpec((DN_W, bn), lambda i, j: (0, j)), pl.BlockSpec((POOL_W, bn), lambda i, j: (0, j)),
                  pl.BlockSpec((SWA_W, bn), lambda i, j: (0, j)),
                  gspec(0), gspec(1), gspec(2)],
        out_specs=pl.BlockSpec((bm, bn), lambda i, j: (i, j)),
        compiler_params=_cparams(("parallel", "parallel")),
        name="merge_branches",
    )(ya, yb, yc, wa, wb, wc, proj, proj, proj)


def _ffn_act_kernel(g_ref, halo_ref, v_ref, cw_ref, cb_ref, o_ref, xx_scr, *, tb, ls):
    i = pl.program_id(0)
    bc = g_ref.shape[1]
    row = lax.broadcasted_iota(jnp.int32, (tb, bc), 0)
    pos = (i * tb + row) % ls
    xx_scr[0:SUBLANES, :] = halo_ref[...]
    xx_scr[SUBLANES:SUBLANES + tb, :] = g_ref[...]
    acc = g_ref[...] * cw_ref[FFN_CONV - 1:FFN_CONV, :]
    for s in range(1, FFN_CONV):
        tap = xx_scr[SUBLANES - s:SUBLANES - s + tb, :]
        acc = acc + jnp.where(pos >= s, tap, 0.0) * cw_ref[FFN_CONV - 1 - s:FFN_CONV - s, :]
    o_ref[...] = (_silu(acc + cb_ref[...]) * v_ref[...]).astype(o_ref.dtype)


def ffn_activation(gate, gcol0, val, vcol0, conv_w, conv_b, d_ff, *, rows, ls, tb, bc):
    assert rows % tb == 0 and d_ff % bc == 0 and gcol0 % bc == 0 and vcol0 % bc == 0
    hb = tb // SUBLANES
    g0 = gcol0 // bc
    v0 = vcol0 // bc
    return pl.pallas_call(
        functools.partial(_ffn_act_kernel, tb=tb, ls=ls),
        out_shape=SDS((rows, d_ff), BF16),
        grid=(rows // tb, d_ff // bc),
        in_specs=[pl.BlockSpec((tb, bc), lambda i, j: (i, g0 + j)),
                  pl.BlockSpec((SUBLANES, bc), lambda i, j: (jnp.maximum(i * hb - 1, 0), g0 + j)),
                  pl.BlockSpec((tb, bc), lambda i, j: (i, v0 + j)),
                  pl.BlockSpec((FFN_CONV, bc), lambda i, j: (0, j)),
                  pl.BlockSpec((1, bc), lambda i, j: (0, j))],
        out_specs=pl.BlockSpec((tb, bc), lambda i, j: (i, j)),
        scratch_shapes=[pltpu.VMEM((tb + SUBLANES, bc), F32)],
        compiler_params=_cparams(("parallel", "parallel")),
        name="ffn_activation",
    )(gate, gate, val, conv_w.astype(F32), conv_b.reshape(1, d_ff).astype(F32))


def _in_layout(d_model):
    off = {}
    c = 0
    for name, width in (("dq", DN_W), ("dk", DN_W), ("dv", DN_W), ("dz", DN_W),
                        ("ga", d_model), ("gb", d_model), ("gc", d_model),
                        ("pin", POOL_W), ("sq", SWA_W), ("sk", SWA_KW), ("sv", SWA_KW), ("ba", LANES)):
        off[name] = c
        c += width
    return off, c


def _prep_w_in(w_in, d_model, n_pad):
    sizes = (DN_W, DN_W, DN_W, DN_W, DN_HEADS, DN_HEADS, POOL_W, SWA_W, SWA_KW, SWA_KW, 3 * d_model)
    offs = [0]
    for s in sizes:
        offs.append(offs[-1] + s)
    seg = lambda a, b: w_in[:, offs[a]:offs[b]]
    parts = [seg(0, 4), seg(10, 11), seg(6, 10), seg(4, 6)]
    used = sum(p.shape[1] for p in parts)
    parts.append(jnp.zeros((w_in.shape[0], n_pad - used), w_in.dtype))
    return jnp.concatenate(parts, axis=1).astype(BF16)


def _ext(hist, new, ls):
    nseq, hr, c = hist.shape
    ln = new.shape[1]
    z = jnp.zeros((nseq, ls - hr - ln, c), new.dtype)
    return jnp.concatenate([z, hist.astype(new.dtype), new], axis=1).reshape(nseq * ls, c)


def _layer(x, nseq, seqlen, states, wts, cfg):
    (n1, w_in_p, dconv_w, a_log, dt_bias, dn_norm_w, pool_w, pool_scale, q_norm_w, k_norm_w, sinks,
     w_ba, w_bb, w_bc, w_out, n2, w_up, fconv_w, fconv_b, w_down) = wts
    rows, d_model = x.shape
    d_ff = w_down.shape[0]
    off = cfg["off"]
    bm = cfg["bm"](rows)
    fresh = states is None

    h = rmsnorm_bf16(x, n1)
    proj = matmul(h, w_in_p, bm=bm, bn=cfg["bn_in"], name="in_proj")
    proj3 = proj.reshape(nseq, seqlen, -1)

    def cols(name, width):
        return proj3[:, :, off[name]:off[name] + width]

    if fresh:
        ls, chunk, vfrom = seqlen, cfg["chunk"], 0
        tbp = _pick(seqlen, cfg["tb_prep"], chunk)
        q, k, v, g, b = delta_prep(proj, 0, proj, off["ba"] // LANES, dconv_w, a_log, dt_bias,
                                   rows=rows, ls=ls, chunk=chunk, valid_from=0, tb=tbp)
        s0 = jnp.zeros((nseq, DN_HEADS, DN_D, DN_D), F32)
        ya, s_new = delta_recurrence(q, k, v, g, b, proj, off["dz"] // LANES, s0, dn_norm_w,
                                     nseq=nseq, ls=ls, chunk=chunk, heads=cfg["heads"])
        conv_new = cols("dq", 3 * DN_W)[:, seqlen - (DN_CONV - 1):]
    else:
        conv_h, s0, pool_h, k_h, v_h, ffn_h = states
        ls = SAMPLE_SEQ_ROWS
        vfrom = ls - seqlen
        xe = _ext(conv_h, cols("dq", 3 * DN_W), ls)
        de = _ext(jnp.zeros((nseq, 0, LANES), F32), cols("ba", LANES), ls)
        ze = _ext(jnp.zeros((nseq, 0, DN_W), F32), cols("dz", DN_W), ls)
        erows = nseq * ls
        q, k, v, g, b = delta_prep(xe, 0, de, 0, dconv_w, a_log, dt_bias,
                                   rows=erows, ls=ls, chunk=ls, valid_from=vfrom, tb=_pick(erows, 1024, ls))
        ya_e, s_new = delta_recurrence(q, k, v, g, b, ze, 0, s0.astype(F32), dn_norm_w,
                                       nseq=nseq, ls=ls, chunk=ls, heads=cfg["heads"])
        ya = ya_e.reshape(nseq, ls, DN_W)[:, vfrom:].reshape(rows, DN_W)
        conv_new = xe.reshape(nseq, ls, 3 * DN_W)[:, ls - (DN_CONV - 1):]

    if fresh:
        yb = pool_mixer(proj, off["pin"] // POOL_W, pool_w, pool_scale, rows=rows, ls=seqlen,
                        tb=_pick(seqlen, 512, 2 * SUBLANES))
        pool_new = cols("pin", POOL_W)[:, seqlen - POOL_HIST:]
    else:
        pe = _ext(pool_h, cols("pin", POOL_W), SAMPLE_SEQ_ROWS)
        erows = nseq * SAMPLE_SEQ_ROWS
        yb_e = pool_mixer(pe, 0, pool_w, pool_scale, rows=erows, ls=SAMPLE_SEQ_ROWS, tb=_pick(erows, 512, SAMPLE_SEQ_ROWS))
        yb = yb_e.reshape(nseq, SAMPLE_SEQ_ROWS, POOL_W)[:, SAMPLE_SEQ_ROWS - seqlen:].reshape(rows, POOL_W)
        pool_new = pe.reshape(nseq, SAMPLE_SEQ_ROWS, POOL_W)[:, SAMPLE_SEQ_ROWS - POOL_HIST:]

    qn, kn = swa_prep(proj, off["sq"] // SWA_W, off["sk"] // SWA_KW, q_norm_w, k_norm_w, rows=rows,
                      tb=_pick(rows, 512, SUBLANES))
    vraw = cols("sv", SWA_KW)
    kn3 = kn.reshape(nseq, seqlen, SWA_KW)
    if fresh:
        kx = _two_slot(kn)
        vx = _two_slot(vraw.reshape(rows, SWA_KW))
        yc = swa_attention(qn, kx, kx, vx, vx, sinks, nseq=nseq, nq=seqlen // WINDOW, qb=WINDOW,
                           prev_shift=1, first_has_prev=False)
        keep = min(WINDOW, seqlen)
        k_new = kn3[:, seqlen - keep:].reshape(nseq, keep, SWA_KV, SWA_HD)
        v_new = vraw[:, seqlen - keep:].reshape(nseq, keep, SWA_KV, SWA_HD)
    else:
        wb = k_h.shape[1]
        qpad = jnp.zeros((nseq, SUBLANES - seqlen, SWA_W), BF16)
        qe = jnp.concatenate([qn.reshape(nseq, seqlen, SWA_W), qpad], axis=1).reshape(nseq * SUBLANES, SWA_W)
        kpad = jnp.zeros((nseq, WINDOW - seqlen, SWA_KW), F32)
        kc = _two_slot(jnp.concatenate([kn3, kpad], axis=1).reshape(nseq * WINDOW, SWA_KW))
        vc = _two_slot(jnp.concatenate([vraw, kpad], axis=1).reshape(nseq * WINDOW, SWA_KW))
        kp = _two_slot(k_h.reshape(nseq * wb, SWA_KW))
        vp = _two_slot(v_h.reshape(nseq * wb, SWA_KW))
        yc_e = swa_attention(qe, kp, kc, vp, vc, sinks, nseq=nseq, nq=1, qb=SUBLANES,
                             prev_shift=0, first_has_prev=True)
        yc = yc_e.reshape(nseq, SUBLANES, SWA_W)[:, :seqlen].reshape(rows, SWA_W)
        k_new = jnp.concatenate([k_h, kn3.reshape(nseq, seqlen, SWA_KV, SWA_HD).astype(k_h.dtype)], axis=1)[:, -wb:]
        v_new = jnp.concatenate([v_h, vraw.reshape(nseq, seqlen, SWA_KV, SWA_HD).astype(v_h.dtype)], axis=1)[:, -wb:]

    merged = merge_branches(ya, yb, yc, w_ba, w_bb, w_bc, proj, off["ga"], d_model, bm=bm, bn=cfg["bn_merge"])
    x = matmul(merged, w_out, bm=bm, bn=cfg["bn_out"], residual=x, name="out_proj")

    h2 = rmsnorm_bf16(x, n2)
    up = matmul(h2, w_up, bm=bm, bn=cfg["bn_up"], name="up_proj")
    bc = cfg["bc_ffn"]
    if fresh:
        hmid = ffn_activation(up, 0, up, d_ff, fconv_w, fconv_b, d_ff, rows=rows, ls=seqlen,
                              tb=_pick(seqlen, 512, SUBLANES), bc=bc)
        ffn_new = up.reshape(nseq, seqlen, 2 * d_ff)[:, seqlen - (FFN_CONV - 1):, :d_ff]
    else:
        up3 = up.reshape(nseq, seqlen, 2 * d_ff)
        ls = SUBLANES
        ge = _ext(ffn_h, up3[:, :, :d_ff], ls)
        ve = _ext(jnp.zeros((nseq, 0, d_ff), F32), up3[:, :, d_ff:], ls)
        hm_e = ffn_activation(ge, 0, ve, 0, fconv_w, fconv_b, d_ff, rows=nseq * ls, ls=ls, tb=nseq * ls, bc=bc)
        hmid = hm_e.reshape(nseq, ls, d_ff)[:, ls - seqlen:].reshape(rows, d_ff)
        ffn_new = ge.reshape(nseq, ls, d_ff)[:, ls - (FFN_CONV - 1):]
    x = matmul(hmid, w_down, bm=bm, bn=cfg["bn_down"], bk=cfg["bk_down"], residual=x, name="down_proj")
    return x, (s_new, conv_new, pool_new, k_new, v_new, ffn_new)


def _config(d_model, d_ff, n_in):
    return {
        "off": _in_layout(d_model)[0],
        "bm": lambda rows: _pick(rows, 1024),
        "bn_in": _pick(n_in, 1024, LANES),
        "bn_merge": _pick(d_model, 512, LANES),
        "bn_out": _pick(d_model, 1024, LANES),
        "bn_up": _pick(2 * d_ff, 512, LANES),
        "bn_down": _pick(d_model, 512, LANES),
        "bk_down": _pick(d_ff, 5504, LANES),
        "bc_ffn": _pick(d_ff, 512, LANES),
        "chunk": DELTA_CHUNK,
        "tb_prep": 1024,
        "heads": 4,
    }


def kernel(x_prompt, x_sample, state_delta, state_dconv, state_pool, cache_swa_k, cache_swa_v, state_ffn_conv, norm1_w, w_in, dconv_w, dn_a_log, dn_dt_bias, dn_norm_w, pool_w, pool_scale, q_norm_w, k_norm_w, sinks, w_branch_a, w_branch_b, w_branch_c, w_out, norm2_w, w_up, ffn_conv_w, ffn_conv_b, w_down):
    depth = w_in.shape[0]
    bp, lp, d_model = x_prompt.shape
    bs, lsm, _ = x_sample.shape
    d_ff = w_down.shape[1]
    n_used = _in_layout(d_model)[1]
    n_in = -(-n_used // 1024) * 1024
    cfg = _config(d_model, d_ff, n_in)
    xp = x_prompt.reshape(bp * lp, d_model)
    xs = x_sample.reshape(bs * lsm, d_model)
    outs_p = [[] for _ in range(6)]
    outs_s = [[] for _ in range(6)]
    for l in range(depth):
        wts = (norm1_w[l], _prep_w_in(w_in[l], d_model, n_in), dconv_w[l].astype(F32), dn_a_log[l], dn_dt_bias[l],
               dn_norm_w[l].astype(F32), pool_w[l], pool_scale[l], q_norm_w[l], k_norm_w[l], sinks[l],
               w_branch_a[l].astype(BF16), w_branch_b[l].astype(BF16), w_branch_c[l].astype(BF16),
               w_out[l].astype(BF16), norm2_w[l], w_up[l].astype(BF16), ffn_conv_w[l], ffn_conv_b[l],
               w_down[l].astype(BF16))
        xp, st_p = _layer(xp, bp, lp, None, wts, cfg)
        st_in = (state_dconv[l], state_delta[l], state_pool[l], cache_swa_k[l], cache_swa_v[l], state_ffn_conv[l])
        xs, st_s = _layer(xs, bs, lsm, st_in, wts, cfg)
        for lst, a in zip(outs_p, st_p):
            lst.append(a)
        for lst, a in zip(outs_s, st_s):
            lst.append(a)
    dt = x_prompt.dtype
    res = [xp.reshape(bp, lp, d_model), xs.reshape(bs, lsm, d_model)]
    for outs in (outs_p, outs_s):
        res.append(jnp.stack(outs[0]).astype(state_delta.dtype))
        for t in outs[1:]:
            res.append(jnp.stack(t).astype(dt))
    return tuple(res)
```

```python
import functools
import math

import jax
import jax.numpy as jnp
from jax import lax
from jax.experimental import pallas as pl
from jax.experimental.pallas import tpu as pltpu

F32 = jnp.float32
BF16 = jnp.bfloat16
EPS = 1e-6
NEG = -1e30

LANES = 128
SUBLANES = 8
VMEM_LIMIT = 56 * 1024 * 1024

DN_HEADS = 16
DN_D = 128
DN_W = DN_HEADS * DN_D
DN_CONV = 4
POOL_WINDOWS = (2, 4, 8, 16)
POOL_GROUP = 256
POOL_W = 1024
POOL_HIST = 15
SWA_HEADS = 16
SWA_KV = 4
SWA_HD = 64
SWA_W = SWA_HEADS * SWA_HD
SWA_KW = SWA_KV * SWA_HD
WINDOW = 128
FFN_CONV = 3

DELTA_CHUNK = 128
SAMPLE_SEQ_ROWS = 32
REC_HEADS = 8

SDS = jax.ShapeDtypeStruct


def _cparams(sem, vmem=VMEM_LIMIT):
    return pltpu.CompilerParams(dimension_semantics=sem, vmem_limit_bytes=vmem)


def _pick(n, target, mult=16):
    best = None
    for d in range(mult, min(n, target) + 1, mult):
        if n % d == 0:
            best = d
    assert best is not None, (n, target, mult)
    return best


def _sigmoid(x):
    return 1.0 / (1.0 + jnp.exp(-x))


def _silu(x):
    return x * _sigmoid(x)


def _dot(a, b):
    return jnp.dot(a.astype(BF16), b.astype(BF16), preferred_element_type=F32)


def _dot_nt(a, b):
    return lax.dot_general(a.astype(BF16), b.astype(BF16), (((1,), (1,)), ((), ())),
                           preferred_element_type=F32)


def _dot_tn(a, b):
    return lax.dot_general(a.astype(BF16), b.astype(BF16), (((0,), (0,)), ((), ())),
                           preferred_element_type=F32)


def _bdot(a, b):
    return lax.dot_general(a.astype(BF16), b.astype(BF16), (((2,), (1,)), ((0,), (0,))),
                           preferred_element_type=F32)


def _bdot_nt(a, b):
    return lax.dot_general(a.astype(BF16), b.astype(BF16), (((2,), (2,)), ((0,), (0,))),
                           preferred_element_type=F32)


def _bdot_tn(a, b):
    return lax.dot_general(a.astype(BF16), b.astype(BF16), (((1,), (1,)), ((0,), (0,))),
                           preferred_element_type=F32)


def _dot_exact(a, b):
    a1 = a.astype(BF16)
    r1 = a - a1.astype(F32)
    a2 = r1.astype(BF16)
    a3 = (r1 - a2.astype(F32)).astype(BF16)
    bb = b.astype(BF16)
    out = jnp.dot(a3, bb, preferred_element_type=F32)
    out = out + jnp.dot(a2, bb, preferred_element_type=F32)
    return out + jnp.dot(a1, bb, preferred_element_type=F32)


def _rmsnorm_kernel(x_ref, w_ref, o_ref):
    x = x_ref[...]
    ms = jnp.mean(x * x, axis=-1, keepdims=True)
    o_ref[...] = (x * lax.rsqrt(ms + EPS) * w_ref[...]).astype(o_ref.dtype)


def rmsnorm_bf16(x, w):
    rows, d = x.shape
    br = _pick(rows, 256, SUBLANES)
    return pl.pallas_call(
        _rmsnorm_kernel,
        out_shape=SDS((rows, d), BF16),
        grid=(rows // br,),
        in_specs=[pl.BlockSpec((br, d), lambda i: (i, 0)), pl.BlockSpec((1, d), lambda i: (0, 0))],
        out_specs=pl.BlockSpec((br, d), lambda i: (i, 0)),
        compiler_params=_cparams(("parallel",)),
        name="rmsnorm",
    )(x, w.reshape(1, d))


def _mm_kernel(*refs, nk, has_res):
    if has_res:
        a_ref, w_ref, r_ref, o_ref = refs
    else:
        a_ref, w_ref, o_ref = refs
        r_ref = None
    acc = jnp.dot(a_ref[...], w_ref[...], preferred_element_type=F32)
    if nk == 1:
        if has_res:
            acc = acc + r_ref[...]
        o_ref[...] = acc.astype(o_ref.dtype)
    else:
        k = pl.program_id(2)

        @pl.when(k == 0)
        def _():
            o_ref[...] = (acc + r_ref[...]) if has_res else acc

        @pl.when(k > 0)
        def _():
            o_ref[...] += acc


def matmul(a, w, layer, *, bm, bn, bk=None, residual=None, out_dtype=F32, name="matmul"):
    m, kdim = a.shape
    n = w.shape[2]
    bk = kdim if bk is None else bk
    nk = kdim // bk
    assert m % bm == 0 and n % bn == 0 and kdim % bk == 0
    assert nk == 1 or out_dtype == F32
    in_specs = [pl.BlockSpec((bm, bk), lambda i, j, k: (i, k)),
                pl.BlockSpec((None, bk, bn), lambda i, j, k: (layer, k, j))]
    args = [a, w]
    if residual is not None:
        in_specs.append(pl.BlockSpec((bm, bn), lambda i, j, k: (i, j)))
        args.append(residual)
    return pl.pallas_call(
        functools.partial(_mm_kernel, nk=nk, has_res=residual is not None),
        out_shape=SDS((m, n), out_dtype),
        grid=(m // bm, n // bn, nk),
        in_specs=in_specs,
        out_specs=pl.BlockSpec((bm, bn), lambda i, j, k: (i, j)),
        compiler_params=_cparams(("parallel", "parallel", "arbitrary")),
        name=name,
    )(*args)


def _cast_kernel(x_ref, o_ref):
    o_ref[...] = x_ref[...].astype(o_ref.dtype)


def cast_bf16(w, *, br):
    depth, kdim, n = w.shape
    assert kdim % br == 0
    spec = pl.BlockSpec((1, br, n), lambda l, i: (l, i, 0))
    return pl.pallas_call(
        _cast_kernel,
        out_shape=SDS(w.shape, BF16),
        grid=(depth, kdim // br),
        in_specs=[spec],
        out_specs=spec,
        compiler_params=_cparams(("parallel", "parallel")),
        name="cast_bf16",
    )(w)


def _mmw_kernel(*refs, shift, has_res):
    a_ref, w_ref = refs[0], refs[1]
    pos = 2
    w2_ref = r_ref = None
    if shift:
        w2_ref = refs[pos]
        pos += 1
    if has_res:
        r_ref = refs[pos]
        pos += 1
    o_ref, wbf_scr = refs[pos], refs[pos + 1]

    @pl.when(pl.program_id(1) == 0)
    def _():
        w = w_ref[...]
        if shift:
            w = jnp.concatenate([w[:, shift:], w2_ref[:, :shift]], axis=1)
        wbf_scr[...] = w.astype(BF16)

    acc = jnp.dot(a_ref[...], wbf_scr[...], preferred_element_type=F32)
    if has_res:
        acc = acc + r_ref[...]
    o_ref[...] = acc.astype(o_ref.dtype)


def matmul_wres(a, w, layer, wcol0, n, *, bm, bn, shift=0, residual=None, out_dtype=F32, name="matmul_wres"):
    m, kdim = a.shape
    assert m % bm == 0 and n % bn == 0 and wcol0 % bn == 0 and 0 <= shift < LANES
    j0 = wcol0 // bn
    in_specs = [pl.BlockSpec((bm, kdim), lambda j, i: (i, 0)),
                pl.BlockSpec((None, kdim, bn), lambda j, i: (layer, 0, j0 + j))]
    args = [a, w]
    if shift:
        in_specs.append(pl.BlockSpec((None, kdim, LANES), lambda j, i: (layer, 0, (wcol0 + (j + 1) * bn) // LANES)))
        args.append(w)
    if residual is not None:
        in_specs.append(pl.BlockSpec((bm, bn), lambda j, i: (i, j)))
        args.append(residual)
    return pl.pallas_call(
        functools.partial(_mmw_kernel, shift=shift, has_res=residual is not None),
        out_shape=SDS((m, n), out_dtype),
        grid=(n // bn, m // bm),
        in_specs=in_specs,
        out_specs=pl.BlockSpec((bm, bn), lambda j, i: (i, j)),
        scratch_shapes=[pltpu.VMEM((kdim, bn), BF16)],
        compiler_params=_cparams(("parallel", "arbitrary")),
        name=name,
    )(*args)


def _upffn_kernel(a_ref, wg_ref, wv_ref, cw_ref, cb_ref, o_ref, tail_ref, wg_scr, wv_scr, g_scr, *, bm, tiles_per_seq):
    i = pl.program_id(1)

    @pl.when(i == 0)
    def _():
        wg_scr[...] = wg_ref[...].astype(BF16)
        wv_scr[...] = wv_ref[...].astype(BF16)

    a = a_ref[...]
    gate = jnp.dot(a, wg_scr[...], preferred_element_type=F32)
    val = jnp.dot(a, wv_scr[...], preferred_element_type=F32)

    @pl.when(i % tiles_per_seq == 0)
    def _():
        g_scr[0:SUBLANES, :] = jnp.zeros((SUBLANES, g_scr.shape[1]), F32)

    g_scr[SUBLANES:SUBLANES + bm, :] = gate
    conv = gate * cw_ref[FFN_CONV - 1:FFN_CONV, :]
    for s in range(1, FFN_CONV):
        conv = conv + g_scr[SUBLANES - s:SUBLANES - s + bm, :] * cw_ref[FFN_CONV - 1 - s:FFN_CONV - s, :]
    o_ref[...] = (_silu(conv + cb_ref[...]) * val).astype(o_ref.dtype)
    tail = gate[bm - SUBLANES:bm, :]
    tail_ref[...] = tail
    g_scr[0:SUBLANES, :] = tail


def up_ffn_fused(a, w_up, layer, conv_w, conv_b, d_ff, *, bm, bc, seqlen):
    m, kdim = a.shape
    assert m % bm == 0 and seqlen % bm == 0 and d_ff % bc == 0
    nb = d_ff // bc
    return pl.pallas_call(
        functools.partial(_upffn_kernel, bm=bm, tiles_per_seq=seqlen // bm),
        out_shape=(SDS((m, d_ff), BF16), SDS((m // bm * SUBLANES, d_ff), F32)),
        grid=(nb, m // bm),
        in_specs=[pl.BlockSpec((bm, kdim), lambda j, i: (i, 0)),
                  pl.BlockSpec((None, kdim, bc), lambda j, i: (layer, 0, j)),
                  pl.BlockSpec((None, kdim, bc), lambda j, i: (layer, 0, nb + j)),
                  pl.BlockSpec((FFN_CONV, bc), lambda j, i: (0, j)),
                  pl.BlockSpec((1, bc), lambda j, i: (0, j))],
        out_specs=(pl.BlockSpec((bm, bc), lambda j, i: (i, j)), pl.BlockSpec((SUBLANES, bc), lambda j, i: (i, j))),
        scratch_shapes=[pltpu.VMEM((kdim, bc), BF16), pltpu.VMEM((kdim, bc), BF16),
                        pltpu.VMEM((bm + SUBLANES, bc), F32)],
        compiler_params=_cparams(("parallel", "arbitrary")),
        name="up_ffn",
    )(a, w_up, w_up, conv_w.astype(F32), conv_b.reshape(1, d_ff).astype(F32))


def _dprep_kernel(xq_ref, xk_ref, xv_ref, hq_ref, hk_ref, hv_ref, d_ref, cq_ref, ck_ref, cv_ref, aneg_ref, dtb_ref,
                  q_ref, k_ref, v_ref, g_ref, b_ref, xx_scr, gam_scr, beta_scr, *, tb, ls, chunk, valid_from):
    i = pl.program_id(0)
    h = pl.program_id(1)
    row = lax.broadcasted_iota(jnp.int32, (tb, LANES), 0)
    lane = lax.broadcasted_iota(jnp.int32, (tb, LANES), 1)
    pos = (i * tb + row) % ls
    valid = pos >= valid_from

    @pl.when(h == 0)
    def _():
        d = d_ref[...]
        beta = jnp.where(valid, jnp.where(lane < DN_HEADS, _sigmoid(d), 0.0), 0.0)
        sp = d + dtb_ref[...]
        sp = jnp.maximum(sp, 0.0) + jnp.log(1.0 + jnp.exp(-jnp.abs(sp)))
        is_alpha = jnp.where(lane >= DN_HEADS, jnp.where(lane < 2 * DN_HEADS, 1, 0), 0)
        g = jnp.where(valid, jnp.where(is_alpha == 1, aneg_ref[...] * sp, 0.0), 0.0)
        pc = pos % chunk
        s = 1
        while s < chunk:
            g = g + jnp.where(pc >= s, pltpu.roll(g, s, 0), 0.0)
            s *= 2
        gam_scr[...] = g
        beta_scr[...] = beta

    def conv_silu(x_ref, halo_ref, cw_ref):
        xx_scr[0:SUBLANES, :] = halo_ref[...]
        xx_scr[SUBLANES:SUBLANES + tb, :] = x_ref[...]
        acc = x_ref[...] * cw_ref[DN_CONV - 1:DN_CONV, :]
        for s in range(1, DN_CONV):
            tap = xx_scr[SUBLANES - s:SUBLANES - s + tb, :]
            acc = acc + jnp.where(pos >= s, tap, 0.0) * cw_ref[DN_CONV - 1 - s:DN_CONV - s, :]
        return _silu(acc)

    q = conv_silu(xq_ref, hq_ref, cq_ref)
    q = q * lax.rsqrt(jnp.sum(q * q, axis=-1, keepdims=True) + EPS) * (DN_D ** -0.5)
    q_ref[...] = jnp.where(valid, q, 0.0)
    k = conv_silu(xk_ref, hk_ref, ck_ref)
    k = k * lax.rsqrt(jnp.sum(k * k, axis=-1, keepdims=True) + EPS)
    k_ref[...] = jnp.where(valid, k, 0.0)
    v_ref[...] = jnp.where(valid, conv_silu(xv_ref, hv_ref, cv_ref), 0.0)

    sel_r = lax.broadcasted_iota(jnp.int32, (LANES, LANES), 0)
    b_ref[...] = _dot_exact(beta_scr[...], jnp.where(sel_r == h, 1.0, 0.0))
    g_ref[...] = _dot_exact(gam_scr[...], jnp.where(sel_r == h + DN_HEADS, 1.0, 0.0))


def delta_prep(x, xcol0, dt, dcol, conv_w, a_log, dt_bias, *, rows, ls, chunk, valid_from, tb):
    assert rows % tb == 0 and tb % chunk == 0 and tb % SUBLANES == 0
    hb = tb // SUBLANES
    aneg = jnp.zeros((1, LANES), F32).at[0, DN_HEADS:2 * DN_HEADS].set(-jnp.exp(a_log.astype(F32)))
    dtb = jnp.zeros((1, LANES), F32).at[0, DN_HEADS:2 * DN_HEADS].set(dt_bias.astype(F32))

    def xspec(off):
        return pl.BlockSpec((tb, LANES), lambda i, h: (i, xcol0 + off + h))

    def hspec(off):
        return pl.BlockSpec((SUBLANES, LANES), lambda i, h: (jnp.maximum(i * hb - 1, 0), xcol0 + off + h))

    def cspec(off):
        return pl.BlockSpec((DN_CONV, LANES), lambda i, h: (0, off + h))

    ospec = pl.BlockSpec((tb, LANES), lambda i, h: (i, h))
    out = SDS((rows, DN_W), F32)
    return pl.pallas_call(
        functools.partial(_dprep_kernel, tb=tb, ls=ls, chunk=chunk, valid_from=valid_from),
        out_shape=(out,) * 5,
        grid=(rows // tb, DN_HEADS),
        in_specs=[xspec(0), xspec(DN_HEADS), xspec(2 * DN_HEADS), hspec(0), hspec(DN_HEADS), hspec(2 * DN_HEADS),
                  pl.BlockSpec((tb, LANES), lambda i, h: (i, dcol)),
                  cspec(0), cspec(DN_HEADS), cspec(2 * DN_HEADS),
                  pl.BlockSpec((1, LANES), lambda i, h: (0, 0)), pl.BlockSpec((1, LANES), lambda i, h: (0, 0))],
        out_specs=(ospec,) * 5,
        scratch_shapes=[pltpu.VMEM((tb + SUBLANES, LANES), F32), pltpu.VMEM((tb, LANES), F32),
                        pltpu.VMEM((tb, LANES), F32)],
        compiler_params=_cparams(("parallel", "arbitrary")),
        name="delta_prep",
    )(x, x, x, x, x, x, dt, conv_w, conv_w, conv_w, aneg, dtb)


def _drec_kernel(*refs, chunk, heads, nsteps, zero_init):
    if zero_init:
        q_ref, k_ref, v_ref, g_ref, b_ref, z_ref, nw_ref, y_ref, sf_ref, s_scr = refs
        s0_ref = None
    else:
        q_ref, k_ref, v_ref, g_ref, b_ref, z_ref, s0_ref, nw_ref, y_ref, sf_ref, s_scr = refs
    n = pl.program_id(2)

    @pl.when(n == 0)
    def _():
        s_scr[...] = jnp.zeros(s_scr.shape, F32) if zero_init else s0_ref[0]

    ri = lax.broadcasted_iota(jnp.int32, (chunk, chunk), 0)
    ci = lax.broadcasted_iota(jnp.int32, (chunk, chunk), 1)
    incl = ri >= ci
    strict = ri > ci
    eye = jnp.where(ri == ci, 1.0, 0.0)
    levels = int(math.log2(chunk))

    def heads_of(ref):
        return jnp.stack([ref[:, j * DN_D:(j + 1) * DN_D] for j in range(heads)], axis=0)

    q = heads_of(q_ref)
    k = heads_of(k_ref)
    v = heads_of(v_ref)
    gb = heads_of(g_ref)
    bb = heads_of(b_ref)
    s = s_scr[...]
    eg = jnp.exp(gb)
    glast = gb[:, chunk - 1:chunk, :]
    ek = jnp.exp(glast - gb)
    tot = jnp.exp(glast)
    gc = gb[:, :, :chunk]
    bc = bb[:, :, :chunk]
    diff = gc - jnp.swapaxes(gc, 1, 2)
    decay = jnp.where(incl, jnp.exp(jnp.where(incl, diff, 0.0)), 0.0)
    kb = k.astype(BF16)
    kk = _bdot_nt(kb, kb)
    qk = _bdot_nt(q, kb)
    x = jnp.where(strict, -(bc * kk * decay), 0.0)
    p = eye + x
    xb = x.astype(BF16)
    x = _bdot(xb, xb)
    for lvl in range(1, levels):
        xb = x.astype(BF16)
        pn = p + _bdot(xb, p)
        if lvl < levels - 1:
            x = _bdot(xb, xb)
        p = pn
    rhs = jnp.concatenate([bb * eg * k, bb * v], axis=2)
    sol = _bdot(p, rhs)
    wk = sol[:, :, :DN_D]
    ub = sol[:, :, DN_D:]
    sb = s.astype(BF16)
    u = ub - _bdot(wk, sb)
    o = _bdot(eg * q, sb) + _bdot(qk * decay, u)
    s_scr[...] = tot * s + _bdot_tn(ek * k, u)
    ms = jnp.mean(o * o, axis=-1, keepdims=True)
    on = o * lax.rsqrt(ms + EPS) * nw_ref[...]
    for j in range(heads):
        sl = slice(j * DN_D, (j + 1) * DN_D)
        y_ref[:, sl] = (on[j] * _silu(z_ref[:, sl])).astype(y_ref.dtype)

    @pl.when(n == nsteps - 1)
    def _():
        sf_ref[0] = s_scr[...]


def delta_recurrence(q, k, v, g, b, z, zcol0, s0, layer, norm_w, *, nseq, ls, chunk, heads):
    rows = nseq * ls
    nsteps = ls // chunk
    hw = heads * DN_D
    ispec = pl.BlockSpec((chunk, hw), lambda bi, hg, n: (bi * nsteps + n, hg))
    zc = zcol0 // heads
    assert zcol0 % heads == 0
    sspec = pl.BlockSpec((1, heads, DN_D, DN_D), lambda bi, hg, n: (bi, hg, 0, 0))
    in_specs = [ispec, ispec, ispec, ispec, ispec,
                pl.BlockSpec((chunk, hw), lambda bi, hg, n: (bi * nsteps + n, zc + hg))]
    args = [q, k, v, g, b, z]
    if s0 is not None:
        in_specs.append(pl.BlockSpec((None, 1, heads, DN_D, DN_D), lambda bi, hg, n: (layer, bi, hg, 0, 0)))
        args.append(s0)
    in_specs.append(pl.BlockSpec((1, DN_D), lambda bi, hg, n: (0, 0)))
    args.append(norm_w.reshape(1, DN_D))
    return pl.pallas_call(
        functools.partial(_drec_kernel, chunk=chunk, heads=heads, nsteps=nsteps, zero_init=s0 is None),
        out_shape=(SDS((rows, DN_W), BF16), SDS((nseq, DN_HEADS, DN_D, DN_D), F32)),
        grid=(nseq, DN_HEADS // heads, nsteps),
        in_specs=in_specs,
        out_specs=(ispec, sspec),
        scratch_shapes=[pltpu.VMEM((heads, DN_D, DN_D), F32)],
        compiler_params=_cparams(("parallel", "parallel", "arbitrary")),
        name="delta_recurrence",
    )(*args)


def _pool_kernel(x_ref, halo_ref, w_ref, sc_ref, y_ref, xx_scr, *, tb, ls):
    i = pl.program_id(0)
    hist = 2 * SUBLANES
    row = lax.broadcasted_iota(jnp.int32, (tb, POOL_GROUP), 0)
    pos = (i * tb + row) % ls
    xx_scr[0:hist, :] = halo_ref[...]
    xx_scr[hist:hist + tb, :] = x_ref[...]
    for gi, w in enumerate(POOL_WINDOWS):
        cs = slice(gi * POOL_GROUP, (gi + 1) * POOL_GROUP)
        x = x_ref[:, cs]
        tot = x
        for s in range(1, w):
            tap = xx_scr[hist - s:hist - s + tb, cs]
            tot = tot + jnp.where(pos >= s, tap, 0.0)
        cnt = jnp.minimum(pos + 1, w).astype(F32)
        pooled = tot / cnt - x
        y = _dot(pooled, w_ref[gi]) * sc_ref[:, cs]
        y_ref[:, cs] = y.astype(y_ref.dtype)


def pool_mixer(x, xcol, w_pool, scale, *, rows, ls, tb):
    assert rows % tb == 0 and tb % (2 * SUBLANES) == 0
    hb = tb // (2 * SUBLANES)
    return pl.pallas_call(
        functools.partial(_pool_kernel, tb=tb, ls=ls),
        out_shape=SDS((rows, POOL_W), BF16),
        grid=(rows // tb,),
        in_specs=[pl.BlockSpec((tb, POOL_W), lambda i: (i, xcol)),
                  pl.BlockSpec((2 * SUBLANES, POOL_W), lambda i: (jnp.maximum(i * hb - 1, 0), xcol)),
                  pl.BlockSpec((len(POOL_WINDOWS), POOL_GROUP, POOL_GROUP), lambda i: (0, 0, 0)),
                  pl.BlockSpec((1, POOL_W), lambda i: (0, 0))],
        out_specs=pl.BlockSpec((tb, POOL_W), lambda i: (i, 0)),
        scratch_shapes=[pltpu.VMEM((tb + 2 * SUBLANES, POOL_W), F32)],
        compiler_params=_cparams(("arbitrary",)),
        name="pool_mixer",
    )(x, x, w_pool.astype(BF16), scale.reshape(1, POOL_W).astype(F32))


def _group_rms(x, w, width):
    rows = x.shape[0]
    lane = lax.broadcasted_iota(jnp.int32, (rows, LANES), 1)
    lo = lane < SWA_HD
    outs = []
    for c in range(width // LANES):
        t = x[:, c * LANES:(c + 1) * LANES]
        t2 = t * t
        tot = jnp.sum(t2, axis=-1, keepdims=True)
        slo = jnp.sum(jnp.where(lo, t2, 0.0), axis=-1, keepdims=True)
        ms = jnp.where(lo, slo, tot - slo) * (1.0 / SWA_HD)
        outs.append(t * lax.rsqrt(ms + EPS))
    return jnp.concatenate(outs, axis=1) * w


def _swa_prep_kernel(q_ref, k_ref, v_ref, qw_ref, kw_ref, pm_ref, qo_ref, ko_ref, kx_ref, vx_ref):
    qo_ref[...] = _group_rms(q_ref[...], qw_ref[...], SWA_W).astype(qo_ref.dtype)
    kn = _group_rms(k_ref[...], kw_ref[...], SWA_KW)
    ko_ref[...] = kn
    kx_ref[...] = jnp.dot(kn.astype(BF16), pm_ref[...], preferred_element_type=F32).astype(BF16)
    vx_ref[...] = jnp.dot(v_ref[...].astype(BF16), pm_ref[...], preferred_element_type=F32).astype(BF16)


def _two_slot_matrix():
    src = jnp.arange(SWA_KW)
    grp, d = src // SWA_HD, src % SWA_HD
    dst = jnp.arange(4 * SWA_KW)
    dgrp, slot, dd = dst // (4 * SWA_HD), (dst % (4 * SWA_HD)) // SWA_HD, dst % SWA_HD
    keep = (slot == 0) | (slot == 3)
    hit = (grp[:, None] == dgrp[None, :]) & (d[:, None] == dd[None, :]) & keep[None, :]
    return hit.astype(BF16)


def swa_prep(x, qcol, kcol, vcol, q_norm_w, k_norm_w, *, rows, tb):
    qw = (jnp.tile(q_norm_w.astype(F32), SWA_HEADS) * (SWA_HD ** -0.5)).reshape(1, SWA_W)
    kw = jnp.tile(k_norm_w.astype(F32), SWA_KV).reshape(1, SWA_KW)
    wide = pl.BlockSpec((tb, SWA_W), lambda i: (i, 0))
    return pl.pallas_call(
        _swa_prep_kernel,
        out_shape=(SDS((rows, SWA_W), BF16), SDS((rows, SWA_KW), F32), SDS((rows, SWA_W), BF16),
                   SDS((rows, SWA_W), BF16)),
        grid=(rows // tb,),
        in_specs=[pl.BlockSpec((tb, SWA_W), lambda i: (i, qcol)), pl.BlockSpec((tb, SWA_KW), lambda i: (i, kcol)),
                  pl.BlockSpec((tb, SWA_KW), lambda i: (i, vcol)),
                  pl.BlockSpec((1, SWA_W), lambda i: (0, 0)), pl.BlockSpec((1, SWA_KW), lambda i: (0, 0)),
                  pl.BlockSpec((SWA_KW, SWA_W), lambda i: (0, 0))],
        out_specs=(wide, pl.BlockSpec((tb, SWA_KW), lambda i: (i, 0)), wide, wide),
        compiler_params=_cparams(("parallel",)),
        name="swa_prep",
    )(x, x, x, qw, kw, _two_slot_matrix())


def _swa_kernel(sink_ref, q_ref, kp_ref, kc_ref, vp_ref, vc_ref, o_ref, *, qb, first_has_prev):
    g = pl.program_id(1)
    i = pl.program_id(2)
    r = lax.broadcasted_iota(jnp.int32, (qb, WINDOW), 0)
    c = lax.broadcasted_iota(jnp.int32, (qb, WINDOW), 1)
    dist_p = WINDOW + r - c
    dist_c = r - c
    ok_p = dist_p <= WINDOW
    if not first_has_prev:
        ok_p = jnp.logical_and(ok_p, i > 0)
    ok_c = dist_c >= 0
    dpf = dist_p.astype(F32)
    dcf = dist_c.astype(F32)
    for pair in range(2):
        q2 = q_ref[:, pair * LANES:(pair + 1) * LANES]
        acc = jnp.zeros((qb, LANES), F32)
        for e in range(2):
            hl = 2 * pair + e
            hidx = g * (SWA_HEADS // SWA_KV) + hl
            sink = sink_ref[hidx]
            slope = sink_ref[SWA_HEADS + hidx]
            ksl = slice(e * LANES, (e + 1) * LANES)
            sp = lax.dot_general(q2, kp_ref[:, ksl], (((1,), (1,)), ((), ())), preferred_element_type=F32)
            sc = lax.dot_general(q2, kc_ref[:, ksl], (((1,), (1,)), ((), ())), preferred_element_type=F32)
            sp = jnp.where(ok_p, sp - slope * dpf, NEG)
            sc = jnp.where(ok_c, sc - slope * dcf, NEG)
            m = jnp.maximum(jnp.maximum(jnp.max(sp, axis=-1, keepdims=True), jnp.max(sc, axis=-1, keepdims=True)), sink)
            ep = jnp.exp(sp - m)
            ec = jnp.exp(sc - m)
            den = jnp.sum(ep, axis=-1, keepdims=True) + jnp.sum(ec, axis=-1, keepdims=True) + jnp.exp(sink - m)
            inv = 1.0 / den
            acc = acc + jnp.dot((ep * inv).astype(BF16), vp_ref[:, ksl], preferred_element_type=F32)
            acc = acc + jnp.dot((ec * inv).astype(BF16), vc_ref[:, ksl], preferred_element_type=F32)
        o_ref[:, pair * LANES:(pair + 1) * LANES] = acc.astype(o_ref.dtype)


def swa_attention(q, kx_prev, kx_cur, vx_prev, vx_cur, sinks, *, nseq, nq, qb, prev_shift, first_has_prev):
    gw = (SWA_HEADS // SWA_KV) * SWA_HD
    slopes = jnp.exp2(-8.0 * jnp.arange(1, SWA_HEADS + 1, dtype=F32) / SWA_HEADS)
    qspec = pl.BlockSpec((qb, gw), lambda b, g, i, s: (b * nq + i, g))
    pspec = pl.BlockSpec((WINDOW, gw), lambda b, g, i, s: (jnp.maximum(b * nq + i - prev_shift, 0), g))
    cspec = pl.BlockSpec((WINDOW, gw), lambda b, g, i, s: (b * nq + i, g))
    grid_spec = pltpu.PrefetchScalarGridSpec(
        num_scalar_prefetch=1,
        grid=(nseq, SWA_KV, nq),
        in_specs=[qspec, pspec, cspec, pspec, cspec],
        out_specs=qspec,
    )
    return pl.pallas_call(
        functools.partial(_swa_kernel, qb=qb, first_has_prev=first_has_prev),
        out_shape=SDS((nseq * nq * qb, SWA_W), BF16),
        grid_spec=grid_spec,
        compiler_params=_cparams(("parallel", "parallel", "arbitrary")),
        name="swa_attention",
    )(jnp.concatenate([sinks.astype(F32), slopes]), q, kx_prev, kx_cur, vx_prev, vx_cur)


def _two_slot(t):
    rows = t.shape[0]
    t = t.astype(BF16).reshape(rows, SWA_KV, SWA_HD)
    z = jnp.zeros_like(t)
    return jnp.concatenate([t, z, z, t], axis=-1).reshape(rows, SWA_KV * 4 * SWA_HD)


def _merge_kernel(ya_ref, yb_ref, yc_ref, wa_ref, wb_ref, wc_ref, ga_ref, gb_ref, gc_ref, o_ref,
                  wa_scr, wb_scr, wc_scr):
    @pl.when(pl.program_id(1) == 0)
    def _():
        wa_scr[...] = wa_ref[...].astype(BF16)
        wb_scr[...] = wb_ref[...].astype(BF16)
        wc_scr[...] = wc_ref[...].astype(BF16)

    a = jnp.dot(ya_ref[...], wa_scr[...], preferred_element_type=F32)
    b = jnp.dot(yb_ref[...], wb_scr[...], preferred_element_type=F32)
    c = jnp.dot(yc_ref[...], wc_scr[...], preferred_element_type=F32)
    o = _sigmoid(ga_ref[...]) * a + _sigmoid(gb_ref[...]) * b + _sigmoid(gc_ref[...]) * c
    o_ref[...] = o.astype(o_ref.dtype)


def merge_branches(ya, yb, yc, wa, wb, wc, layer, proj, gcol0, d_model, *, bm, bn):
    m = ya.shape[0]
    gb0 = gcol0 // bn
    gstep = d_model // bn
    assert gcol0 % bn == 0 and d_model % bn == 0 and m % bm == 0

    def gspec(t):
        return pl.BlockSpec((bm, bn), lambda j, i: (i, gb0 + t * gstep + j))

    def wspec(kdim):
        return pl.BlockSpec((None, kdim, bn), lambda j, i: (layer, 0, j))

    return pl.pallas_call(
        _merge_kernel,
        out_shape=SDS((m, d_model), BF16),
        grid=(d_model // bn, m // bm),
        in_specs=[pl.BlockSpec((bm, DN_W), lambda j, i: (i, 0)), pl.BlockSpec((bm, POOL_W), lambda j, i: (i, 0)),
                  pl.BlockSpec((bm, SWA_W), lambda j, i: (i, 0)),
                  wspec(DN_W), wspec(POOL_W), wspec(SWA_W),
                  gspec(0), gspec(1), gspec(2)],
        out_specs=pl.BlockSpec((bm, bn), lambda j, i: (i, j)),
        scratch_shapes=[pltpu.VMEM((DN_W, bn), BF16), pltpu.VMEM((POOL_W, bn), BF16), pltpu.VMEM((SWA_W, bn), BF16)],
        compiler_params=_cparams(("parallel", "arbitrary")),
        name="merge_branches",
    )(ya, yb, yc, wa, wb, wc, proj, proj, proj)


def _ffn_act_kernel(g_ref, halo_ref, v_ref, cw_ref, cb_ref, o_ref, xx_scr, *, tb, ls):
    i = pl.program_id(0)
    bc = g_ref.shape[1]
    row = lax.broadcasted_iota(jnp.int32, (tb, bc), 0)
    pos = (i * tb + row) % ls
    xx_scr[0:SUBLANES, :] = halo_ref[...]
    xx_scr[SUBLANES:SUBLANES + tb, :] = g_ref[...]
    acc = g_ref[...] * cw_ref[FFN_CONV - 1:FFN_CONV, :]
    for s in range(1, FFN_CONV):
        tap = xx_scr[SUBLANES - s:SUBLANES - s + tb, :]
        acc = acc + jnp.where(pos >= s, tap, 0.0) * cw_ref[FFN_CONV - 1 - s:FFN_CONV - s, :]
    o_ref[...] = (_silu(acc + cb_ref[...]) * v_ref[...]).astype(o_ref.dtype)


def ffn_activation(gate, gcol0, val, vcol0, conv_w, conv_b, d_ff, *, rows, ls, tb, bc):
    assert rows % tb == 0 and d_ff % bc == 0 and gcol0 % bc == 0 and vcol0 % bc == 0
    hb = tb // SUBLANES
    g0 = gcol0 // bc
    v0 = vcol0 // bc
    return pl.pallas_call(
        functools.partial(_ffn_act_kernel, tb=tb, ls=ls),
        out_shape=SDS((rows, d_ff), BF16),
        grid=(rows // tb, d_ff // bc),
        in_specs=[pl.BlockSpec((tb, bc), lambda i, j: (i, g0 + j)),
                  pl.BlockSpec((SUBLANES, bc), lambda i, j: (jnp.maximum(i * hb - 1, 0), g0 + j)),
                  pl.BlockSpec((tb, bc), lambda i, j: (i, v0 + j)),
                  pl.BlockSpec((FFN_CONV, bc), lambda i, j: (0, j)),
                  pl.BlockSpec((1, bc), lambda i, j: (0, j))],
        out_specs=pl.BlockSpec((tb, bc), lambda i, j: (i, j)),
        scratch_shapes=[pltpu.VMEM((tb + SUBLANES, bc), F32)],
        compiler_params=_cparams(("parallel", "parallel")),
        name="ffn_activation",
    )(gate, gate, val, conv_w.astype(F32), conv_b.reshape(1, d_ff).astype(F32))


def _in_layout(d_model):
    offa = {"dq": 0, "dk": DN_W, "dv": 2 * DN_W, "dz": 3 * DN_W}
    offb = {}
    c = 0
    for name, width in (("pin", POOL_W), ("sq", SWA_W), ("sk", SWA_KW), ("sv", SWA_KW),
                        ("ga", d_model), ("gb", d_model), ("gc", d_model)):
        offb[name] = c
        c += width
    return offa, offb, c


def _ext(hist, new, ls):
    nseq, hr, c = hist.shape
    ln = new.shape[1]
    z = jnp.zeros((nseq, ls - hr - ln, c), new.dtype)
    return jnp.concatenate([z, hist.astype(new.dtype), new], axis=1).reshape(nseq * ls, c)


def _layer(x, nseq, seqlen, states, layer, wts, cfg):
    (n1, w_in, dconv_w, a_log, dt_bias, dn_norm_w, pool_w, pool_scale, q_norm_w, k_norm_w, sinks,
     w_ba, w_bb, w_bc, w_out, n2, w_up, fconv_w, fconv_b, w_down_bf, state_delta) = wts
    rows, d_model = x.shape
    d_ff = w_down_bf.shape[1]
    offa, offb, nb = cfg["offa"], cfg["offb"], cfg["nb"]
    fresh = states is None
    bm = cfg["bm"](seqlen if fresh else rows)

    h = rmsnorm_bf16(x, n1)
    pa = matmul_wres(h, w_in, layer, 0, 4 * DN_W, bm=bm, bn=cfg["bn_in"], name="in_proj_a")
    pc = matmul_wres(h, w_in, layer, 4 * DN_W, LANES, bm=bm, bn=LANES, name="in_proj_c")
    pb = matmul_wres(h, w_in, layer, 4 * DN_W, nb, bm=bm, bn=cfg["bn_in"], shift=2 * DN_HEADS, name="in_proj_b")
    pa3 = pa.reshape(nseq, seqlen, -1)
    pb3 = pb.reshape(nseq, seqlen, -1)

    def cols_a(name, width):
        return pa3[:, :, offa[name]:offa[name] + width]

    def cols_b(name, width):
        return pb3[:, :, offb[name]:offb[name] + width]

    if fresh:
        ls, chunk, vfrom = seqlen, cfg["chunk"], 0
        tbp = _pick(seqlen, cfg["tb_prep"], chunk)
        q, k, v, g, b = delta_prep(pa, 0, pc, 0, dconv_w, a_log, dt_bias,
                                   rows=rows, ls=ls, chunk=chunk, valid_from=0, tb=tbp)
        ya, s_new = delta_recurrence(q, k, v, g, b, pa, offa["dz"] // LANES, None, layer, dn_norm_w,
                                     nseq=nseq, ls=ls, chunk=chunk, heads=cfg["heads"])
        conv_new = cols_a("dq", 3 * DN_W)[:, seqlen - (DN_CONV - 1):]
    else:
        conv_h, pool_h, k_h, v_h, ffn_h = states
        ls = SAMPLE_SEQ_ROWS
        vfrom = ls - seqlen
        xe = _ext(conv_h, cols_a("dq", 3 * DN_W), ls)
        de = _ext(jnp.zeros((nseq, 0, LANES), F32), pc.reshape(nseq, seqlen, LANES), ls)
        ze = _ext(jnp.zeros((nseq, 0, DN_W), F32), cols_a("dz", DN_W), ls)
        erows = nseq * ls
        q, k, v, g, b = delta_prep(xe, 0, de, 0, dconv_w, a_log, dt_bias,
                                   rows=erows, ls=ls, chunk=ls, valid_from=vfrom, tb=_pick(erows, 1024, ls))
        ya_e, s_new = delta_recurrence(q, k, v, g, b, ze, 0, state_delta, layer, dn_norm_w,
                                       nseq=nseq, ls=ls, chunk=ls, heads=cfg["heads"])
        ya = ya_e.reshape(nseq, ls, DN_W)[:, vfrom:].reshape(rows, DN_W)
        conv_new = xe.reshape(nseq, ls, 3 * DN_W)[:, ls - (DN_CONV - 1):]

    if fresh:
        yb = pool_mixer(pb, offb["pin"] // POOL_W, pool_w, pool_scale, rows=rows, ls=seqlen,
                        tb=_pick(seqlen, 512, 2 * SUBLANES))
        pool_new = cols_b("pin", POOL_W)[:, seqlen - POOL_HIST:]
    else:
        pe = _ext(pool_h, cols_b("pin", POOL_W), SAMPLE_SEQ_ROWS)
        erows = nseq * SAMPLE_SEQ_ROWS
        yb_e = pool_mixer(pe, 0, pool_w, pool_scale, rows=erows, ls=SAMPLE_SEQ_ROWS, tb=_pick(erows, 512, SAMPLE_SEQ_ROWS))
        yb = yb_e.reshape(nseq, SAMPLE_SEQ_ROWS, POOL_W)[:, SAMPLE_SEQ_ROWS - seqlen:].reshape(rows, POOL_W)
        pool_new = pe.reshape(nseq, SAMPLE_SEQ_ROWS, POOL_W)[:, SAMPLE_SEQ_ROWS - POOL_HIST:]

    qn, kn, kx, vx = swa_prep(pb, offb["sq"] // SWA_W, offb["sk"] // SWA_KW, offb["sv"] // SWA_KW,
                              q_norm_w, k_norm_w, rows=rows, tb=_pick(rows, 512, SUBLANES))
    vraw = cols_b("sv", SWA_KW)
    kn3 = kn.reshape(nseq, seqlen, SWA_KW)
    if fresh:
        yc = swa_attention(qn, kx, kx, vx, vx, sinks, nseq=nseq, nq=seqlen // WINDOW, qb=WINDOW,
                           prev_shift=1, first_has_prev=False)
        keep = min(WINDOW, seqlen)
        k_new = kn3[:, seqlen - keep:].reshape(nseq, keep, SWA_KV, SWA_HD)
        v_new = vraw[:, seqlen - keep:].reshape(nseq, keep, SWA_KV, SWA_HD)
    else:
        wb = k_h.shape[1]
        qpad = jnp.zeros((nseq, SUBLANES - seqlen, SWA_W), BF16)
        qe = jnp.concatenate([qn.reshape(nseq, seqlen, SWA_W), qpad], axis=1).reshape(nseq * SUBLANES, SWA_W)
        kpad = jnp.zeros((nseq, WINDOW - seqlen, SWA_W), BF16)
        kc = jnp.concatenate([kx.reshape(nseq, seqlen, SWA_W), kpad], axis=1).reshape(nseq * WINDOW, SWA_W)
        vc = jnp.concatenate([vx.reshape(nseq, seqlen, SWA_W), kpad], axis=1).reshape(nseq * WINDOW, SWA_W)
        kp = _two_slot(k_h.reshape(nseq * wb, SWA_KW))
        vp = _two_slot(v_h.reshape(nseq * wb, SWA_KW))
        yc_e = swa_attention(qe, kp, kc, vp, vc, sinks, nseq=nseq, nq=1, qb=SUBLANES,
                             prev_shift=0, first_has_prev=True)
        yc = yc_e.reshape(nseq, SUBLANES, SWA_W)[:, :seqlen].reshape(rows, SWA_W)
        k_new = jnp.concatenate([k_h, kn3.reshape(nseq, seqlen, SWA_KV, SWA_HD).astype(k_h.dtype)], axis=1)[:, -wb:]
        v_new = jnp.concatenate([v_h, vraw.reshape(nseq, seqlen, SWA_KV, SWA_HD).astype(v_h.dtype)], axis=1)[:, -wb:]

    merged = merge_branches(ya, yb, yc, w_ba, w_bb, w_bc, layer, pb, offb["ga"], d_model, bm=bm, bn=cfg["bn_merge"])
    x = matmul_wres(merged, w_out, layer, 0, d_model, bm=bm, bn=cfg["bn_out"], residual=x, name="out_proj")

    h2 = rmsnorm_bf16(x, n2)
    bc = cfg["bc_ffn"]
    if fresh:
        hmid, tail = up_ffn_fused(h2, w_up, layer, fconv_w, fconv_b, d_ff, bm=bm, bc=bc, seqlen=seqlen)
        tps = seqlen // bm
        tail4 = tail.reshape(nseq, tps, SUBLANES, d_ff)
        ffn_new = tail4[:, tps - 1, SUBLANES - (FFN_CONV - 1):]
    else:
        up = matmul_wres(h2, w_up, layer, 0, 2 * d_ff, bm=bm, bn=cfg["bn_up"], name="up_proj")
        up3 = up.reshape(nseq, seqlen, 2 * d_ff)
        ls = SUBLANES
        ge = _ext(ffn_h, up3[:, :, :d_ff], ls)
        ve = _ext(jnp.zeros((nseq, 0, d_ff), F32), up3[:, :, d_ff:], ls)
        hm_e = ffn_activation(ge, 0, ve, 0, fconv_w, fconv_b, d_ff, rows=nseq * ls, ls=ls, tb=nseq * ls, bc=bc)
        hmid = hm_e.reshape(nseq, ls, d_ff)[:, ls - seqlen:].reshape(rows, d_ff)
        ffn_new = ge.reshape(nseq, ls, d_ff)[:, ls - (FFN_CONV - 1):]
    x = matmul(hmid, w_down_bf, layer, bm=bm, bn=cfg["bn_down"], bk=cfg["bk_down"], residual=x, name="down_proj")
    return x, (s_new, conv_new, pool_new, k_new, v_new, ffn_new)


def _config(d_model, d_ff):
    offa, offb, nb = _in_layout(d_model)
    bn_in = 512
    assert nb % bn_in == 0 and d_model % bn_in == 0
    return {
        "offa": offa, "offb": offb, "nb": nb,
        "bm": lambda rows: _pick(rows, 1024),
        "bn_in": bn_in,
        "bn_merge": bn_in,
        "bn_out": _pick(d_model, 512, LANES),
        "bn_up": _pick(2 * d_ff, 512, LANES),
        "bn_down": _pick(d_model, 512, LANES),
        "bk_down": _pick(d_ff, 5504, LANES),
        "bc_ffn": _pick(d_ff, 256, LANES),
        "chunk": DELTA_CHUNK,
        "tb_prep": 1024,
        "heads": REC_HEADS,
    }


def kernel(x_prompt, x_sample, state_delta, state_dconv, state_pool, cache_swa_k, cache_swa_v, state_ffn_conv, norm1_w, w_in, dconv_w, dn_a_log, dn_dt_bias, dn_norm_w, pool_w, pool_scale, q_norm_w, k_norm_w, sinks, w_branch_a, w_branch_b, w_branch_c, w_out, norm2_w, w_up, ffn_conv_w, ffn_conv_b, w_down):
    depth = w_in.shape[0]
    bp, lp, d_model = x_prompt.shape
    bs, lsm, _ = x_sample.shape
    d_ff = w_down.shape[1]
    cfg = _config(d_model, d_ff)
    xp = x_prompt.reshape(bp * lp, d_model)
    xs = x_sample.reshape(bs * lsm, d_model)
    w_down_bf = cast_bf16(w_down, br=_pick(d_ff, 512, SUBLANES))
    outs_p = [[] for _ in range(6)]
    outs_s = [[] for _ in range(6)]
    for l in range(depth):
        wts = (norm1_w[l], w_in, dconv_w[l].astype(F32), dn_a_log[l], dn_dt_bias[l],
               dn_norm_w[l].astype(F32), pool_w[l], pool_scale[l], q_norm_w[l], k_norm_w[l], sinks[l],
               w_branch_a, w_branch_b, w_branch_c, w_out, norm2_w[l], w_up, ffn_conv_w[l], ffn_conv_b[l],
               w_down_bf, state_delta.astype(F32))
        xp, st_p = _layer(xp, bp, lp, None, l, wts, cfg)
        st_in = (state_dconv[l], state_pool[l], cache_swa_k[l], cache_swa_v[l], state_ffn_conv[l])
        xs, st_s = _layer(xs, bs, lsm, st_in, l, wts, cfg)
        for lst, a in zip(outs_p, st_p):
            lst.append(a)
        for lst, a in zip(outs_s, st_s):
            lst.append(a)
    dt = x_prompt.dtype
    res = [xp.reshape(bp, lp, d_model), xs.reshape(bs, lsm, d_model)]
    for outs in (outs_p, outs_s):
        res.append(jnp.stack(outs[0]).astype(state_delta.dtype))
        for t in outs[1:]:
            res.append(jnp.stack(t).astype(dt))
    return tuple(res)
```

```python
import functools
import math

import jax
import jax.numpy as jnp
from jax import lax
from jax.experimental import pallas as pl
from jax.experimental.pallas import tpu as pltpu

F32 = jnp.float32
BF16 = jnp.bfloat16
EPS = 1e-6
NEG = -1e30

LANES = 128
SUBLANES = 8
VMEM_LIMIT = 56 * 1024 * 1024

DN_HEADS = 16
DN_D = 128
DN_W = DN_HEADS * DN_D
DN_CONV = 4
POOL_WINDOWS = (2, 4, 8, 16)
POOL_GROUP = 256
POOL_W = 1024
POOL_HIST = 15
SWA_HEADS = 16
SWA_KV = 4
SWA_HD = 64
SWA_W = SWA_HEADS * SWA_HD
SWA_KW = SWA_KV * SWA_HD
WINDOW = 128
FFN_CONV = 3

DELTA_CHUNK = 128
SAMPLE_SEQ_ROWS = 32
REC_HEADS = 8

SDS = jax.ShapeDtypeStruct


def _cparams(sem, vmem=VMEM_LIMIT):
    return pltpu.CompilerParams(dimension_semantics=sem, vmem_limit_bytes=vmem)


def _pick(n, target, mult=16):
    best = None
    for d in range(mult, min(n, target) + 1, mult):
        if n % d == 0:
            best = d
    assert best is not None, (n, target, mult)
    return best


def _sigmoid(x):
    return 1.0 / (1.0 + jnp.exp(-x))


def _silu(x):
    return x * _sigmoid(x)


def _dot(a, b):
    return jnp.dot(a.astype(BF16), b.astype(BF16), preferred_element_type=F32)


def _dot_nt(a, b):
    return lax.dot_general(a.astype(BF16), b.astype(BF16), (((1,), (1,)), ((), ())),
                           preferred_element_type=F32)


def _dot_tn(a, b):
    return lax.dot_general(a.astype(BF16), b.astype(BF16), (((0,), (0,)), ((), ())),
                           preferred_element_type=F32)


def _bdot(a, b):
    return lax.dot_general(a.astype(BF16), b.astype(BF16), (((2,), (1,)), ((0,), (0,))),
                           preferred_element_type=F32)


def _bdot_nt(a, b):
    return lax.dot_general(a.astype(BF16), b.astype(BF16), (((2,), (2,)), ((0,), (0,))),
                           preferred_element_type=F32)


def _bdot_tn(a, b):
    return lax.dot_general(a.astype(BF16), b.astype(BF16), (((1,), (1,)), ((0,), (0,))),
                           preferred_element_type=F32)


def _dot_exact(a, b):
    a1 = a.astype(BF16)
    r1 = a - a1.astype(F32)
    a2 = r1.astype(BF16)
    a3 = (r1 - a2.astype(F32)).astype(BF16)
    bb = b.astype(BF16)
    out = jnp.dot(a3, bb, preferred_element_type=F32)
    out = out + jnp.dot(a2, bb, preferred_element_type=F32)
    return out + jnp.dot(a1, bb, preferred_element_type=F32)


def _rmsnorm_kernel(x_ref, w_ref, o_ref):
    x = x_ref[...]
    ms = jnp.mean(x * x, axis=-1, keepdims=True)
    o_ref[...] = (x * lax.rsqrt(ms + EPS) * w_ref[...]).astype(o_ref.dtype)


def rmsnorm_bf16(x, w):
    rows, d = x.shape
    br = _pick(rows, 256, SUBLANES)
    return pl.pallas_call(
        _rmsnorm_kernel,
        out_shape=SDS((rows, d), BF16),
        grid=(rows // br,),
        in_specs=[pl.BlockSpec((br, d), lambda i: (i, 0)), pl.BlockSpec((1, d), lambda i: (0, 0))],
        out_specs=pl.BlockSpec((br, d), lambda i: (i, 0)),
        compiler_params=_cparams(("parallel",)),
        name="rmsnorm",
    )(x, w.reshape(1, d))


def _mm_kernel(*refs, nk, has_res):
    if has_res:
        a_ref, w_ref, r_ref, o_ref = refs
    else:
        a_ref, w_ref, o_ref = refs
        r_ref = None
    acc = jnp.dot(a_ref[...], w_ref[...], preferred_element_type=F32)
    if nk == 1:
        if has_res:
            acc = acc + r_ref[...]
        o_ref[...] = acc.astype(o_ref.dtype)
    else:
        k = pl.program_id(2)

        @pl.when(k == 0)
        def _():
            o_ref[...] = (acc + r_ref[...]) if has_res else acc

        @pl.when(k > 0)
        def _():
            o_ref[...] += acc


def matmul(a, w, layer, *, bm, bn, bk=None, residual=None, out_dtype=F32, name="matmul"):
    m, kdim = a.shape
    n = w.shape[2]
    bk = kdim if bk is None else bk
    nk = kdim // bk
    assert m % bm == 0 and n % bn == 0 and kdim % bk == 0
    assert nk == 1 or out_dtype == F32
    in_specs = [pl.BlockSpec((bm, bk), lambda i, j, k: (i, k)),
                pl.BlockSpec((None, bk, bn), lambda i, j, k: (layer, k, j))]
    args = [a, w]
    if residual is not None:
        in_specs.append(pl.BlockSpec((bm, bn), lambda i, j, k: (i, j)))
        args.append(residual)
    return pl.pallas_call(
        functools.partial(_mm_kernel, nk=nk, has_res=residual is not None),
        out_shape=SDS((m, n), out_dtype),
        grid=(m // bm, n // bn, nk),
        in_specs=in_specs,
        out_specs=pl.BlockSpec((bm, bn), lambda i, j, k: (i, j)),
        compiler_params=_cparams(("parallel", "parallel", "arbitrary")),
        name=name,
    )(*args)


def _cast_kernel(x_ref, o_ref):
    o_ref[...] = x_ref[...].astype(o_ref.dtype)


CAST_BLOCK_ELEMS = 2 * 1024 * 1024


def cast_bf16(w):
    depth, kdim, n = w.shape
    bc = _pick(n, 8192, LANES)
    br = _pick(kdim, max(SUBLANES, CAST_BLOCK_ELEMS // bc), SUBLANES)
    spec = pl.BlockSpec((1, br, bc), lambda l, i, j: (l, i, j))
    return pl.pallas_call(
        _cast_kernel,
        out_shape=SDS(w.shape, BF16),
        grid=(depth, kdim // br, n // bc),
        in_specs=[spec],
        out_specs=spec,
        compiler_params=_cparams(("parallel", "parallel", "parallel")),
        name="cast_bf16",
    )(w)


def _w_in_kernel(w1_ref, w2_ref, o_ref, *, n_a, n_b, shift):
    j = pl.program_id(1)

    def emit(rows):
        o_ref[...] = rows.T.astype(o_ref.dtype)

    @pl.when(jnp.logical_or(j < n_a, j == n_a + n_b))
    def _():
        emit(w1_ref[...])

    @pl.when(jnp.logical_and(j >= n_a, j < n_a + n_b))
    def _():
        emit(jnp.concatenate([w1_ref[shift:, :], w2_ref[:shift, :]], axis=0))

    @pl.when(j > n_a + n_b)
    def _():
        o_ref[...] = jnp.zeros(o_ref.shape, o_ref.dtype)


def prep_w_in(w_in, d_model):
    depth, kdim, n_raw = w_in.shape
    offs, n_pad = _in_layout(d_model)
    n_a = 4 * DN_W // LANES
    n_b = (offs["ba"] - 4 * DN_W) // LANES
    shift = 2 * DN_HEADS
    assert n_raw == 4 * DN_W + shift + n_b * LANES and shift % SUBLANES == 0
    wt = jnp.swapaxes(w_in, 1, 2)

    def src(j):
        return jnp.where(j < n_a + n_b, j, n_a)

    return pl.pallas_call(
        functools.partial(_w_in_kernel, n_a=n_a, n_b=n_b, shift=shift),
        out_shape=SDS((depth, kdim, n_pad), BF16),
        grid=(depth, n_pad // LANES),
        in_specs=[pl.BlockSpec((None, LANES, kdim), lambda l, j: (l, src(j), 0)),
                  pl.BlockSpec((None, LANES, kdim), lambda l, j: (l, jnp.minimum(src(j) + 1, n_a + n_b), 0))],
        out_specs=pl.BlockSpec((None, kdim, LANES), lambda l, j: (l, 0, j)),
        compiler_params=_cparams(("parallel", "parallel")),
        name="prep_w_in",
    )(wt, wt)


def _upffn_kernel(a_ref, wg_ref, wv_ref, cw_ref, cb_ref, o_ref, tail_ref, g_scr, carry_scr, *, bm, tiles_per_seq):
    i = pl.program_id(0)
    j = pl.program_id(1)
    a = a_ref[...]
    gate = jnp.dot(a, wg_ref[...], preferred_element_type=F32)
    val = jnp.dot(a, wv_ref[...], preferred_element_type=F32)

    @pl.when(i % tiles_per_seq == 0)
    def _():
        g_scr[0:SUBLANES, :] = jnp.zeros((SUBLANES, g_scr.shape[1]), F32)

    @pl.when(i % tiles_per_seq != 0)
    def _():
        g_scr[0:SUBLANES, :] = carry_scr[j]

    g_scr[SUBLANES:SUBLANES + bm, :] = gate
    conv = gate * cw_ref[FFN_CONV - 1:FFN_CONV, :]
    for s in range(1, FFN_CONV):
        conv = conv + g_scr[SUBLANES - s:SUBLANES - s + bm, :] * cw_ref[FFN_CONV - 1 - s:FFN_CONV - s, :]
    o_ref[...] = (_silu(conv + cb_ref[...]) * val).astype(o_ref.dtype)
    tail = gate[bm - SUBLANES:bm, :]
    tail_ref[...] = tail
    carry_scr[j] = tail


def up_ffn_fused(a, w_up, layer, conv_w, conv_b, d_ff, *, bm, bc, seqlen):
    m, kdim = a.shape
    assert m % bm == 0 and seqlen % bm == 0 and d_ff % bc == 0
    nb = d_ff // bc
    return pl.pallas_call(
        functools.partial(_upffn_kernel, bm=bm, tiles_per_seq=seqlen // bm),
        out_shape=(SDS((m, d_ff), BF16), SDS((m // bm * SUBLANES, d_ff), F32)),
        grid=(m // bm, nb),
        in_specs=[pl.BlockSpec((bm, kdim), lambda i, j: (i, 0)),
                  pl.BlockSpec((None, kdim, bc), lambda i, j: (layer, 0, j)),
                  pl.BlockSpec((None, kdim, bc), lambda i, j: (layer, 0, nb + j)),
                  pl.BlockSpec((FFN_CONV, bc), lambda i, j: (0, j)),
                  pl.BlockSpec((1, bc), lambda i, j: (0, j))],
        out_specs=(pl.BlockSpec((bm, bc), lambda i, j: (i, j)), pl.BlockSpec((SUBLANES, bc), lambda i, j: (i, j))),
        scratch_shapes=[pltpu.VMEM((bm + SUBLANES, bc), F32), pltpu.VMEM((nb, SUBLANES, bc), F32)],
        compiler_params=_cparams(("arbitrary", "arbitrary")),
        name="up_ffn",
    )(a, w_up, w_up, conv_w.astype(F32), conv_b.reshape(1, d_ff).astype(F32))


def _dprep_kernel(xq_ref, xk_ref, xv_ref, hq_ref, hk_ref, hv_ref, d_ref, cq_ref, ck_ref, cv_ref, aneg_ref, dtb_ref,
                  q_ref, k_ref, v_ref, g_ref, b_ref, xx_scr, gam_scr, beta_scr, *, tb, ls, chunk, valid_from):
    i = pl.program_id(0)
    h = pl.program_id(1)
    row = lax.broadcasted_iota(jnp.int32, (tb, LANES), 0)
    lane = lax.broadcasted_iota(jnp.int32, (tb, LANES), 1)
    pos = (i * tb + row) % ls
    valid = pos >= valid_from

    @pl.when(h == 0)
    def _():
        d = d_ref[...]
        beta = jnp.where(valid, jnp.where(lane < DN_HEADS, _sigmoid(d), 0.0), 0.0)
        sp = d + dtb_ref[...]
        sp = jnp.maximum(sp, 0.0) + jnp.log(1.0 + jnp.exp(-jnp.abs(sp)))
        is_alpha = jnp.where(lane >= DN_HEADS, jnp.where(lane < 2 * DN_HEADS, 1, 0), 0)
        g = jnp.where(valid, jnp.where(is_alpha == 1, aneg_ref[...] * sp, 0.0), 0.0)
        pc = pos % chunk
        s = 1
        while s < chunk:
            g = g + jnp.where(pc >= s, pltpu.roll(g, s, 0), 0.0)
            s *= 2
        gam_scr[...] = g
        beta_scr[...] = beta

    def conv_silu(x_ref, halo_ref, cw_ref):
        xx_scr[0:SUBLANES, :] = halo_ref[...]
        xx_scr[SUBLANES:SUBLANES + tb, :] = x_ref[...]
        acc = x_ref[...] * cw_ref[DN_CONV - 1:DN_CONV, :]
        for s in range(1, DN_CONV):
            tap = xx_scr[SUBLANES - s:SUBLANES - s + tb, :]
            acc = acc + jnp.where(pos >= s, tap, 0.0) * cw_ref[DN_CONV - 1 - s:DN_CONV - s, :]
        return _silu(acc)

    q = conv_silu(xq_ref, hq_ref, cq_ref)
    q = q * lax.rsqrt(jnp.sum(q * q, axis=-1, keepdims=True) + EPS) * (DN_D ** -0.5)
    q_ref[...] = jnp.where(valid, q, 0.0)
    k = conv_silu(xk_ref, hk_ref, ck_ref)
    k = k * lax.rsqrt(jnp.sum(k * k, axis=-1, keepdims=True) + EPS)
    k_ref[...] = jnp.where(valid, k, 0.0)
    v_ref[...] = jnp.where(valid, conv_silu(xv_ref, hv_ref, cv_ref), 0.0)

    sel_r = lax.broadcasted_iota(jnp.int32, (LANES, LANES), 0)
    b_ref[...] = _dot_exact(beta_scr[...], jnp.where(sel_r == h, 1.0, 0.0))
    g_ref[...] = _dot_exact(gam_scr[...], jnp.where(sel_r == h + DN_HEADS, 1.0, 0.0))


def delta_prep(x, xcol0, dt, dcol, conv_w, a_log, dt_bias, *, rows, ls, chunk, valid_from, tb):
    assert rows % tb == 0 and tb % chunk == 0 and tb % SUBLANES == 0
    hb = tb // SUBLANES
    aneg = jnp.zeros((1, LANES), F32).at[0, DN_HEADS:2 * DN_HEADS].set(-jnp.exp(a_log.astype(F32)))
    dtb = jnp.zeros((1, LANES), F32).at[0, DN_HEADS:2 * DN_HEADS].set(dt_bias.astype(F32))

    def xspec(off):
        return pl.BlockSpec((tb, LANES), lambda i, h: (i, xcol0 + off + h))

    def hspec(off):
        return pl.BlockSpec((SUBLANES, LANES), lambda i, h: (jnp.maximum(i * hb - 1, 0), xcol0 + off + h))

    def cspec(off):
        return pl.BlockSpec((DN_CONV, LANES), lambda i, h: (0, off + h))

    ospec = pl.BlockSpec((tb, LANES), lambda i, h: (i, h))
    out = SDS((rows, DN_W), F32)
    return pl.pallas_call(
        functools.partial(_dprep_kernel, tb=tb, ls=ls, chunk=chunk, valid_from=valid_from),
        out_shape=(out,) * 5,
        grid=(rows // tb, DN_HEADS),
        in_specs=[xspec(0), xspec(DN_HEADS), xspec(2 * DN_HEADS), hspec(0), hspec(DN_HEADS), hspec(2 * DN_HEADS),
                  pl.BlockSpec((tb, LANES), lambda i, h: (i, dcol)),
                  cspec(0), cspec(DN_HEADS), cspec(2 * DN_HEADS),
                  pl.BlockSpec((1, LANES), lambda i, h: (0, 0)), pl.BlockSpec((1, LANES), lambda i, h: (0, 0))],
        out_specs=(ospec,) * 5,
        scratch_shapes=[pltpu.VMEM((tb + SUBLANES, LANES), F32), pltpu.VMEM((tb, LANES), F32),
                        pltpu.VMEM((tb, LANES), F32)],
        compiler_params=_cparams(("parallel", "arbitrary")),
        name="delta_prep",
    )(x, x, x, x, x, x, dt, conv_w, conv_w, conv_w, aneg, dtb)


def _drec_kernel(*refs, chunk, heads, nsteps, zero_init):
    if zero_init:
        q_ref, k_ref, v_ref, g_ref, b_ref, z_ref, nw_ref, y_ref, sf_ref, s_scr = refs
        s0_ref = None
    else:
        q_ref, k_ref, v_ref, g_ref, b_ref, z_ref, s0_ref, nw_ref, y_ref, sf_ref, s_scr = refs
    n = pl.program_id(2)

    @pl.when(n == 0)
    def _():
        s_scr[...] = jnp.zeros(s_scr.shape, F32) if zero_init else s0_ref[0]

    ri = lax.broadcasted_iota(jnp.int32, (chunk, chunk), 0)
    ci = lax.broadcasted_iota(jnp.int32, (chunk, chunk), 1)
    incl = ri >= ci
    strict = ri > ci
    eye = jnp.where(ri == ci, 1.0, 0.0)
    levels = int(math.log2(chunk))

    def heads_of(ref):
        return jnp.stack([ref[:, j * DN_D:(j + 1) * DN_D] for j in range(heads)], axis=0)

    q = heads_of(q_ref)
    k = heads_of(k_ref)
    v = heads_of(v_ref)
    gb = heads_of(g_ref)
    bb = heads_of(b_ref)
    s = s_scr[...]
    eg = jnp.exp(gb)
    glast = gb[:, chunk - 1:chunk, :]
    ek = jnp.exp(glast - gb)
    tot = jnp.exp(glast)
    gc = gb[:, :, :chunk]
    bc = bb[:, :, :chunk]
    diff = gc - jnp.swapaxes(gc, 1, 2)
    decay = jnp.where(incl, jnp.exp(jnp.where(incl, diff, 0.0)), 0.0)
    kb = k.astype(BF16)
    kk = _bdot_nt(kb, kb)
    qk = _bdot_nt(q, kb)
    x = jnp.where(strict, -(bc * kk * decay), 0.0)
    p = eye + x
    xb = x.astype(BF16)
    x = _bdot(xb, xb)
    for lvl in range(1, levels):
        xb = x.astype(BF16)
        pn = p + _bdot(xb, p)
        if lvl < levels - 1:
            x = _bdot(xb, xb)
        p = pn
    rhs = jnp.concatenate([bb * eg * k, bb * v], axis=2)
    sol = _bdot(p, rhs)
    wk = sol[:, :, :DN_D]
    ub = sol[:, :, DN_D:]
    sb = s.astype(BF16)
    u = ub - _bdot(wk, sb)
    o = _bdot(eg * q, sb) + _bdot(qk * decay, u)
    s_scr[...] = tot * s + _bdot_tn(ek * k, u)
    ms = jnp.mean(o * o, axis=-1, keepdims=True)
    on = o * lax.rsqrt(ms + EPS) * nw_ref[...]
    for j in range(heads):
        sl = slice(j * DN_D, (j + 1) * DN_D)
        y_ref[:, sl] = (on[j] * _silu(z_ref[:, sl])).astype(y_ref.dtype)

    @pl.when(n == nsteps - 1)
    def _():
        sf_ref[0] = s_scr[...]


def delta_recurrence(q, k, v, g, b, z, zcol0, s0, layer, norm_w, *, nseq, ls, chunk, heads):
    rows = nseq * ls
    nsteps = ls // chunk
    hw = heads * DN_D
    ispec = pl.BlockSpec((chunk, hw), lambda bi, hg, n: (bi * nsteps + n, hg))
    zc = zcol0 // heads
    assert zcol0 % heads == 0
    sspec = pl.BlockSpec((1, heads, DN_D, DN_D), lambda bi, hg, n: (bi, hg, 0, 0))
    in_specs = [ispec, ispec, ispec, ispec, ispec,
                pl.BlockSpec((chunk, hw), lambda bi, hg, n: (bi * nsteps + n, zc + hg))]
    args = [q, k, v, g, b, z]
    if s0 is not None:
        in_specs.append(pl.BlockSpec((None, 1, heads, DN_D, DN_D), lambda bi, hg, n: (layer, bi, hg, 0, 0)))
        args.append(s0)
    in_specs.append(pl.BlockSpec((1, DN_D), lambda bi, hg, n: (0, 0)))
    args.append(norm_w.reshape(1, DN_D))
    return pl.pallas_call(
        functools.partial(_drec_kernel, chunk=chunk, heads=heads, nsteps=nsteps, zero_init=s0 is None),
        out_shape=(SDS((rows, DN_W), BF16), SDS((nseq, DN_HEADS, DN_D, DN_D), F32)),
        grid=(nseq, DN_HEADS // heads, nsteps),
        in_specs=in_specs,
        out_specs=(ispec, sspec),
        scratch_shapes=[pltpu.VMEM((heads, DN_D, DN_D), F32)],
        compiler_params=_cparams(("parallel", "parallel", "arbitrary")),
        name="delta_recurrence",
    )(*args)


def _pool_kernel(x_ref, halo_ref, w_ref, sc_ref, y_ref, xx_scr, *, tb, ls):
    i = pl.program_id(0)
    hist = 2 * SUBLANES
    row = lax.broadcasted_iota(jnp.int32, (tb, POOL_GROUP), 0)
    pos = (i * tb + row) % ls
    xx_scr[0:hist, :] = halo_ref[...]
    xx_scr[hist:hist + tb, :] = x_ref[...]
    for gi, w in enumerate(POOL_WINDOWS):
        cs = slice(gi * POOL_GROUP, (gi + 1) * POOL_GROUP)
        x = x_ref[:, cs]
        tot = x
        for s in range(1, w):
            tap = xx_scr[hist - s:hist - s + tb, cs]
            tot = tot + jnp.where(pos >= s, tap, 0.0)
        cnt = jnp.minimum(pos + 1, w).astype(F32)
        pooled = tot / cnt - x
        y = _dot(pooled, w_ref[gi]) * sc_ref[:, cs]
        y_ref[:, cs] = y.astype(y_ref.dtype)


def pool_mixer(x, xcol, w_pool, scale, *, rows, ls, tb):
    assert rows % tb == 0 and tb % (2 * SUBLANES) == 0
    hb = tb // (2 * SUBLANES)
    return pl.pallas_call(
        functools.partial(_pool_kernel, tb=tb, ls=ls),
        out_shape=SDS((rows, POOL_W), BF16),
        grid=(rows // tb,),
        in_specs=[pl.BlockSpec((tb, POOL_W), lambda i: (i, xcol)),
                  pl.BlockSpec((2 * SUBLANES, POOL_W), lambda i: (jnp.maximum(i * hb - 1, 0), xcol)),
                  pl.BlockSpec((len(POOL_WINDOWS), POOL_GROUP, POOL_GROUP), lambda i: (0, 0, 0)),
                  pl.BlockSpec((1, POOL_W), lambda i: (0, 0))],
        out_specs=pl.BlockSpec((tb, POOL_W), lambda i: (i, 0)),
        scratch_shapes=[pltpu.VMEM((tb + 2 * SUBLANES, POOL_W), F32)],
        compiler_params=_cparams(("arbitrary",)),
        name="pool_mixer",
    )(x, x, w_pool.astype(BF16), scale.reshape(1, POOL_W).astype(F32))


def _group_rms(x, w, width):
    rows = x.shape[0]
    lane = lax.broadcasted_iota(jnp.int32, (rows, LANES), 1)
    lo = lane < SWA_HD
    outs = []
    for c in range(width // LANES):
        t = x[:, c * LANES:(c + 1) * LANES]
        t2 = t * t
        tot = jnp.sum(t2, axis=-1, keepdims=True)
        slo = jnp.sum(jnp.where(lo, t2, 0.0), axis=-1, keepdims=True)
        ms = jnp.where(lo, slo, tot - slo) * (1.0 / SWA_HD)
        outs.append(t * lax.rsqrt(ms + EPS))
    return jnp.concatenate(outs, axis=1) * w


def _swa_prep_kernel(q_ref, k_ref, v_ref, qw_ref, kw_ref, pm_ref, qo_ref, ko_ref, kx_ref, vx_ref):
    qo_ref[...] = _group_rms(q_ref[...], qw_ref[...], SWA_W).astype(qo_ref.dtype)
    kn = _group_rms(k_ref[...], kw_ref[...], SWA_KW)
    ko_ref[...] = kn
    kx_ref[...] = jnp.dot(kn.astype(BF16), pm_ref[...], preferred_element_type=F32).astype(BF16)
    vx_ref[...] = jnp.dot(v_ref[...].astype(BF16), pm_ref[...], preferred_element_type=F32).astype(BF16)


def _two_slot_matrix():
    src = jnp.arange(SWA_KW)
    grp, d = src // SWA_HD, src % SWA_HD
    dst = jnp.arange(4 * SWA_KW)
    dgrp, slot, dd = dst // (4 * SWA_HD), (dst % (4 * SWA_HD)) // SWA_HD, dst % SWA_HD
    keep = (slot == 0) | (slot == 3)
    hit = (grp[:, None] == dgrp[None, :]) & (d[:, None] == dd[None, :]) & keep[None, :]
    return hit.astype(BF16)


def swa_prep(x, qcol, kcol, vcol, q_norm_w, k_norm_w, *, rows, tb):
    qw = (jnp.tile(q_norm_w.astype(F32), SWA_HEADS) * (SWA_HD ** -0.5)).reshape(1, SWA_W)
    kw = jnp.tile(k_norm_w.astype(F32), SWA_KV).reshape(1, SWA_KW)
    wide = pl.BlockSpec((tb, SWA_W), lambda i: (i, 0))
    return pl.pallas_call(
        _swa_prep_kernel,
        out_shape=(SDS((rows, SWA_W), BF16), SDS((rows, SWA_KW), F32), SDS((rows, SWA_W), BF16),
                   SDS((rows, SWA_W), BF16)),
        grid=(rows // tb,),
        in_specs=[pl.BlockSpec((tb, SWA_W), lambda i: (i, qcol)), pl.BlockSpec((tb, SWA_KW), lambda i: (i, kcol)),
                  pl.BlockSpec((tb, SWA_KW), lambda i: (i, vcol)),
                  pl.BlockSpec((1, SWA_W), lambda i: (0, 0)), pl.BlockSpec((1, SWA_KW), lambda i: (0, 0)),
                  pl.BlockSpec((SWA_KW, SWA_W), lambda i: (0, 0))],
        out_specs=(wide, pl.BlockSpec((tb, SWA_KW), lambda i: (i, 0)), wide, wide),
        compiler_params=_cparams(("parallel",)),
        name="swa_prep",
    )(x, x, x, qw, kw, _two_slot_matrix())


def _swa_kernel(sink_ref, q_ref, kp_ref, kc_ref, vp_ref, vc_ref, o_ref, *, qb, first_has_prev):
    g = pl.program_id(1)
    i = pl.program_id(2)
    r = lax.broadcasted_iota(jnp.int32, (qb, WINDOW), 0)
    c = lax.broadcasted_iota(jnp.int32, (qb, WINDOW), 1)
    dist_p = WINDOW + r - c
    dist_c = r - c
    ok_p = dist_p <= WINDOW
    if not first_has_prev:
        ok_p = jnp.logical_and(ok_p, i > 0)
    ok_c = dist_c >= 0
    dpf = dist_p.astype(F32)
    dcf = dist_c.astype(F32)
    for pair in range(2):
        q2 = q_ref[:, pair * LANES:(pair + 1) * LANES]
        acc = jnp.zeros((qb, LANES), F32)
        for e in range(2):
            hl = 2 * pair + e
            hidx = g * (SWA_HEADS // SWA_KV) + hl
            sink = sink_ref[hidx]
            slope = sink_ref[SWA_HEADS + hidx]
            ksl = slice(e * LANES, (e + 1) * LANES)
            sp = lax.dot_general(q2, kp_ref[:, ksl], (((1,), (1,)), ((), ())), preferred_element_type=F32)
            sc = lax.dot_general(q2, kc_ref[:, ksl], (((1,), (1,)), ((), ())), preferred_element_type=F32)
            sp = jnp.where(ok_p, sp - slope * dpf, NEG)
            sc = jnp.where(ok_c, sc - slope * dcf, NEG)
            m = jnp.maximum(jnp.maximum(jnp.max(sp, axis=-1, keepdims=True), jnp.max(sc, axis=-1, keepdims=True)), sink)
            ep = jnp.exp(sp - m)
            ec = jnp.exp(sc - m)
            den = jnp.sum(ep, axis=-1, keepdims=True) + jnp.sum(ec, axis=-1, keepdims=True) + jnp.exp(sink - m)
            inv = 1.0 / den
            acc = acc + jnp.dot((ep * inv).astype(BF16), vp_ref[:, ksl], preferred_element_type=F32)
            acc = acc + jnp.dot((ec * inv).astype(BF16), vc_ref[:, ksl], preferred_element_type=F32)
        o_ref[:, pair * LANES:(pair + 1) * LANES] = acc.astype(o_ref.dtype)


def swa_attention(q, kx_prev, kx_cur, vx_prev, vx_cur, sinks, *, nseq, nq, qb, prev_shift, first_has_prev):
    gw = (SWA_HEADS // SWA_KV) * SWA_HD
    slopes = jnp.exp2(-8.0 * jnp.arange(1, SWA_HEADS + 1, dtype=F32) / SWA_HEADS)
    qspec = pl.BlockSpec((qb, gw), lambda b, g, i, s: (b * nq + i, g))
    pspec = pl.BlockSpec((WINDOW, gw), lambda b, g, i, s: (jnp.maximum(b * nq + i - prev_shift, 0), g))
    cspec = pl.BlockSpec((WINDOW, gw), lambda b, g, i, s: (b * nq + i, g))
    grid_spec = pltpu.PrefetchScalarGridSpec(
        num_scalar_prefetch=1,
        grid=(nseq, SWA_KV, nq),
        in_specs=[qspec, pspec, cspec, pspec, cspec],
        out_specs=qspec,
    )
    return pl.pallas_call(
        functools.partial(_swa_kernel, qb=qb, first_has_prev=first_has_prev),
        out_shape=SDS((nseq * nq * qb, SWA_W), BF16),
        grid_spec=grid_spec,
        compiler_params=_cparams(("parallel", "parallel", "arbitrary")),
        name="swa_attention",
    )(jnp.concatenate([sinks.astype(F32), slopes]), q, kx_prev, kx_cur, vx_prev, vx_cur)


def _two_slot(t):
    rows = t.shape[0]
    t = t.astype(BF16).reshape(rows, SWA_KV, SWA_HD)
    z = jnp.zeros_like(t)
    return jnp.concatenate([t, z, z, t], axis=-1).reshape(rows, SWA_KV * 4 * SWA_HD)


def _merge_kernel(ya_ref, yb_ref, yc_ref, wa_ref, wb_ref, wc_ref, ga_ref, gb_ref, gc_ref, o_ref):
    a = jnp.dot(ya_ref[...], wa_ref[...], preferred_element_type=F32)
    b = jnp.dot(yb_ref[...], wb_ref[...], preferred_element_type=F32)
    c = jnp.dot(yc_ref[...], wc_ref[...], preferred_element_type=F32)
    o = _sigmoid(ga_ref[...]) * a + _sigmoid(gb_ref[...]) * b + _sigmoid(gc_ref[...]) * c
    o_ref[...] = o.astype(o_ref.dtype)


def merge_branches(ya, yb, yc, wa, wb, wc, layer, proj, gcol0, d_model, *, bm, bn):
    m = ya.shape[0]
    gb0 = gcol0 // bn
    gstep = d_model // bn
    assert gcol0 % bn == 0 and d_model % bn == 0 and m % bm == 0

    def gspec(t):
        return pl.BlockSpec((bm, bn), lambda i, j: (i, gb0 + t * gstep + j))

    def wspec(kdim):
        return pl.BlockSpec((None, kdim, bn), lambda i, j: (layer, 0, j))

    return pl.pallas_call(
        _merge_kernel,
        out_shape=SDS((m, d_model), BF16),
        grid=(m // bm, d_model // bn),
        in_specs=[pl.BlockSpec((bm, DN_W), lambda i, j: (i, 0)), pl.BlockSpec((bm, POOL_W), lambda i, j: (i, 0)),
                  pl.BlockSpec((bm, SWA_W), lambda i, j: (i, 0)),
                  wspec(DN_W), wspec(POOL_W), wspec(SWA_W),
                  gspec(0), gspec(1), gspec(2)],
        out_specs=pl.BlockSpec((bm, bn), lambda i, j: (i, j)),
        compiler_params=_cparams(("parallel", "parallel")),
        name="merge_branches",
    )(ya, yb, yc, wa, wb, wc, proj, proj, proj)


def _ffn_act_kernel(g_ref, halo_ref, v_ref, cw_ref, cb_ref, o_ref, xx_scr, *, tb, ls):
    i = pl.program_id(0)
    bc = g_ref.shape[1]
    row = lax.broadcasted_iota(jnp.int32, (tb, bc), 0)
    pos = (i * tb + row) % ls
    xx_scr[0:SUBLANES, :] = halo_ref[...]
    xx_scr[SUBLANES:SUBLANES + tb, :] = g_ref[...]
    acc = g_ref[...] * cw_ref[FFN_CONV - 1:FFN_CONV, :]
    for s in range(1, FFN_CONV):
        tap = xx_scr[SUBLANES - s:SUBLANES - s + tb, :]
        acc = acc + jnp.where(pos >= s, tap, 0.0) * cw_ref[FFN_CONV - 1 - s:FFN_CONV - s, :]
    o_ref[...] = (_silu(acc + cb_ref[...]) * v_ref[...]).astype(o_ref.dtype)


def ffn_activation(gate, gcol0, val, vcol0, conv_w, conv_b, d_ff, *, rows, ls, tb, bc):
    assert rows % tb == 0 and d_ff % bc == 0 and gcol0 % bc == 0 and vcol0 % bc == 0
    hb = tb // SUBLANES
    g0 = gcol0 // bc
    v0 = vcol0 // bc
    return pl.pallas_call(
        functools.partial(_ffn_act_kernel, tb=tb, ls=ls),
        out_shape=SDS((rows, d_ff), BF16),
        grid=(rows // tb, d_ff // bc),
        in_specs=[pl.BlockSpec((tb, bc), lambda i, j: (i, g0 + j)),
                  pl.BlockSpec((SUBLANES, bc), lambda i, j: (jnp.maximum(i * hb - 1, 0), g0 + j)),
                  pl.BlockSpec((tb, bc), lambda i, j: (i, v0 + j)),
                  pl.BlockSpec((FFN_CONV, bc), lambda i, j: (0, j)),
                  pl.BlockSpec((1, bc), lambda i, j: (0, j))],
        out_specs=pl.BlockSpec((tb, bc), lambda i, j: (i, j)),
        scratch_shapes=[pltpu.VMEM((tb + SUBLANES, bc), F32)],
        compiler_params=_cparams(("parallel", "parallel")),
        name="ffn_activation",
    )(gate, gate, val, conv_w.astype(F32), conv_b.reshape(1, d_ff).astype(F32))


IN_PAD = 512


def _in_layout(d_model):
    off = {}
    c = 0
    for name, width in (("dq", DN_W), ("dk", DN_W), ("dv", DN_W), ("dz", DN_W),
                        ("pin", POOL_W), ("sq", SWA_W), ("sk", SWA_KW), ("sv", SWA_KW),
                        ("ga", d_model), ("gb", d_model), ("gc", d_model), ("ba", LANES)):
        off[name] = c
        c += width
    return off, -(-c // IN_PAD) * IN_PAD


def _ext(hist, new, ls):
    nseq, hr, c = hist.shape
    ln = new.shape[1]
    z = jnp.zeros((nseq, ls - hr - ln, c), new.dtype)
    return jnp.concatenate([z, hist.astype(new.dtype), new], axis=1).reshape(nseq * ls, c)


def _layer(x, nseq, seqlen, states, layer, wts, cfg):
    (n1, w_in, dconv_w, a_log, dt_bias, dn_norm_w, pool_w, pool_scale, q_norm_w, k_norm_w, sinks,
     w_ba, w_bb, w_bc, w_out, n2, w_up, fconv_w, fconv_b, w_down, state_delta) = wts
    rows, d_model = x.shape
    d_ff = w_down.shape[1]
    off = cfg["off"]
    fresh = states is None
    bm = _pick(seqlen if fresh else rows, cfg["bm"])
    bm_big = _pick(seqlen if fresh else rows, cfg["bm_big"])

    h = rmsnorm_bf16(x, n1)
    proj = matmul(h, w_in, layer, bm=bm_big, bn=cfg["bn_in"], name="in_proj")
    pb = proj
    offb = off
    proj3 = proj.reshape(nseq, seqlen, -1)

    def cols_a(name, width):
        return proj3[:, :, off[name]:off[name] + width]

    cols_b = cols_a

    if fresh:
        ls, chunk, vfrom = seqlen, cfg["chunk"], 0
        tbp = _pick(seqlen, cfg["tb_prep"], chunk)
        q, k, v, g, b = delta_prep(proj, 0, proj, off["ba"] // LANES, dconv_w, a_log, dt_bias,
                                   rows=rows, ls=ls, chunk=chunk, valid_from=0, tb=tbp)
        ya, s_new = delta_recurrence(q, k, v, g, b, proj, off["dz"] // LANES, None, layer, dn_norm_w,
                                     nseq=nseq, ls=ls, chunk=chunk, heads=cfg["heads"])
        conv_new = cols_a("dq", 3 * DN_W)[:, seqlen - (DN_CONV - 1):]
    else:
        conv_h, pool_h, k_h, v_h, ffn_h = states
        ls = SAMPLE_SEQ_ROWS
        vfrom = ls - seqlen
        xe = _ext(conv_h, cols_a("dq", 3 * DN_W), ls)
        de = _ext(jnp.zeros((nseq, 0, LANES), F32), cols_a("ba", LANES), ls)
        ze = _ext(jnp.zeros((nseq, 0, DN_W), F32), cols_a("dz", DN_W), ls)
        erows = nseq * ls
        q, k, v, g, b = delta_prep(xe, 0, de, 0, dconv_w, a_log, dt_bias,
                                   rows=erows, ls=ls, chunk=ls, valid_from=vfrom, tb=_pick(erows, 1024, ls))
        ya_e, s_new = delta_recurrence(q, k, v, g, b, ze, 0, state_delta, layer, dn_norm_w,
                                       nseq=nseq, ls=ls, chunk=ls, heads=cfg["heads"])
        ya = ya_e.reshape(nseq, ls, DN_W)[:, vfrom:].reshape(rows, DN_W)
        conv_new = xe.reshape(nseq, ls, 3 * DN_W)[:, ls - (DN_CONV - 1):]

    if fresh:
        yb = pool_mixer(pb, offb["pin"] // POOL_W, pool_w, pool_scale, rows=rows, ls=seqlen,
                        tb=_pick(seqlen, 512, 2 * SUBLANES))
        pool_new = cols_b("pin", POOL_W)[:, seqlen - POOL_HIST:]
    else:
        pe = _ext(pool_h, cols_b("pin", POOL_W), SAMPLE_SEQ_ROWS)
        erows = nseq * SAMPLE_SEQ_ROWS
        yb_e = pool_mixer(pe, 0, pool_w, pool_scale, rows=erows, ls=SAMPLE_SEQ_ROWS, tb=_pick(erows, 512, SAMPLE_SEQ_ROWS))
        yb = yb_e.reshape(nseq, SAMPLE_SEQ_ROWS, POOL_W)[:, SAMPLE_SEQ_ROWS - seqlen:].reshape(rows, POOL_W)
        pool_new = pe.reshape(nseq, SAMPLE_SEQ_ROWS, POOL_W)[:, SAMPLE_SEQ_ROWS - POOL_HIST:]

    qn, kn, kx, vx = swa_prep(pb, offb["sq"] // SWA_W, offb["sk"] // SWA_KW, offb["sv"] // SWA_KW,
                              q_norm_w, k_norm_w, rows=rows, tb=_pick(rows, 512, SUBLANES))
    vraw = cols_b("sv", SWA_KW)
    kn3 = kn.reshape(nseq, seqlen, SWA_KW)
    if fresh:
        yc = swa_attention(qn, kx, kx, vx, vx, sinks, nseq=nseq, nq=seqlen // WINDOW, qb=WINDOW,
                           prev_shift=1, first_has_prev=False)
        keep = min(WINDOW, seqlen)
        k_new = kn3[:, seqlen - keep:].reshape(nseq, keep, SWA_KV, SWA_HD)
        v_new = vraw[:, seqlen - keep:].reshape(nseq, keep, SWA_KV, SWA_HD)
    else:
        wb = k_h.shape[1]
        qpad = jnp.zeros((nseq, SUBLANES - seqlen, SWA_W), BF16)
        qe = jnp.concatenate([qn.reshape(nseq, seqlen, SWA_W), qpad], axis=1).reshape(nseq * SUBLANES, SWA_W)
        kpad = jnp.zeros((nseq, WINDOW - seqlen, SWA_W), BF16)
        kc = jnp.concatenate([kx.reshape(nseq, seqlen, SWA_W), kpad], axis=1).reshape(nseq * WINDOW, SWA_W)
        vc = jnp.concatenate([vx.reshape(nseq, seqlen, SWA_W), kpad], axis=1).reshape(nseq * WINDOW, SWA_W)
        kp = _two_slot(k_h.reshape(nseq * wb, SWA_KW))
        vp = _two_slot(v_h.reshape(nseq * wb, SWA_KW))
        yc_e = swa_attention(qe, kp, kc, vp, vc, sinks, nseq=nseq, nq=1, qb=SUBLANES,
                             prev_shift=0, first_has_prev=True)
        yc = yc_e.reshape(nseq, SUBLANES, SWA_W)[:, :seqlen].reshape(rows, SWA_W)
        k_new = jnp.concatenate([k_h, kn3.reshape(nseq, seqlen, SWA_KV, SWA_HD).astype(k_h.dtype)], axis=1)[:, -wb:]
        v_new = jnp.concatenate([v_h, vraw.reshape(nseq, seqlen, SWA_KV, SWA_HD).astype(v_h.dtype)], axis=1)[:, -wb:]

    merged = merge_branches(ya, yb, yc, w_ba, w_bb, w_bc, layer, pb, offb["ga"], d_model, bm=bm, bn=cfg["bn_merge"])
    x = matmul(merged, w_out, layer, bm=bm, bn=cfg["bn_out"], residual=x, name="out_proj")

    h2 = rmsnorm_bf16(x, n2)
    bc = cfg["bc_ffn"]
    if fresh:
        hmid, tail = up_ffn_fused(h2, w_up, layer, fconv_w, fconv_b, d_ff, bm=bm_big, bc=bc, seqlen=seqlen)
        tps = seqlen // bm_big
        tail4 = tail.reshape(nseq, tps, SUBLANES, d_ff)
        ffn_new = tail4[:, tps - 1, SUBLANES - (FFN_CONV - 1):]
    else:
        up = matmul(h2, w_up, layer, bm=bm, bn=cfg["bn_up"], name="up_proj")
        up3 = up.reshape(nseq, seqlen, 2 * d_ff)
        ls = SUBLANES
        ge = _ext(ffn_h, up3[:, :, :d_ff], ls)
        ve = _ext(jnp.zeros((nseq, 0, d_ff), F32), up3[:, :, d_ff:], ls)
        hm_e = ffn_activation(ge, 0, ve, 0, fconv_w, fconv_b, d_ff, rows=nseq * ls, ls=ls, tb=nseq * ls, bc=bc)
        hmid = hm_e.reshape(nseq, ls, d_ff)[:, ls - seqlen:].reshape(rows, d_ff)
        ffn_new = ge.reshape(nseq, ls, d_ff)[:, ls - (FFN_CONV - 1):]
    x = matmul(hmid, w_down, layer, bm=bm, bn=cfg["bn_down"], bk=cfg["bk_down"], residual=x, name="down_proj")
    return x, (s_new, conv_new, pool_new, k_new, v_new, ffn_new)


def _config(d_model, d_ff):
    off, n_pad = _in_layout(d_model)
    assert off["ga"] % IN_PAD == 0 and d_model % IN_PAD == 0
    return {
        "off": off,
        "bm": 1024,
        "bm_big": 2048,
        "bn_in": IN_PAD,
        "bn_merge": IN_PAD,
        "bn_out": _pick(d_model, 512, LANES),
        "bn_up": _pick(2 * d_ff, 512, LANES),
        "bn_down": _pick(d_model, 512, LANES),
        "bk_down": _pick(d_ff, 5504, LANES),
        "bc_ffn": _pick(d_ff, 256, LANES),
        "chunk": DELTA_CHUNK,
        "tb_prep": 1024,
        "heads": REC_HEADS,
    }


def kernel(x_prompt, x_sample, state_delta, state_dconv, state_pool, cache_swa_k, cache_swa_v, state_ffn_conv, norm1_w, w_in, dconv_w, dn_a_log, dn_dt_bias, dn_norm_w, pool_w, pool_scale, q_norm_w, k_norm_w, sinks, w_branch_a, w_branch_b, w_branch_c, w_out, norm2_w, w_up, ffn_conv_w, ffn_conv_b, w_down):
    depth = w_in.shape[0]
    bp, lp, d_model = x_prompt.shape
    bs, lsm, _ = x_sample.shape
    d_ff = w_down.shape[1]
    cfg = _config(d_model, d_ff)
    xp = x_prompt.reshape(bp * lp, d_model)
    xs = x_sample.reshape(bs * lsm, d_model)
    w_in_bf = prep_w_in(w_in, d_model)
    w_ba_bf, w_bb_bf, w_bc_bf, w_out_bf, w_up_bf, w_down_bf = (
        cast_bf16(w) for w in (w_branch_a, w_branch_b, w_branch_c, w_out, w_up, w_down))
    outs_p = [[] for _ in range(6)]
    outs_s = [[] for _ in range(6)]
    for l in range(depth):
        wts = (norm1_w[l], w_in_bf, dconv_w[l].astype(F32), dn_a_log[l], dn_dt_bias[l],
               dn_norm_w[l].astype(F32), pool_w[l], pool_scale[l], q_norm_w[l], k_norm_w[l], sinks[l],
               w_ba_bf, w_bb_bf, w_bc_bf, w_out_bf, norm2_w[l], w_up_bf, ffn_conv_w[l], ffn_conv_b[l],
               w_down_bf, state_delta.astype(F32))
        xp, st_p = _layer(xp, bp, lp, None, l, wts, cfg)
        st_in = (state_dconv[l], state_pool[l], cache_swa_k[l], cache_swa_v[l], state_ffn_conv[l])
        xs, st_s = _layer(xs, bs, lsm, st_in, l, wts, cfg)
        for lst, a in zip(outs_p, st_p):
            lst.append(a)
        for lst, a in zip(outs_s, st_s):
            lst.append(a)
    dt = x_prompt.dtype
    res = [xp.reshape(bp, lp, d_model), xs.reshape(bs, lsm, d_model)]
    for outs in (outs_p, outs_s):
        res.append(jnp.stack(outs[0]).astype(state_delta.dtype))
        for t in outs[1:]:
            res.append(jnp.stack(t).astype(dt))
    return tuple(res)
```

```python
import functools
import math

import jax
import jax.numpy as jnp
from jax import lax
from jax.experimental import pallas as pl
from jax.experimental.pallas import tpu as pltpu

F32 = jnp.float32
BF16 = jnp.bfloat16
EPS = 1e-6
NEG = -1e30

LANES = 128
SUBLANES = 8
VMEM_LIMIT = 56 * 1024 * 1024

DN_HEADS = 16
DN_D = 128
DN_W = DN_HEADS * DN_D
DN_CONV = 4
POOL_WINDOWS = (2, 4, 8, 16)
POOL_GROUP = 256
POOL_W = 1024
POOL_HIST = 15
SWA_HEADS = 16
SWA_KV = 4
SWA_HD = 64
SWA_W = SWA_HEADS * SWA_HD
SWA_KW = SWA_KV * SWA_HD
WINDOW = 128
FFN_CONV = 3

DELTA_CHUNK = 128
SAMPLE_SEQ_ROWS = 32
REC_HEADS = 8

SDS = jax.ShapeDtypeStruct


def _cparams(sem, vmem=VMEM_LIMIT):
    return pltpu.CompilerParams(dimension_semantics=sem, vmem_limit_bytes=vmem)


def _pick(n, target, mult=16):
    best = None
    for d in range(mult, min(n, target) + 1, mult):
        if n % d == 0:
            best = d
    assert best is not None, (n, target, mult)
    return best


def _sigmoid(x):
    return 1.0 / (1.0 + jnp.exp(-x))


def _silu(x):
    return x * _sigmoid(x)


def _dot(a, b):
    return jnp.dot(a.astype(BF16), b.astype(BF16), preferred_element_type=F32)


def _bdot(a, b):
    return lax.dot_general(a.astype(BF16), b.astype(BF16), (((2,), (1,)), ((0,), (0,))),
                           preferred_element_type=F32)


def _bdot_nt(a, b):
    return lax.dot_general(a.astype(BF16), b.astype(BF16), (((2,), (2,)), ((0,), (0,))),
                           preferred_element_type=F32)


def _bdot_tn(a, b):
    return lax.dot_general(a.astype(BF16), b.astype(BF16), (((1,), (1,)), ((0,), (0,))),
                           preferred_element_type=F32)


def _rmsnorm_kernel(x_ref, w_ref, o_ref):
    x = x_ref[...]
    ms = jnp.mean(x * x, axis=-1, keepdims=True)
    o_ref[...] = (x * lax.rsqrt(ms + EPS) * w_ref[...]).astype(o_ref.dtype)


def rmsnorm_bf16(x, w):
    rows, d = x.shape
    br = _pick(rows, 256, SUBLANES)
    return pl.pallas_call(
        _rmsnorm_kernel,
        out_shape=SDS((rows, d), BF16),
        grid=(rows // br,),
        in_specs=[pl.BlockSpec((br, d), lambda i: (i, 0)), pl.BlockSpec((1, d), lambda i: (0, 0))],
        out_specs=pl.BlockSpec((br, d), lambda i: (i, 0)),
        compiler_params=_cparams(("parallel",)),
        name="rmsnorm",
    )(x, w.reshape(1, d))


def _mm_kernel(*refs, nk, has_res):
    if has_res:
        a_ref, w_ref, r_ref, o_ref = refs
    else:
        a_ref, w_ref, o_ref = refs
        r_ref = None
    acc = jnp.dot(a_ref[...], w_ref[...], preferred_element_type=F32)
    if nk == 1:
        if has_res:
            acc = acc + r_ref[...]
        o_ref[...] = acc.astype(o_ref.dtype)
    else:
        k = pl.program_id(2)

        @pl.when(k == 0)
        def _():
            o_ref[...] = (acc + r_ref[...]) if has_res else acc

        @pl.when(k > 0)
        def _():
            o_ref[...] += acc


def matmul(a, w, layer, *, bm, bn, bk=None, residual=None, out_dtype=F32, name="matmul"):
    m, kdim = a.shape
    n = w.shape[2]
    bk = kdim if bk is None else bk
    nk = kdim // bk
    assert m % bm == 0 and n % bn == 0 and kdim % bk == 0
    assert nk == 1 or out_dtype == F32
    in_specs = [pl.BlockSpec((bm, bk), lambda i, j, k: (i, k)),
                pl.BlockSpec((None, bk, bn), lambda i, j, k: (layer, k, j))]
    args = [a, w]
    if residual is not None:
        in_specs.append(pl.BlockSpec((bm, bn), lambda i, j, k: (i, j)))
        args.append(residual)
    return pl.pallas_call(
        functools.partial(_mm_kernel, nk=nk, has_res=residual is not None),
        out_shape=SDS((m, n), out_dtype),
        grid=(m // bm, n // bn, nk),
        in_specs=in_specs,
        out_specs=pl.BlockSpec((bm, bn), lambda i, j, k: (i, j)),
        compiler_params=_cparams(("parallel", "parallel", "arbitrary")),
        name=name,
    )(*args)


def _cast_kernel(x_ref, o_ref):
    o_ref[...] = x_ref[...].astype(o_ref.dtype)


CAST_BLOCK_ELEMS = 2 * 1024 * 1024


def cast_bf16(w):
    depth, kdim, n = w.shape
    bc = _pick(n, 8192, LANES)
    br = _pick(kdim, max(SUBLANES, CAST_BLOCK_ELEMS // bc), SUBLANES)
    spec = pl.BlockSpec((1, br, bc), lambda l, i, j: (l, i, j))
    return pl.pallas_call(
        _cast_kernel,
        out_shape=SDS(w.shape, BF16),
        grid=(depth, kdim // br, n // bc),
        in_specs=[spec],
        out_specs=spec,
        compiler_params=_cparams(("parallel", "parallel", "parallel")),
        name="cast_bf16",
    )(w)


W_IN_BLOCK = 256


def _w_in_kernel(w1_ref, w2_ref, o_ref, *, n_a, n_b, shift):
    j = pl.program_id(1)

    def emit(rows):
        o_ref[...] = rows.T.astype(o_ref.dtype)

    @pl.when(jnp.logical_or(j < n_a, j == n_a + n_b))
    def _():
        emit(w1_ref[...])

    @pl.when(jnp.logical_and(j >= n_a, j < n_a + n_b))
    def _():
        emit(jnp.concatenate([w1_ref[shift:, :], w2_ref[...]], axis=0))

    @pl.when(j > n_a + n_b)
    def _():
        o_ref[...] = jnp.zeros(o_ref.shape, o_ref.dtype)


def prep_w_in(w_in, d_model):
    depth, kdim, n_raw = w_in.shape
    offs, n_pad = _in_layout(d_model)
    blk = W_IN_BLOCK
    shift = 2 * DN_HEADS
    n_a = 4 * DN_W // blk
    n_b = (offs["ba"] - 4 * DN_W) // blk
    assert 4 * DN_W % blk == 0 and (offs["ba"] - 4 * DN_W) % blk == 0 and n_pad % blk == 0
    assert n_raw == 4 * DN_W + shift + n_b * blk and blk % shift == 0
    wt = jnp.swapaxes(w_in, 1, 2)

    def src(j):
        return jnp.where(j < n_a + n_b, j, n_a)

    return pl.pallas_call(
        functools.partial(_w_in_kernel, n_a=n_a, n_b=n_b, shift=shift),
        out_shape=SDS((depth, kdim, n_pad), BF16),
        grid=(depth, n_pad // blk),
        in_specs=[pl.BlockSpec((None, blk, kdim), lambda l, j: (l, src(j), 0)),
                  pl.BlockSpec((None, shift, kdim),
                               lambda l, j: (l, jnp.minimum(src(j) + 1, n_a + n_b) * (blk // shift), 0))],
        out_specs=pl.BlockSpec((None, kdim, blk), lambda l, j: (l, 0, j)),
        compiler_params=_cparams(("parallel", "parallel")),
        name="prep_w_in",
    )(wt, wt)


def _upffn_kernel(a_ref, wg_ref, wv_ref, cw_ref, cb_ref, o_ref, tail_ref, g_scr, carry_scr, *, bm, tiles_per_seq):
    i = pl.program_id(0)
    j = pl.program_id(1)
    a = a_ref[...]
    gate = jnp.dot(a, wg_ref[...], preferred_element_type=F32)
    val = jnp.dot(a, wv_ref[...], preferred_element_type=F32)

    @pl.when(i % tiles_per_seq == 0)
    def _():
        g_scr[0:SUBLANES, :] = jnp.zeros((SUBLANES, g_scr.shape[1]), F32)

    @pl.when(i % tiles_per_seq != 0)
    def _():
        g_scr[0:SUBLANES, :] = carry_scr[j]

    g_scr[SUBLANES:SUBLANES + bm, :] = gate
    conv = gate * cw_ref[FFN_CONV - 1:FFN_CONV, :]
    for s in range(1, FFN_CONV):
        conv = conv + g_scr[SUBLANES - s:SUBLANES - s + bm, :] * cw_ref[FFN_CONV - 1 - s:FFN_CONV - s, :]
    o_ref[...] = (_silu(conv + cb_ref[...]) * val).astype(o_ref.dtype)
    tail = gate[bm - SUBLANES:bm, :]
    tail_ref[...] = tail
    carry_scr[j] = tail


def up_ffn_fused(a, w_up, layer, conv_w, conv_b, d_ff, *, bm, bc, seqlen):
    m, kdim = a.shape
    assert m % bm == 0 and seqlen % bm == 0 and d_ff % bc == 0
    nb = d_ff // bc
    return pl.pallas_call(
        functools.partial(_upffn_kernel, bm=bm, tiles_per_seq=seqlen // bm),
        out_shape=(SDS((m, d_ff), BF16), SDS((m // bm * SUBLANES, d_ff), F32)),
        grid=(m // bm, nb),
        in_specs=[pl.BlockSpec((bm, kdim), lambda i, j: (i, 0)),
                  pl.BlockSpec((None, kdim, bc), lambda i, j: (layer, 0, j)),
                  pl.BlockSpec((None, kdim, bc), lambda i, j: (layer, 0, nb + j)),
                  pl.BlockSpec((FFN_CONV, bc), lambda i, j: (0, j)),
                  pl.BlockSpec((1, bc), lambda i, j: (0, j))],
        out_specs=(pl.BlockSpec((bm, bc), lambda i, j: (i, j)), pl.BlockSpec((SUBLANES, bc), lambda i, j: (i, j))),
        scratch_shapes=[pltpu.VMEM((bm + SUBLANES, bc), F32), pltpu.VMEM((nb, SUBLANES, bc), F32)],
        compiler_params=_cparams(("arbitrary", "arbitrary")),
        name="up_ffn",
    )(a, w_up, w_up, conv_w.astype(F32), conv_b.reshape(1, d_ff).astype(F32))


def _dprep_kernel(xq_ref, xk_ref, xv_ref, hq_ref, hk_ref, hv_ref, d_ref, cq_ref, ck_ref, cv_ref, aneg_ref, dtb_ref,
                  q_ref, k_ref, v_ref, g_ref, b_ref, xx_scr, gam_scr, beta_scr, *, tb, ls, chunk, valid_from):
    i = pl.program_id(0)
    h = pl.program_id(1)
    row = lax.broadcasted_iota(jnp.int32, (tb, LANES), 0)
    lane = lax.broadcasted_iota(jnp.int32, (tb, LANES), 1)
    pos = (i * tb + row) % ls
    valid = pos >= valid_from

    @pl.when(h == 0)
    def _():
        d = d_ref[...]
        beta = jnp.where(valid, jnp.where(lane < DN_HEADS, _sigmoid(d), 0.0), 0.0)
        sp = d + dtb_ref[...]
        sp = jnp.maximum(sp, 0.0) + jnp.log(1.0 + jnp.exp(-jnp.abs(sp)))
        is_alpha = jnp.where(lane >= DN_HEADS, jnp.where(lane < 2 * DN_HEADS, 1, 0), 0)
        g = jnp.where(valid, jnp.where(is_alpha == 1, aneg_ref[...] * sp, 0.0), 0.0)
        pc = pos % chunk
        s = 1
        while s < chunk:
            g = g + jnp.where(pc >= s, pltpu.roll(g, s, 0), 0.0)
            s *= 2
        for scr, t in ((gam_scr, g), (beta_scr, beta)):
            for piece in range(3):
                tb16 = t.astype(BF16)
                scr[piece] = tb16
                t = t - tb16.astype(F32)

    fresh_tile = (i * tb) % ls == 0

    def conv_silu(x_ref, halo_ref, cw_ref):
        xx_scr[0:SUBLANES, :] = jnp.where(fresh_tile, 0.0, halo_ref[...])
        xx_scr[SUBLANES:SUBLANES + tb, :] = x_ref[...]
        acc = x_ref[...] * cw_ref[DN_CONV - 1:DN_CONV, :]
        for s in range(1, DN_CONV):
            acc = acc + xx_scr[SUBLANES - s:SUBLANES - s + tb, :] * cw_ref[DN_CONV - 1 - s:DN_CONV - s, :]
        return _silu(acc)

    def mask(t):
        return t if valid_from == 0 else jnp.where(valid, t, 0.0)

    q = conv_silu(xq_ref, hq_ref, cq_ref)
    q_ref[...] = mask(q * (lax.rsqrt(jnp.sum(q * q, axis=-1, keepdims=True) + EPS) * (DN_D ** -0.5)))
    k = conv_silu(xk_ref, hk_ref, ck_ref)
    k_ref[...] = mask(k * lax.rsqrt(jnp.sum(k * k, axis=-1, keepdims=True) + EPS))
    v_ref[...] = mask(conv_silu(xv_ref, hv_ref, cv_ref))

    sel_r = lax.broadcasted_iota(jnp.int32, (LANES, LANES), 0)

    def lane_broadcast(scr, col):
        onehot = jnp.where(sel_r == col, 1.0, 0.0).astype(BF16)
        out = jnp.dot(scr[2], onehot, preferred_element_type=F32)
        out = out + jnp.dot(scr[1], onehot, preferred_element_type=F32)
        return out + jnp.dot(scr[0], onehot, preferred_element_type=F32)

    b_ref[...] = lane_broadcast(beta_scr, h)
    g_ref[...] = lane_broadcast(gam_scr, h + DN_HEADS)


def delta_prep(x, xcol0, dt, dcol, conv_w, a_log, dt_bias, *, rows, ls, chunk, valid_from, tb):
    assert rows % tb == 0 and tb % chunk == 0 and tb % SUBLANES == 0
    assert (valid_from == 0 and ls % tb == 0) or (valid_from >= DN_CONV - 1 and tb % ls == 0)
    hb = tb // SUBLANES
    aneg = jnp.zeros((1, LANES), F32).at[0, DN_HEADS:2 * DN_HEADS].set(-jnp.exp(a_log.astype(F32)))
    dtb = jnp.zeros((1, LANES), F32).at[0, DN_HEADS:2 * DN_HEADS].set(dt_bias.astype(F32))

    def xspec(off):
        return pl.BlockSpec((tb, LANES), lambda i, h: (i, xcol0 + off + h))

    def hspec(off):
        return pl.BlockSpec((SUBLANES, LANES), lambda i, h: (jnp.maximum(i * hb - 1, 0), xcol0 + off + h))

    def cspec(off):
        return pl.BlockSpec((DN_CONV, LANES), lambda i, h: (0, off + h))

    ospec = pl.BlockSpec((tb, LANES), lambda i, h: (i, h))
    out = SDS((rows, DN_W), F32)
    return pl.pallas_call(
        functools.partial(_dprep_kernel, tb=tb, ls=ls, chunk=chunk, valid_from=valid_from),
        out_shape=(out,) * 5,
        grid=(rows // tb, DN_HEADS),
        in_specs=[xspec(0), xspec(DN_HEADS), xspec(2 * DN_HEADS), hspec(0), hspec(DN_HEADS), hspec(2 * DN_HEADS),
                  pl.BlockSpec((tb, LANES), lambda i, h: (i, dcol)),
                  cspec(0), cspec(DN_HEADS), cspec(2 * DN_HEADS),
                  pl.BlockSpec((1, LANES), lambda i, h: (0, 0)), pl.BlockSpec((1, LANES), lambda i, h: (0, 0))],
        out_specs=(ospec,) * 5,
        scratch_shapes=[pltpu.VMEM((tb + SUBLANES, LANES), F32), pltpu.VMEM((3, tb, LANES), BF16),
                        pltpu.VMEM((3, tb, LANES), BF16)],
        compiler_params=_cparams(("parallel", "arbitrary")),
        name="delta_prep",
    )(x, x, x, x, x, x, dt, conv_w, conv_w, conv_w, aneg, dtb)


def _drec_kernel(*refs, chunk, heads, nsteps, zero_init):
    if zero_init:
        q_ref, k_ref, v_ref, g_ref, b_ref, z_ref, nw_ref, y_ref, sf_ref, s_scr = refs
        s0_ref = None
    else:
        q_ref, k_ref, v_ref, g_ref, b_ref, z_ref, s0_ref, nw_ref, y_ref, sf_ref, s_scr = refs
    n = pl.program_id(2)

    @pl.when(n == 0)
    def _():
        s_scr[...] = jnp.zeros(s_scr.shape, F32) if zero_init else s0_ref[0]

    ri = lax.broadcasted_iota(jnp.int32, (chunk, chunk), 0)
    ci = lax.broadcasted_iota(jnp.int32, (chunk, chunk), 1)
    incl = ri >= ci
    strict = ri > ci
    eye = jnp.where(ri == ci, 1.0, 0.0)
    levels = int(math.log2(chunk))

    def heads_of(ref):
        return jnp.stack([ref[:, j * DN_D:(j + 1) * DN_D] for j in range(heads)], axis=0)

    q = heads_of(q_ref)
    k = heads_of(k_ref)
    v = heads_of(v_ref)
    gb = heads_of(g_ref)
    bb = heads_of(b_ref)
    s = s_scr[...]
    eg = jnp.exp(gb)
    glast = gb[:, chunk - 1:chunk, :]
    ek = jnp.exp(glast - gb)
    tot = jnp.exp(glast)
    gc = gb[:, :, :chunk]
    bc = bb[:, :, :chunk]
    diff = gc - jnp.swapaxes(gc, 1, 2)
    decay = jnp.where(incl, jnp.exp(jnp.where(incl, diff, 0.0)), 0.0)
    kb = k.astype(BF16)
    kk = _bdot_nt(kb, kb)
    qk = _bdot_nt(q, kb)
    x = jnp.where(strict, -(bc * kk * decay), 0.0)
    p = eye + x
    xb = x.astype(BF16)
    x = _bdot(xb, xb)
    for lvl in range(1, levels):
        xb = x.astype(BF16)
        pn = p + _bdot(xb, p)
        if lvl < levels - 1:
            x = _bdot(xb, xb)
        p = pn
    rhs = jnp.concatenate([bb * eg * k, bb * v], axis=2)
    sol = _bdot(p, rhs)
    wk = sol[:, :, :DN_D]
    ub = sol[:, :, DN_D:]
    sb = s.astype(BF16)
    u = ub - _bdot(wk, sb)
    o = _bdot(eg * q, sb) + _bdot(qk * decay, u)
    s_scr[...] = tot * s + _bdot_tn(ek * k, u)
    ms = jnp.mean(o * o, axis=-1, keepdims=True)
    on = o * lax.rsqrt(ms + EPS) * nw_ref[...]
    for j in range(heads):
        sl = slice(j * DN_D, (j + 1) * DN_D)
        y_ref[:, sl] = (on[j] * _silu(z_ref[:, sl])).astype(y_ref.dtype)

    @pl.when(n == nsteps - 1)
    def _():
        sf_ref[0] = s_scr[...]


def delta_recurrence(q, k, v, g, b, z, zcol0, s0, layer, norm_w, *, nseq, ls, chunk, heads):
    rows = nseq * ls
    nsteps = ls // chunk
    hw = heads * DN_D
    ispec = pl.BlockSpec((chunk, hw), lambda bi, hg, n: (bi * nsteps + n, hg))
    zc = zcol0 // heads
    assert zcol0 % heads == 0
    sspec = pl.BlockSpec((1, heads, DN_D, DN_D), lambda bi, hg, n: (bi, hg, 0, 0))
    in_specs = [ispec, ispec, ispec, ispec, ispec,
                pl.BlockSpec((chunk, hw), lambda bi, hg, n: (bi * nsteps + n, zc + hg))]
    args = [q, k, v, g, b, z]
    if s0 is not None:
        in_specs.append(pl.BlockSpec((None, 1, heads, DN_D, DN_D), lambda bi, hg, n: (layer, bi, hg, 0, 0)))
        args.append(s0)
    in_specs.append(pl.BlockSpec((1, DN_D), lambda bi, hg, n: (0, 0)))
    args.append(norm_w.reshape(1, DN_D))
    return pl.pallas_call(
        functools.partial(_drec_kernel, chunk=chunk, heads=heads, nsteps=nsteps, zero_init=s0 is None),
        out_shape=(SDS((rows, DN_W), BF16), SDS((nseq, DN_HEADS, DN_D, DN_D), F32)),
        grid=(nseq, DN_HEADS // heads, nsteps),
        in_specs=in_specs,
        out_specs=(ispec, sspec),
        scratch_shapes=[pltpu.VMEM((heads, DN_D, DN_D), F32)],
        compiler_params=_cparams(("parallel", "parallel", "arbitrary")),
        name="delta_recurrence",
    )(*args)


def _pool_kernel(x_ref, halo_ref, w_ref, sc_ref, y_ref, xx_scr, *, tb, ls):
    i = pl.program_id(0)
    hist = 2 * SUBLANES
    row = lax.broadcasted_iota(jnp.int32, (tb, POOL_GROUP), 0)
    pos = (i * tb + row) % ls
    xx_scr[0:hist, :] = halo_ref[...]
    xx_scr[hist:hist + tb, :] = x_ref[...]
    for gi, w in enumerate(POOL_WINDOWS):
        cs = slice(gi * POOL_GROUP, (gi + 1) * POOL_GROUP)
        x = x_ref[:, cs]
        tot = x
        for s in range(1, w):
            tap = xx_scr[hist - s:hist - s + tb, cs]
            tot = tot + jnp.where(pos >= s, tap, 0.0)
        cnt = jnp.minimum(pos + 1, w).astype(F32)
        pooled = tot / cnt - x
        y = _dot(pooled, w_ref[gi]) * sc_ref[:, cs]
        y_ref[:, cs] = y.astype(y_ref.dtype)


def pool_mixer(x, xcol, w_pool, scale, *, rows, ls, tb):
    assert rows % tb == 0 and tb % (2 * SUBLANES) == 0
    hb = tb // (2 * SUBLANES)
    return pl.pallas_call(
        functools.partial(_pool_kernel, tb=tb, ls=ls),
        out_shape=SDS((rows, POOL_W), BF16),
        grid=(rows // tb,),
        in_specs=[pl.BlockSpec((tb, POOL_W), lambda i: (i, xcol)),
                  pl.BlockSpec((2 * SUBLANES, POOL_W), lambda i: (jnp.maximum(i * hb - 1, 0), xcol)),
                  pl.BlockSpec((len(POOL_WINDOWS), POOL_GROUP, POOL_GROUP), lambda i: (0, 0, 0)),
                  pl.BlockSpec((1, POOL_W), lambda i: (0, 0))],
        out_specs=pl.BlockSpec((tb, POOL_W), lambda i: (i, 0)),
        scratch_shapes=[pltpu.VMEM((tb + 2 * SUBLANES, POOL_W), F32)],
        compiler_params=_cparams(("arbitrary",)),
        name="pool_mixer",
    )(x, x, w_pool.astype(BF16), scale.reshape(1, POOL_W).astype(F32))


def _group_rms(x, w, width):
    rows = x.shape[0]
    lane = lax.broadcasted_iota(jnp.int32, (rows, LANES), 1)
    lo = lane < SWA_HD
    outs = []
    for c in range(width // LANES):
        t = x[:, c * LANES:(c + 1) * LANES]
        t2 = t * t
        tot = jnp.sum(t2, axis=-1, keepdims=True)
        slo = jnp.sum(jnp.where(lo, t2, 0.0), axis=-1, keepdims=True)
        ms = jnp.where(lo, slo, tot - slo) * (1.0 / SWA_HD)
        outs.append(t * lax.rsqrt(ms + EPS))
    return jnp.concatenate(outs, axis=1) * w


def _swa_prep_kernel(q_ref, k_ref, v_ref, qw_ref, kw_ref, pm_ref, qo_ref, ko_ref, kx_ref, vx_ref):
    qo_ref[...] = _group_rms(q_ref[...], qw_ref[...], SWA_W).astype(qo_ref.dtype)
    kn = _group_rms(k_ref[...], kw_ref[...], SWA_KW)
    ko_ref[...] = kn
    kx_ref[...] = jnp.dot(kn.astype(BF16), pm_ref[...], preferred_element_type=F32).astype(BF16)
    vx_ref[...] = jnp.dot(v_ref[...].astype(BF16), pm_ref[...], preferred_element_type=F32).astype(BF16)


def _two_slot_matrix():
    src = jnp.arange(SWA_KW)
    grp, d = src // SWA_HD, src % SWA_HD
    dst = jnp.arange(4 * SWA_KW)
    dgrp, slot, dd = dst // (4 * SWA_HD), (dst % (4 * SWA_HD)) // SWA_HD, dst % SWA_HD
    keep = (slot == 0) | (slot == 3)
    hit = (grp[:, None] == dgrp[None, :]) & (d[:, None] == dd[None, :]) & keep[None, :]
    return hit.astype(BF16)


def swa_prep(x, qcol, kcol, vcol, q_norm_w, k_norm_w, *, rows, tb):
    qw = (jnp.tile(q_norm_w.astype(F32), SWA_HEADS) * (SWA_HD ** -0.5)).reshape(1, SWA_W)
    kw = jnp.tile(k_norm_w.astype(F32), SWA_KV).reshape(1, SWA_KW)
    wide = pl.BlockSpec((tb, SWA_W), lambda i: (i, 0))
    return pl.pallas_call(
        _swa_prep_kernel,
        out_shape=(SDS((rows, SWA_W), BF16), SDS((rows, SWA_KW), F32), SDS((rows, SWA_W), BF16),
                   SDS((rows, SWA_W), BF16)),
        grid=(rows // tb,),
        in_specs=[pl.BlockSpec((tb, SWA_W), lambda i: (i, qcol)), pl.BlockSpec((tb, SWA_KW), lambda i: (i, kcol)),
                  pl.BlockSpec((tb, SWA_KW), lambda i: (i, vcol)),
                  pl.BlockSpec((1, SWA_W), lambda i: (0, 0)), pl.BlockSpec((1, SWA_KW), lambda i: (0, 0)),
                  pl.BlockSpec((SWA_KW, SWA_W), lambda i: (0, 0))],
        out_specs=(wide, pl.BlockSpec((tb, SWA_KW), lambda i: (i, 0)), wide, wide),
        compiler_params=_cparams(("parallel",)),
        name="swa_prep",
    )(x, x, x, qw, kw, _two_slot_matrix())


def _swa_kernel(slope_ref, sink_ref, q_ref, kp_ref, kc_ref, vp_ref, vc_ref, o_ref, *, qb, first_has_prev):
    i = pl.program_id(1)
    r = lax.broadcasted_iota(jnp.int32, (qb, WINDOW), 0)
    c = lax.broadcasted_iota(jnp.int32, (qb, WINDOW), 1)
    dist_p = WINDOW + r - c
    dist_c = r - c
    ok_p = dist_p <= WINDOW
    if not first_has_prev:
        ok_p = jnp.logical_and(ok_p, i > 0)
    ok_c = dist_c >= 0
    dpf = dist_p.astype(F32)
    dcf = dist_c.astype(F32)
    gh = SWA_HEADS // SWA_KV

    def per_head(ref, lane_block):
        return jnp.stack([ref[:, lane_block(h) * LANES:(lane_block(h) + 1) * LANES] for h in range(SWA_HEADS)], axis=0)

    slot = lambda h: (h // gh) * 2 + h % 2
    q = per_head(q_ref, lambda h: h // 2)
    kp = per_head(kp_ref, slot)
    kc = per_head(kc_ref, slot)
    vp = per_head(vp_ref, slot)
    vc = per_head(vc_ref, slot)
    slope = slope_ref[...]
    sink = sink_ref[...][:, :, :1]
    sp = lax.dot_general(q, kp, (((2,), (2,)), ((0,), (0,))), preferred_element_type=F32)
    sc = lax.dot_general(q, kc, (((2,), (2,)), ((0,), (0,))), preferred_element_type=F32)
    sp = jnp.where(ok_p, sp - slope * dpf, NEG)
    sc = jnp.where(ok_c, sc - slope * dcf, NEG)
    m = jnp.maximum(jnp.maximum(jnp.max(sp, axis=-1, keepdims=True), jnp.max(sc, axis=-1, keepdims=True)), sink)
    ep = jnp.exp(sp - m)
    ec = jnp.exp(sc - m)
    den = jnp.sum(ep, axis=-1, keepdims=True) + jnp.sum(ec, axis=-1, keepdims=True) + jnp.exp(sink - m)
    inv = 1.0 / den
    o = _bdot(ep * inv, vp) + _bdot(ec * inv, vc)
    for t in range(SWA_HEADS // 2):
        o_ref[:, t * LANES:(t + 1) * LANES] = (o[2 * t] + o[2 * t + 1]).astype(o_ref.dtype)


def swa_attention(q, kx_prev, kx_cur, vx_prev, vx_cur, sinks, *, nseq, nq, qb, prev_shift, first_has_prev):
    slopes = jnp.exp2(-8.0 * jnp.arange(1, SWA_HEADS + 1, dtype=F32) / SWA_HEADS)
    rep = lambda t: jnp.broadcast_to(t.astype(F32).reshape(SWA_HEADS, 1, 1), (SWA_HEADS, 1, LANES))
    qspec = pl.BlockSpec((qb, SWA_W), lambda b, i: (b * nq + i, 0))
    pspec = pl.BlockSpec((WINDOW, SWA_W), lambda b, i: (jnp.maximum(b * nq + i - prev_shift, 0), 0))
    cspec = pl.BlockSpec((WINDOW, SWA_W), lambda b, i: (b * nq + i, 0))
    tspec = pl.BlockSpec((SWA_HEADS, 1, LANES), lambda b, i: (0, 0, 0))
    return pl.pallas_call(
        functools.partial(_swa_kernel, qb=qb, first_has_prev=first_has_prev),
        out_shape=SDS((nseq * nq * qb, SWA_W), BF16),
        grid=(nseq, nq),
        in_specs=[tspec, tspec, qspec, pspec, cspec, pspec, cspec],
        out_specs=qspec,
        compiler_params=_cparams(("parallel", "arbitrary")),
        name="swa_attention",
    )(rep(slopes), rep(sinks), q, kx_prev, kx_cur, vx_prev, vx_cur)


def _two_slot(t):
    rows = t.shape[0]
    t = t.astype(BF16).reshape(rows, SWA_KV, SWA_HD)
    z = jnp.zeros_like(t)
    return jnp.concatenate([t, z, z, t], axis=-1).reshape(rows, SWA_KV * 4 * SWA_HD)


def _merge_kernel(ya_ref, yb_ref, yc_ref, wa_ref, wb_ref, wc_ref, ga_ref, gb_ref, gc_ref, o_ref):
    a = jnp.dot(ya_ref[...], wa_ref[...], preferred_element_type=F32)
    b = jnp.dot(yb_ref[...], wb_ref[...], preferred_element_type=F32)
    c = jnp.dot(yc_ref[...], wc_ref[...], preferred_element_type=F32)
    o = _sigmoid(ga_ref[...]) * a + _sigmoid(gb_ref[...]) * b + _sigmoid(gc_ref[...]) * c
    o_ref[...] = o.astype(o_ref.dtype)


def merge_branches(ya, yb, yc, wa, wb, wc, layer, proj, gcol0, d_model, *, bm, bn):
    m = ya.shape[0]
    gb0 = gcol0 // bn
    gstep = d_model // bn
    assert gcol0 % bn == 0 and d_model % bn == 0 and m % bm == 0

    def gspec(t):
        return pl.BlockSpec((bm, bn), lambda i, j: (i, gb0 + t * gstep + j))

    def wspec(kdim):
        return pl.BlockSpec((None, kdim, bn), lambda i, j: (layer, 0, j))

    return pl.pallas_call(
        _merge_kernel,
        out_shape=SDS((m, d_model), BF16),
        grid=(m // bm, d_model // bn),
        in_specs=[pl.BlockSpec((bm, DN_W), lambda i, j: (i, 0)), pl.BlockSpec((bm, POOL_W), lambda i, j: (i, 0)),
                  pl.BlockSpec((bm, SWA_W), lambda i, j: (i, 0)),
                  wspec(DN_W), wspec(POOL_W), wspec(SWA_W),
                  gspec(0), gspec(1), gspec(2)],
        out_specs=pl.BlockSpec((bm, bn), lambda i, j: (i, j)),
        compiler_params=_cparams(("parallel", "parallel")),
        name="merge_branches",
    )(ya, yb, yc, wa, wb, wc, proj, proj, proj)


def _ffn_act_kernel(g_ref, halo_ref, v_ref, cw_ref, cb_ref, o_ref, xx_scr, *, tb, ls):
    i = pl.program_id(0)
    bc = g_ref.shape[1]
    row = lax.broadcasted_iota(jnp.int32, (tb, bc), 0)
    pos = (i * tb + row) % ls
    xx_scr[0:SUBLANES, :] = halo_ref[...]
    xx_scr[SUBLANES:SUBLANES + tb, :] = g_ref[...]
    acc = g_ref[...] * cw_ref[FFN_CONV - 1:FFN_CONV, :]
    for s in range(1, FFN_CONV):
        tap = xx_scr[SUBLANES - s:SUBLANES - s + tb, :]
        acc = acc + jnp.where(pos >= s, tap, 0.0) * cw_ref[FFN_CONV - 1 - s:FFN_CONV - s, :]
    o_ref[...] = (_silu(acc + cb_ref[...]) * v_ref[...]).astype(o_ref.dtype)


def ffn_activation(gate, gcol0, val, vcol0, conv_w, conv_b, d_ff, *, rows, ls, tb, bc):
    assert rows % tb == 0 and d_ff % bc == 0 and gcol0 % bc == 0 and vcol0 % bc == 0
    hb = tb // SUBLANES
    g0 = gcol0 // bc
    v0 = vcol0 // bc
    return pl.pallas_call(
        functools.partial(_ffn_act_kernel, tb=tb, ls=ls),
        out_shape=SDS((rows, d_ff), BF16),
        grid=(rows // tb, d_ff // bc),
        in_specs=[pl.BlockSpec((tb, bc), lambda i, j: (i, g0 + j)),
                  pl.BlockSpec((SUBLANES, bc), lambda i, j: (jnp.maximum(i * hb - 1, 0), g0 + j)),
                  pl.BlockSpec((tb, bc), lambda i, j: (i, v0 + j)),
                  pl.BlockSpec((FFN_CONV, bc), lambda i, j: (0, j)),
                  pl.BlockSpec((1, bc), lambda i, j: (0, j))],
        out_specs=pl.BlockSpec((tb, bc), lambda i, j: (i, j)),
        scratch_shapes=[pltpu.VMEM((tb + SUBLANES, bc), F32)],
        compiler_params=_cparams(("parallel", "parallel")),
        name="ffn_activation",
    )(gate, gate, val, conv_w.astype(F32), conv_b.reshape(1, d_ff).astype(F32))


IN_PAD = 512


def _in_layout(d_model):
    off = {}
    c = 0
    for name, width in (("dq", DN_W), ("dk", DN_W), ("dv", DN_W), ("dz", DN_W),
                        ("pin", POOL_W), ("sq", SWA_W), ("sk", SWA_KW), ("sv", SWA_KW),
                        ("ga", d_model), ("gb", d_model), ("gc", d_model), ("ba", LANES)):
        off[name] = c
        c += width
    return off, -(-c // IN_PAD) * IN_PAD


def _ext(hist, new, ls):
    nseq, hr, c = hist.shape
    ln = new.shape[1]
    z = jnp.zeros((nseq, ls - hr - ln, c), new.dtype)
    return jnp.concatenate([z, hist.astype(new.dtype), new], axis=1).reshape(nseq * ls, c)


def _layer(x, nseq, seqlen, states, layer, wts, cfg):
    (n1, w_in, dconv_w, a_log, dt_bias, dn_norm_w, pool_w, pool_scale, q_norm_w, k_norm_w, sinks,
     w_ba, w_bb, w_bc, w_out, n2, w_up, fconv_w, fconv_b, w_down, state_delta) = wts
    rows, d_model = x.shape
    d_ff = w_down.shape[1]
    off = cfg["off"]
    fresh = states is None
    bm = _pick(seqlen if fresh else rows, cfg["bm"])
    bm_big = _pick(seqlen if fresh else rows, cfg["bm_big"])

    h = rmsnorm_bf16(x, n1)
    proj = matmul(h, w_in, layer, bm=bm_big, bn=cfg["bn_in"], name="in_proj")
    pb = proj
    offb = off
    proj3 = proj.reshape(nseq, seqlen, -1)

    def cols_a(name, width):
        return proj3[:, :, off[name]:off[name] + width]

    cols_b = cols_a

    if fresh:
        ls, chunk, vfrom = seqlen, cfg["chunk"], 0
        tbp = _pick(seqlen, cfg["tb_prep"], chunk)
        q, k, v, g, b = delta_prep(proj, 0, proj, off["ba"] // LANES, dconv_w, a_log, dt_bias,
                                   rows=rows, ls=ls, chunk=chunk, valid_from=0, tb=tbp)
        ya, s_new = delta_recurrence(q, k, v, g, b, proj, off["dz"] // LANES, None, layer, dn_norm_w,
                                     nseq=nseq, ls=ls, chunk=chunk, heads=cfg["heads"])
        conv_new = cols_a("dq", 3 * DN_W)[:, seqlen - (DN_CONV - 1):]
    else:
        conv_h, pool_h, k_h, v_h, ffn_h = states
        ls = SAMPLE_SEQ_ROWS
        vfrom = ls - seqlen
        xe = _ext(conv_h, cols_a("dq", 3 * DN_W), ls)
        de = _ext(jnp.zeros((nseq, 0, LANES), F32), cols_a("ba", LANES), ls)
        ze = _ext(jnp.zeros((nseq, 0, DN_W), F32), cols_a("dz", DN_W), ls)
        erows = nseq * ls
        q, k, v, g, b = delta_prep(xe, 0, de, 0, dconv_w, a_log, dt_bias,
                                   rows=erows, ls=ls, chunk=ls, valid_from=vfrom, tb=_pick(erows, 1024, ls))
        ya_e, s_new = delta_recurrence(q, k, v, g, b, ze, 0, state_delta, layer, dn_norm_w,
                                       nseq=nseq, ls=ls, chunk=ls, heads=cfg["heads"])
        ya = ya_e.reshape(nseq, ls, DN_W)[:, vfrom:].reshape(rows, DN_W)
        conv_new = xe.reshape(nseq, ls, 3 * DN_W)[:, ls - (DN_CONV - 1):]

    if fresh:
        yb = pool_mixer(pb, offb["pin"] // POOL_W, pool_w, pool_scale, rows=rows, ls=seqlen,
                        tb=_pick(seqlen, 512, 2 * SUBLANES))
        pool_new = cols_b("pin", POOL_W)[:, seqlen - POOL_HIST:]
    else:
        pe = _ext(pool_h, cols_b("pin", POOL_W), SAMPLE_SEQ_ROWS)
        erows = nseq * SAMPLE_SEQ_ROWS
        yb_e = pool_mixer(pe, 0, pool_w, pool_scale, rows=erows, ls=SAMPLE_SEQ_ROWS, tb=_pick(erows, 512, SAMPLE_SEQ_ROWS))
        yb = yb_e.reshape(nseq, SAMPLE_SEQ_ROWS, POOL_W)[:, SAMPLE_SEQ_ROWS - seqlen:].reshape(rows, POOL_W)
        pool_new = pe.reshape(nseq, SAMPLE_SEQ_ROWS, POOL_W)[:, SAMPLE_SEQ_ROWS - POOL_HIST:]

    qn, kn, kx, vx = swa_prep(pb, offb["sq"] // SWA_W, offb["sk"] // SWA_KW, offb["sv"] // SWA_KW,
                              q_norm_w, k_norm_w, rows=rows, tb=_pick(rows, 512, SUBLANES))
    vraw = cols_b("sv", SWA_KW)
    kn3 = kn.reshape(nseq, seqlen, SWA_KW)
    if fresh:
        yc = swa_attention(qn, kx, kx, vx, vx, sinks, nseq=nseq, nq=seqlen // WINDOW, qb=WINDOW,
                           prev_shift=1, first_has_prev=False)
        keep = min(WINDOW, seqlen)
        k_new = kn3[:, seqlen - keep:].reshape(nseq, keep, SWA_KV, SWA_HD)
        v_new = vraw[:, seqlen - keep:].reshape(nseq, keep, SWA_KV, SWA_HD)
    else:
        wb = k_h.shape[1]
        qpad = jnp.zeros((nseq, SUBLANES - seqlen, SWA_W), BF16)
        qe = jnp.concatenate([qn.reshape(nseq, seqlen, SWA_W), qpad], axis=1).reshape(nseq * SUBLANES, SWA_W)
        kpad = jnp.zeros((nseq, WINDOW - seqlen, SWA_W), BF16)
        kc = jnp.concatenate([kx.reshape(nseq, seqlen, SWA_W), kpad], axis=1).reshape(nseq * WINDOW, SWA_W)
        vc = jnp.concatenate([vx.reshape(nseq, seqlen, SWA_W), kpad], axis=1).reshape(nseq * WINDOW, SWA_W)
        kp = _two_slot(k_h.reshape(nseq * wb, SWA_KW))
        vp = _two_slot(v_h.reshape(nseq * wb, SWA_KW))
        yc_e = swa_attention(qe, kp, kc, vp, vc, sinks, nseq=nseq, nq=1, qb=SUBLANES,
                             prev_shift=0, first_has_prev=True)
        yc = yc_e.reshape(nseq, SUBLANES, SWA_W)[:, :seqlen].reshape(rows, SWA_W)
        k_new = jnp.concatenate([k_h, kn3.reshape(nseq, seqlen, SWA_KV, SWA_HD).astype(k_h.dtype)], axis=1)[:, -wb:]
        v_new = jnp.concatenate([v_h, vraw.reshape(nseq, seqlen, SWA_KV, SWA_HD).astype(v_h.dtype)], axis=1)[:, -wb:]

    merged = merge_branches(ya, yb, yc, w_ba, w_bb, w_bc, layer, pb, offb["ga"], d_model, bm=bm, bn=cfg["bn_merge"])
    x = matmul(merged, w_out, layer, bm=bm, bn=cfg["bn_out"], residual=x, name="out_proj")

    h2 = rmsnorm_bf16(x, n2)
    bc = cfg["bc_ffn"]
    if fresh:
        hmid, tail = up_ffn_fused(h2, w_up, layer, fconv_w, fconv_b, d_ff, bm=bm_big, bc=bc, seqlen=seqlen)
        tps = seqlen // bm_big
        tail4 = tail.reshape(nseq, tps, SUBLANES, d_ff)
        ffn_new = tail4[:, tps - 1, SUBLANES - (FFN_CONV - 1):]
    else:
        up = matmul(h2, w_up, layer, bm=bm, bn=cfg["bn_up"], name="up_proj")
        up3 = up.reshape(nseq, seqlen, 2 * d_ff)
        ls = SUBLANES
        ge = _ext(ffn_h, up3[:, :, :d_ff], ls)
        ve = _ext(jnp.zeros((nseq, 0, d_ff), F32), up3[:, :, d_ff:], ls)
        hm_e = ffn_activation(ge, 0, ve, 0, fconv_w, fconv_b, d_ff, rows=nseq * ls, ls=ls, tb=nseq * ls, bc=bc)
        hmid = hm_e.reshape(nseq, ls, d_ff)[:, ls - seqlen:].reshape(rows, d_ff)
        ffn_new = ge.reshape(nseq, ls, d_ff)[:, ls - (FFN_CONV - 1):]
    x = matmul(hmid, w_down, layer, bm=bm, bn=cfg["bn_down"], bk=cfg["bk_down"], residual=x, name="down_proj")
    return x, (s_new, conv_new, pool_new, k_new, v_new, ffn_new)


def _config(d_model, d_ff):
    off, n_pad = _in_layout(d_model)
    assert off["ga"] % IN_PAD == 0 and d_model % IN_PAD == 0
    return {
        "off": off,
        "bm": 1024,
        "bm_big": 2048,
        "bn_in": IN_PAD,
        "bn_merge": IN_PAD,
        "bn_out": _pick(d_model, 512, LANES),
        "bn_up": _pick(2 * d_ff, 512, LANES),
        "bn_down": _pick(d_model, 512, LANES),
        "bk_down": _pick(d_ff, 5504, LANES),
        "bc_ffn": _pick(d_ff, 256, LANES),
        "chunk": DELTA_CHUNK,
        "tb_prep": 1024,
        "heads": REC_HEADS,
    }


def kernel(x_prompt, x_sample, state_delta, state_dconv, state_pool, cache_swa_k, cache_swa_v, state_ffn_conv, norm1_w, w_in, dconv_w, dn_a_log, dn_dt_bias, dn_norm_w, pool_w, pool_scale, q_norm_w, k_norm_w, sinks, w_branch_a, w_branch_b, w_branch_c, w_out, norm2_w, w_up, ffn_conv_w, ffn_conv_b, w_down):
    depth = w_in.shape[0]
    bp, lp, d_model = x_prompt.shape
    bs, lsm, _ = x_sample.shape
    d_ff = w_down.shape[1]
    cfg = _config(d_model, d_ff)
    xp = x_prompt.reshape(bp * lp, d_model)
    xs = x_sample.reshape(bs * lsm, d_model)
    w_in_bf = prep_w_in(w_in, d_model)
    w_ba_bf, w_bb_bf, w_bc_bf, w_out_bf, w_up_bf, w_down_bf = (
        cast_bf16(w) for w in (w_branch_a, w_branch_b, w_branch_c, w_out, w_up, w_down))
    outs_p = [[] for _ in range(6)]
    outs_s = [[] for _ in range(6)]
    for l in range(depth):
        wts = (norm1_w[l], w_in_bf, dconv_w[l].astype(F32), dn_a_log[l], dn_dt_bias[l],
               dn_norm_w[l].astype(F32), pool_w[l], pool_scale[l], q_norm_w[l], k_norm_w[l], sinks[l],
               w_ba_bf, w_bb_bf, w_bc_bf, w_out_bf, norm2_w[l], w_up_bf, ffn_conv_w[l], ffn_conv_b[l],
               w_down_bf, state_delta.astype(F32))
        xp, st_p = _layer(xp, bp, lp, None, l, wts, cfg)
        st_in = (state_dconv[l], state_pool[l], cache_swa_k[l], cache_swa_v[l], state_ffn_conv[l])
        xs, st_s = _layer(xs, bs, lsm, st_in, l, wts, cfg)
        for lst, a in zip(outs_p, st_p):
            lst.append(a)
        for lst, a in zip(outs_s, st_s):
            lst.append(a)
    dt = x_prompt.dtype
    res = [xp.reshape(bp, lp, d_model), xs.reshape(bs, lsm, d_model)]
    for outs in (outs_p, outs_s):
        res.append(jnp.stack(outs[0]).astype(state_delta.dtype))
        for t in outs[1:]:
            res.append(jnp.stack(t).astype(dt))
    return tuple(res)
```

```python
import functools
import math

import jax
import jax.numpy as jnp
from jax import lax
from jax.experimental import pallas as pl
from jax.experimental.pallas import tpu as pltpu

F32 = jnp.float32
BF16 = jnp.bfloat16
EPS = 1e-6
NEG = -1e30

LANES = 128
SUBLANES = 8
VMEM_LIMIT = 56 * 1024 * 1024

DN_HEADS = 16
DN_D = 128
DN_W = DN_HEADS * DN_D
DN_CONV = 4
POOL_WINDOWS = (2, 4, 8, 16)
POOL_GROUP = 256
POOL_W = 1024
POOL_HIST = 15
SWA_HEADS = 16
SWA_KV = 4
SWA_HD = 64
SWA_W = SWA_HEADS * SWA_HD
SWA_KW = SWA_KV * SWA_HD
WINDOW = 128
FFN_CONV = 3

DELTA_CHUNK = 128
SAMPLE_SEQ_ROWS = 32
REC_HEADS = 16

SDS = jax.ShapeDtypeStruct


def _cparams(sem, vmem=VMEM_LIMIT):
    return pltpu.CompilerParams(dimension_semantics=sem, vmem_limit_bytes=vmem)


def _pick(n, target, mult=16):
    best = None
    for d in range(mult, min(n, target) + 1, mult):
        if n % d == 0:
            best = d
    assert best is not None, (n, target, mult)
    return best


def _sigmoid(x):
    return 1.0 / (1.0 + jnp.exp(-x))


def _silu(x):
    return x * _sigmoid(x)


def _dot(a, b):
    return jnp.dot(a.astype(BF16), b.astype(BF16), preferred_element_type=F32)


def _bdot(a, b):
    return lax.dot_general(a.astype(BF16), b.astype(BF16), (((2,), (1,)), ((0,), (0,))),
                           preferred_element_type=F32)


def _bdot_nt(a, b):
    return lax.dot_general(a.astype(BF16), b.astype(BF16), (((2,), (2,)), ((0,), (0,))),
                           preferred_element_type=F32)


def _bdot_tn(a, b):
    return lax.dot_general(a.astype(BF16), b.astype(BF16), (((1,), (1,)), ((0,), (0,))),
                           preferred_element_type=F32)


def _rmsnorm_kernel(x_ref, w_ref, o_ref):
    x = x_ref[...]
    ms = jnp.mean(x * x, axis=-1, keepdims=True)
    o_ref[...] = (x * lax.rsqrt(ms + EPS) * w_ref[...]).astype(o_ref.dtype)


def rmsnorm_bf16(x, w):
    rows, d = x.shape
    br = _pick(rows, 256, SUBLANES)
    return pl.pallas_call(
        _rmsnorm_kernel,
        out_shape=SDS((rows, d), BF16),
        grid=(rows // br,),
        in_specs=[pl.BlockSpec((br, d), lambda i: (i, 0)), pl.BlockSpec((1, d), lambda i: (0, 0))],
        out_specs=pl.BlockSpec((br, d), lambda i: (i, 0)),
        compiler_params=_cparams(("parallel",)),
        name="rmsnorm",
    )(x, w.reshape(1, d))


def _mm_kernel(*refs, nk, has_res):
    if has_res:
        a_ref, w_ref, r_ref, o_ref = refs
    else:
        a_ref, w_ref, o_ref = refs
        r_ref = None
    acc = jnp.dot(a_ref[...], w_ref[...], preferred_element_type=F32)
    if nk == 1:
        if has_res:
            acc = acc + r_ref[...]
        o_ref[...] = acc.astype(o_ref.dtype)
    else:
        k = pl.program_id(2)

        @pl.when(k == 0)
        def _():
            o_ref[...] = (acc + r_ref[...]) if has_res else acc

        @pl.when(k > 0)
        def _():
            o_ref[...] += acc


def matmul(a, w, layer, *, bm, bn, bk=None, residual=None, out_dtype=F32, name="matmul"):
    m, kdim = a.shape
    n = w.shape[2]
    bk = kdim if bk is None else bk
    nk = kdim // bk
    assert m % bm == 0 and n % bn == 0 and kdim % bk == 0
    assert nk == 1 or out_dtype == F32
    in_specs = [pl.BlockSpec((bm, bk), lambda i, j, k: (i, k)),
                pl.BlockSpec((None, bk, bn), lambda i, j, k: (layer, k, j))]
    args = [a, w]
    if residual is not None:
        in_specs.append(pl.BlockSpec((bm, bn), lambda i, j, k: (i, j)))
        args.append(residual)
    return pl.pallas_call(
        functools.partial(_mm_kernel, nk=nk, has_res=residual is not None),
        out_shape=SDS((m, n), out_dtype),
        grid=(m // bm, n // bn, nk),
        in_specs=in_specs,
        out_specs=pl.BlockSpec((bm, bn), lambda i, j, k: (i, j)),
        compiler_params=_cparams(("parallel", "parallel", "arbitrary")),
        name=name,
    )(*args)


def _cast_kernel(x_ref, o_ref):
    o_ref[...] = x_ref[...].astype(o_ref.dtype)


CAST_BLOCK_ELEMS = 2 * 1024 * 1024


def cast_bf16(w):
    depth, kdim, n = w.shape
    bc = _pick(n, 8192, LANES)
    br = _pick(kdim, max(SUBLANES, CAST_BLOCK_ELEMS // bc), SUBLANES)
    spec = pl.BlockSpec((1, br, bc), lambda l, i, j: (l, i, j))
    return pl.pallas_call(
        _cast_kernel,
        out_shape=SDS(w.shape, BF16),
        grid=(depth, kdim // br, n // bc),
        in_specs=[spec],
        out_specs=spec,
        compiler_params=_cparams(("parallel", "parallel", "parallel")),
        name="cast_bf16",
    )(w)


W_IN_BLOCK = 256


def _w_in_kernel(w1_ref, w2_ref, o_ref, *, n_a, n_b, shift):
    j = pl.program_id(1)

    def emit(rows):
        o_ref[...] = rows.T.astype(o_ref.dtype)

    @pl.when(jnp.logical_or(j < n_a, j == n_a + n_b))
    def _():
        emit(w1_ref[...])

    @pl.when(jnp.logical_and(j >= n_a, j < n_a + n_b))
    def _():
        emit(jnp.concatenate([w1_ref[shift:, :], w2_ref[...]], axis=0))

    @pl.when(j > n_a + n_b)
    def _():
        o_ref[...] = jnp.zeros(o_ref.shape, o_ref.dtype)


def prep_w_in(w_in, d_model):
    depth, kdim, n_raw = w_in.shape
    offs, n_pad = _in_layout(d_model)
    blk = W_IN_BLOCK
    shift = 2 * DN_HEADS
    n_a = 4 * DN_W // blk
    n_b = (offs["ba"] - 4 * DN_W) // blk
    assert 4 * DN_W % blk == 0 and (offs["ba"] - 4 * DN_W) % blk == 0 and n_pad % blk == 0
    assert n_raw == 4 * DN_W + shift + n_b * blk and blk % shift == 0
    wt = jnp.swapaxes(w_in, 1, 2)

    def src(j):
        return jnp.where(j < n_a + n_b, j, n_a)

    return pl.pallas_call(
        functools.partial(_w_in_kernel, n_a=n_a, n_b=n_b, shift=shift),
        out_shape=SDS((depth, kdim, n_pad), BF16),
        grid=(depth, n_pad // blk),
        in_specs=[pl.BlockSpec((None, blk, kdim), lambda l, j: (l, src(j), 0)),
                  pl.BlockSpec((None, shift, kdim),
                               lambda l, j: (l, jnp.minimum(src(j) + 1, n_a + n_b) * (blk // shift), 0))],
        out_specs=pl.BlockSpec((None, kdim, blk), lambda l, j: (l, 0, j)),
        compiler_params=_cparams(("parallel", "parallel")),
        name="prep_w_in",
    )(wt, wt)


def _upffn_kernel(a_ref, wg_ref, wv_ref, cw_ref, cb_ref, o_ref, tail_ref, g_scr, carry_scr, *, bm, tiles_per_seq):
    i = pl.program_id(0)
    j = pl.program_id(1)
    a = a_ref[...]
    gate = jnp.dot(a, wg_ref[...], preferred_element_type=F32)
    val = jnp.dot(a, wv_ref[...], preferred_element_type=F32)

    @pl.when(i % tiles_per_seq == 0)
    def _():
        g_scr[0:SUBLANES, :] = jnp.zeros((SUBLANES, g_scr.shape[1]), F32)

    @pl.when(i % tiles_per_seq != 0)
    def _():
        g_scr[0:SUBLANES, :] = carry_scr[j]

    g_scr[SUBLANES:SUBLANES + bm, :] = gate
    conv = gate * cw_ref[FFN_CONV - 1:FFN_CONV, :]
    for s in range(1, FFN_CONV):
        conv = conv + g_scr[SUBLANES - s:SUBLANES - s + bm, :] * cw_ref[FFN_CONV - 1 - s:FFN_CONV - s, :]
    o_ref[...] = (_silu(conv + cb_ref[...]) * val).astype(o_ref.dtype)
    tail = gate[bm - SUBLANES:bm, :]
    tail_ref[...] = tail
    carry_scr[j] = tail


def up_ffn_fused(a, w_up, layer, conv_w, conv_b, d_ff, *, bm, bc, seqlen):
    m, kdim = a.shape
    assert m % bm == 0 and seqlen % bm == 0 and d_ff % bc == 0
    nb = d_ff // bc
    return pl.pallas_call(
        functools.partial(_upffn_kernel, bm=bm, tiles_per_seq=seqlen // bm),
        out_shape=(SDS((m, d_ff), BF16), SDS((m // bm * SUBLANES, d_ff), F32)),
        grid=(m // bm, nb),
        in_specs=[pl.BlockSpec((bm, kdim), lambda i, j: (i, 0)),
                  pl.BlockSpec((None, kdim, bc), lambda i, j: (layer, 0, j)),
                  pl.BlockSpec((None, kdim, bc), lambda i, j: (layer, 0, nb + j)),
                  pl.BlockSpec((FFN_CONV, bc), lambda i, j: (0, j)),
                  pl.BlockSpec((1, bc), lambda i, j: (0, j))],
        out_specs=(pl.BlockSpec((bm, bc), lambda i, j: (i, j)), pl.BlockSpec((SUBLANES, bc), lambda i, j: (i, j))),
        scratch_shapes=[pltpu.VMEM((bm + SUBLANES, bc), F32), pltpu.VMEM((nb, SUBLANES, bc), F32)],
        compiler_params=_cparams(("arbitrary", "arbitrary")),
        name="up_ffn",
    )(a, w_up, w_up, conv_w.astype(F32), conv_b.reshape(1, d_ff).astype(F32))


def _dprep_kernel(xq_ref, xk_ref, xv_ref, hq_ref, hk_ref, hv_ref, d_ref, cq_ref, ck_ref, cv_ref, aneg_ref, dtb_ref,
                  q_ref, k_ref, v_ref, g_ref, b_ref, xx_scr, gam_scr, beta_scr, *, tb, ls, chunk, valid_from):
    i = pl.program_id(0)
    h = pl.program_id(1)
    row = lax.broadcasted_iota(jnp.int32, (tb, LANES), 0)
    lane = lax.broadcasted_iota(jnp.int32, (tb, LANES), 1)
    pos = (i * tb + row) % ls
    valid = pos >= valid_from

    @pl.when(h == 0)
    def _():
        d = d_ref[...]
        beta = jnp.where(valid, jnp.where(lane < DN_HEADS, _sigmoid(d), 0.0), 0.0)
        sp = d + dtb_ref[...]
        sp = jnp.maximum(sp, 0.0) + jnp.log(1.0 + jnp.exp(-jnp.abs(sp)))
        is_alpha = jnp.where(lane >= DN_HEADS, jnp.where(lane < 2 * DN_HEADS, 1, 0), 0)
        g = jnp.where(valid, jnp.where(is_alpha == 1, aneg_ref[...] * sp, 0.0), 0.0)
        pc = pos % chunk
        s = 1
        while s < chunk:
            g = g + jnp.where(pc >= s, pltpu.roll(g, s, 0), 0.0)
            s *= 2
        for scr, t in ((gam_scr, g), (beta_scr, beta)):
            for piece in range(3):
                tb16 = t.astype(BF16)
                scr[piece] = tb16
                t = t - tb16.astype(F32)

    fresh_tile = (i * tb) % ls == 0

    def conv_silu(x_ref, halo_ref, cw_ref):
        xx_scr[0:SUBLANES, :] = jnp.where(fresh_tile, 0.0, halo_ref[...])
        xx_scr[SUBLANES:SUBLANES + tb, :] = x_ref[...]
        acc = x_ref[...] * cw_ref[DN_CONV - 1:DN_CONV, :]
        for s in range(1, DN_CONV):
            acc = acc + xx_scr[SUBLANES - s:SUBLANES - s + tb, :] * cw_ref[DN_CONV - 1 - s:DN_CONV - s, :]
        return _silu(acc)

    def mask(t):
        return t if valid_from == 0 else jnp.where(valid, t, 0.0)

    q = conv_silu(xq_ref, hq_ref, cq_ref)
    q_ref[...] = mask(q * (lax.rsqrt(jnp.sum(q * q, axis=-1, keepdims=True) + EPS) * (DN_D ** -0.5)))
    k = conv_silu(xk_ref, hk_ref, ck_ref)
    k_ref[...] = mask(k * lax.rsqrt(jnp.sum(k * k, axis=-1, keepdims=True) + EPS))
    v_ref[...] = mask(conv_silu(xv_ref, hv_ref, cv_ref))

    sel_r = lax.broadcasted_iota(jnp.int32, (LANES, LANES), 0)

    def lane_broadcast(scr, col):
        onehot = jnp.where(sel_r == col, 1.0, 0.0).astype(BF16)
        out = jnp.dot(scr[2], onehot, preferred_element_type=F32)
        out = out + jnp.dot(scr[1], onehot, preferred_element_type=F32)
        return out + jnp.dot(scr[0], onehot, preferred_element_type=F32)

    b_ref[...] = lane_broadcast(beta_scr, h)
    g_ref[...] = lane_broadcast(gam_scr, h + DN_HEADS)


def delta_prep(x, xcol0, dt, dcol, conv_w, a_log, dt_bias, *, rows, ls, chunk, valid_from, tb):
    assert rows % tb == 0 and tb % chunk == 0 and tb % SUBLANES == 0
    assert (valid_from == 0 and ls % tb == 0) or (valid_from >= DN_CONV - 1 and tb % ls == 0)
    hb = tb // SUBLANES
    aneg = jnp.zeros((1, LANES), F32).at[0, DN_HEADS:2 * DN_HEADS].set(-jnp.exp(a_log.astype(F32)))
    dtb = jnp.zeros((1, LANES), F32).at[0, DN_HEADS:2 * DN_HEADS].set(dt_bias.astype(F32))

    def xspec(off):
        return pl.BlockSpec((tb, LANES), lambda i, h: (i, xcol0 + off + h))

    def hspec(off):
        return pl.BlockSpec((SUBLANES, LANES), lambda i, h: (jnp.maximum(i * hb - 1, 0), xcol0 + off + h))

    def cspec(off):
        return pl.BlockSpec((DN_CONV, LANES), lambda i, h: (0, off + h))

    ospec = pl.BlockSpec((tb, LANES), lambda i, h: (i, h))
    out = SDS((rows, DN_W), F32)
    return pl.pallas_call(
        functools.partial(_dprep_kernel, tb=tb, ls=ls, chunk=chunk, valid_from=valid_from),
        out_shape=(out,) * 5,
        grid=(rows // tb, DN_HEADS),
        in_specs=[xspec(0), xspec(DN_HEADS), xspec(2 * DN_HEADS), hspec(0), hspec(DN_HEADS), hspec(2 * DN_HEADS),
                  pl.BlockSpec((tb, LANES), lambda i, h: (i, dcol)),
                  cspec(0), cspec(DN_HEADS), cspec(2 * DN_HEADS),
                  pl.BlockSpec((1, LANES), lambda i, h: (0, 0)), pl.BlockSpec((1, LANES), lambda i, h: (0, 0))],
        out_specs=(ospec,) * 5,
        scratch_shapes=[pltpu.VMEM((tb + SUBLANES, LANES), F32), pltpu.VMEM((3, tb, LANES), BF16),
                        pltpu.VMEM((3, tb, LANES), BF16)],
        compiler_params=_cparams(("parallel", "arbitrary")),
        name="delta_prep",
    )(x, x, x, x, x, x, dt, conv_w, conv_w, conv_w, aneg, dtb)


def _drec_kernel(*refs, chunk, heads, nsteps, zero_init):
    if zero_init:
        q_ref, k_ref, v_ref, g_ref, b_ref, z_ref, nw_ref, y_ref, sf_ref, s_scr = refs
        s0_ref = None
    else:
        q_ref, k_ref, v_ref, g_ref, b_ref, z_ref, s0_ref, nw_ref, y_ref, sf_ref, s_scr = refs
    n = pl.program_id(2)

    @pl.when(n == 0)
    def _():
        s_scr[...] = jnp.zeros(s_scr.shape, F32) if zero_init else s0_ref[0]

    ri = lax.broadcasted_iota(jnp.int32, (chunk, chunk), 0)
    ci = lax.broadcasted_iota(jnp.int32, (chunk, chunk), 1)
    incl = ri >= ci
    strict = ri > ci
    eye = jnp.where(ri == ci, 1.0, 0.0)
    levels = int(math.log2(chunk))

    def heads_of(ref):
        return jnp.stack([ref[:, j * DN_D:(j + 1) * DN_D] for j in range(heads)], axis=0)

    q = heads_of(q_ref)
    k = heads_of(k_ref)
    v = heads_of(v_ref)
    gb = heads_of(g_ref)
    bb = heads_of(b_ref)
    s = s_scr[...]
    eg = jnp.exp(gb)
    glast = gb[:, chunk - 1:chunk, :]
    ek = jnp.exp(glast - gb)
    tot = jnp.exp(glast)
    gc = gb[:, :, :chunk]
    bc = bb[:, :, :chunk]
    diff = gc - jnp.swapaxes(gc, 1, 2)
    decay = jnp.where(incl, jnp.exp(jnp.where(incl, diff, 0.0)), 0.0)
    kb = k.astype(BF16)
    kk = _bdot_nt(kb, kb)
    qk = _bdot_nt(q, kb)
    x = jnp.where(strict, -(bc * kk * decay), 0.0)
    p = eye + x
    xb = x.astype(BF16)
    x = _bdot(xb, xb)
    for lvl in range(1, levels):
        xb = x.astype(BF16)
        pn = p + _bdot(xb, p)
        if lvl < levels - 1:
            x = _bdot(xb, xb)
        p = pn
    rhs = jnp.concatenate([bb * eg * k, bb * v], axis=2)
    sol = _bdot(p, rhs)
    wk = sol[:, :, :DN_D]
    ub = sol[:, :, DN_D:]
    sb = s.astype(BF16)
    u = ub - _bdot(wk, sb)
    o = _bdot(eg * q, sb) + _bdot(qk * decay, u)
    s_scr[...] = tot * s + _bdot_tn(ek * k, u)
    ms = jnp.mean(o * o, axis=-1, keepdims=True)
    on = o * lax.rsqrt(ms + EPS) * nw_ref[...]
    for j in range(heads):
        sl = slice(j * DN_D, (j + 1) * DN_D)
        y_ref[:, sl] = (on[j] * _silu(z_ref[:, sl])).astype(y_ref.dtype)

    @pl.when(n == nsteps - 1)
    def _():
        sf_ref[0] = s_scr[...]


def delta_recurrence(q, k, v, g, b, z, zcol0, s0, layer, norm_w, *, nseq, ls, chunk, heads):
    rows = nseq * ls
    nsteps = ls // chunk
    hw = heads * DN_D
    ispec = pl.BlockSpec((chunk, hw), lambda bi, hg, n: (bi * nsteps + n, hg))
    zc = zcol0 // heads
    assert zcol0 % heads == 0
    sspec = pl.BlockSpec((1, heads, DN_D, DN_D), lambda bi, hg, n: (bi, hg, 0, 0))
    in_specs = [ispec, ispec, ispec, ispec, ispec,
                pl.BlockSpec((chunk, hw), lambda bi, hg, n: (bi * nsteps + n, zc + hg))]
    args = [q, k, v, g, b, z]
    if s0 is not None:
        in_specs.append(pl.BlockSpec((None, 1, heads, DN_D, DN_D), lambda bi, hg, n: (layer, bi, hg, 0, 0)))
        args.append(s0)
    in_specs.append(pl.BlockSpec((1, DN_D), lambda bi, hg, n: (0, 0)))
    args.append(norm_w.reshape(1, DN_D))
    return pl.pallas_call(
        functools.partial(_drec_kernel, chunk=chunk, heads=heads, nsteps=nsteps, zero_init=s0 is None),
        out_shape=(SDS((rows, DN_W), BF16), SDS((nseq, DN_HEADS, DN_D, DN_D), F32)),
        grid=(nseq, DN_HEADS // heads, nsteps),
        in_specs=in_specs,
        out_specs=(ispec, sspec),
        scratch_shapes=[pltpu.VMEM((heads, DN_D, DN_D), F32)],
        compiler_params=_cparams(("parallel", "parallel", "arbitrary")),
        name="delta_recurrence",
    )(*args)


def _pool_kernel(x_ref, halo_ref, w_ref, sc_ref, y_ref, xx_scr, *, tb, ls):
    i = pl.program_id(0)
    hist = 2 * SUBLANES
    row = lax.broadcasted_iota(jnp.int32, (tb, POOL_GROUP), 0)
    pos = (i * tb + row) % ls
    xx_scr[0:hist, :] = halo_ref[...]
    xx_scr[hist:hist + tb, :] = x_ref[...]
    for gi, w in enumerate(POOL_WINDOWS):
        cs = slice(gi * POOL_GROUP, (gi + 1) * POOL_GROUP)
        x = x_ref[:, cs]
        tot = x
        for s in range(1, w):
            tap = xx_scr[hist - s:hist - s + tb, cs]
            tot = tot + jnp.where(pos >= s, tap, 0.0)
        cnt = jnp.minimum(pos + 1, w).astype(F32)
        pooled = tot / cnt - x
        y = _dot(pooled, w_ref[gi]) * sc_ref[:, cs]
        y_ref[:, cs] = y.astype(y_ref.dtype)


def pool_mixer(x, xcol, w_pool, scale, *, rows, ls, tb):
    assert rows % tb == 0 and tb % (2 * SUBLANES) == 0
    hb = tb // (2 * SUBLANES)
    return pl.pallas_call(
        functools.partial(_pool_kernel, tb=tb, ls=ls),
        out_shape=SDS((rows, POOL_W), BF16),
        grid=(rows // tb,),
        in_specs=[pl.BlockSpec((tb, POOL_W), lambda i: (i, xcol)),
                  pl.BlockSpec((2 * SUBLANES, POOL_W), lambda i: (jnp.maximum(i * hb - 1, 0), xcol)),
                  pl.BlockSpec((len(POOL_WINDOWS), POOL_GROUP, POOL_GROUP), lambda i: (0, 0, 0)),
                  pl.BlockSpec((1, POOL_W), lambda i: (0, 0))],
        out_specs=pl.BlockSpec((tb, POOL_W), lambda i: (i, 0)),
        scratch_shapes=[pltpu.VMEM((tb + 2 * SUBLANES, POOL_W), F32)],
        compiler_params=_cparams(("arbitrary",)),
        name="pool_mixer",
    )(x, x, w_pool.astype(BF16), scale.reshape(1, POOL_W).astype(F32))


def _group_rms(x, w, width):
    rows = x.shape[0]
    lane = lax.broadcasted_iota(jnp.int32, (rows, LANES), 1)
    lo = lane < SWA_HD
    outs = []
    for c in range(width // LANES):
        t = x[:, c * LANES:(c + 1) * LANES]
        t2 = t * t
        tot = jnp.sum(t2, axis=-1, keepdims=True)
        slo = jnp.sum(jnp.where(lo, t2, 0.0), axis=-1, keepdims=True)
        ms = jnp.where(lo, slo, tot - slo) * (1.0 / SWA_HD)
        outs.append(t * lax.rsqrt(ms + EPS))
    return jnp.concatenate(outs, axis=1) * w


def _swa_prep_kernel(q_ref, k_ref, v_ref, qw_ref, kw_ref, pm_ref, qo_ref, ko_ref, kx_ref, vx_ref):
    qo_ref[...] = _group_rms(q_ref[...], qw_ref[...], SWA_W).astype(qo_ref.dtype)
    kn = _group_rms(k_ref[...], kw_ref[...], SWA_KW)
    ko_ref[...] = kn
    kx_ref[...] = jnp.dot(kn.astype(BF16), pm_ref[...], preferred_element_type=F32).astype(BF16)
    vx_ref[...] = jnp.dot(v_ref[...].astype(BF16), pm_ref[...], preferred_element_type=F32).astype(BF16)


def _two_slot_matrix():
    src = jnp.arange(SWA_KW)
    grp, d = src // SWA_HD, src % SWA_HD
    dst = jnp.arange(4 * SWA_KW)
    dgrp, slot, dd = dst // (4 * SWA_HD), (dst % (4 * SWA_HD)) // SWA_HD, dst % SWA_HD
    keep = (slot == 0) | (slot == 3)
    hit = (grp[:, None] == dgrp[None, :]) & (d[:, None] == dd[None, :]) & keep[None, :]
    return hit.astype(BF16)


def swa_prep(x, qcol, kcol, vcol, q_norm_w, k_norm_w, *, rows, tb):
    qw = (jnp.tile(q_norm_w.astype(F32), SWA_HEADS) * (SWA_HD ** -0.5)).reshape(1, SWA_W)
    kw = jnp.tile(k_norm_w.astype(F32), SWA_KV).reshape(1, SWA_KW)
    wide = pl.BlockSpec((tb, SWA_W), lambda i: (i, 0))
    return pl.pallas_call(
        _swa_prep_kernel,
        out_shape=(SDS((rows, SWA_W), BF16), SDS((rows, SWA_KW), F32), SDS((rows, SWA_W), BF16),
                   SDS((rows, SWA_W), BF16)),
        grid=(rows // tb,),
        in_specs=[pl.BlockSpec((tb, SWA_W), lambda i: (i, qcol)), pl.BlockSpec((tb, SWA_KW), lambda i: (i, kcol)),
                  pl.BlockSpec((tb, SWA_KW), lambda i: (i, vcol)),
                  pl.BlockSpec((1, SWA_W), lambda i: (0, 0)), pl.BlockSpec((1, SWA_KW), lambda i: (0, 0)),
                  pl.BlockSpec((SWA_KW, SWA_W), lambda i: (0, 0))],
        out_specs=(wide, pl.BlockSpec((tb, SWA_KW), lambda i: (i, 0)), wide, wide),
        compiler_params=_cparams(("parallel",)),
        name="swa_prep",
    )(x, x, x, qw, kw, _two_slot_matrix())


def _swa_kernel(slope_ref, sink_ref, q_ref, kp_ref, kc_ref, vp_ref, vc_ref, o_ref, *, qb, first_has_prev):
    i = pl.program_id(1)
    r = lax.broadcasted_iota(jnp.int32, (qb, WINDOW), 0)
    c = lax.broadcasted_iota(jnp.int32, (qb, WINDOW), 1)
    dist_p = WINDOW + r - c
    dist_c = r - c
    ok_p = dist_p <= WINDOW
    if not first_has_prev:
        ok_p = jnp.logical_and(ok_p, i > 0)
    ok_c = dist_c >= 0
    dpf = dist_p.astype(F32)
    dcf = dist_c.astype(F32)
    gh = SWA_HEADS // SWA_KV

    def per_head(ref, lane_block):
        return jnp.stack([ref[:, lane_block(h) * LANES:(lane_block(h) + 1) * LANES] for h in range(SWA_HEADS)], axis=0)

    slot = lambda h: (h // gh) * 2 + h % 2
    q = per_head(q_ref, lambda h: h // 2)
    kp = per_head(kp_ref, slot)
    kc = per_head(kc_ref, slot)
    vp = per_head(vp_ref, slot)
    vc = per_head(vc_ref, slot)
    slope = slope_ref[...]
    sink = sink_ref[...][:, :, :1]
    sp = lax.dot_general(q, kp, (((2,), (2,)), ((0,), (0,))), preferred_element_type=F32)
    sc = lax.dot_general(q, kc, (((2,), (2,)), ((0,), (0,))), preferred_element_type=F32)
    sp = jnp.where(ok_p, sp - slope * dpf, NEG)
    sc = jnp.where(ok_c, sc - slope * dcf, NEG)
    m = jnp.maximum(jnp.maximum(jnp.max(sp, axis=-1, keepdims=True), jnp.max(sc, axis=-1, keepdims=True)), sink)
    ep = jnp.exp(sp - m)
    ec = jnp.exp(sc - m)
    den = jnp.sum(ep, axis=-1, keepdims=True) + jnp.sum(ec, axis=-1, keepdims=True) + jnp.exp(sink - m)
    inv = 1.0 / den
    o = _bdot(ep * inv, vp) + _bdot(ec * inv, vc)
    for t in range(SWA_HEADS // 2):
        o_ref[:, t * LANES:(t + 1) * LANES] = (o[2 * t] + o[2 * t + 1]).astype(o_ref.dtype)


def swa_attention(q, kx_prev, kx_cur, vx_prev, vx_cur, sinks, *, nseq, nq, qb, prev_shift, first_has_prev):
    slopes = jnp.exp2(-8.0 * jnp.arange(1, SWA_HEADS + 1, dtype=F32) / SWA_HEADS)
    rep = lambda t: jnp.broadcast_to(t.astype(F32).reshape(SWA_HEADS, 1, 1), (SWA_HEADS, 1, LANES))
    qspec = pl.BlockSpec((qb, SWA_W), lambda b, i: (b * nq + i, 0))
    pspec = pl.BlockSpec((WINDOW, SWA_W), lambda b, i: (jnp.maximum(b * nq + i - prev_shift, 0), 0))
    cspec = pl.BlockSpec((WINDOW, SWA_W), lambda b, i: (b * nq + i, 0))
    tspec = pl.BlockSpec((SWA_HEADS, 1, LANES), lambda b, i: (0, 0, 0))
    return pl.pallas_call(
        functools.partial(_swa_kernel, qb=qb, first_has_prev=first_has_prev),
        out_shape=SDS((nseq * nq * qb, SWA_W), BF16),
        grid=(nseq, nq),
        in_specs=[tspec, tspec, qspec, pspec, cspec, pspec, cspec],
        out_specs=qspec,
        compiler_params=_cparams(("parallel", "arbitrary")),
        name="swa_attention",
    )(rep(slopes), rep(sinks), q, kx_prev, kx_cur, vx_prev, vx_cur)


def _two_slot(t):
    rows = t.shape[0]
    t = t.astype(BF16).reshape(rows, SWA_KV, SWA_HD)
    z = jnp.zeros_like(t)
    return jnp.concatenate([t, z, z, t], axis=-1).reshape(rows, SWA_KV * 4 * SWA_HD)


def _merge_kernel(ya_ref, yb_ref, yc_ref, wa_ref, wb_ref, wc_ref, ga_ref, gb_ref, gc_ref, o_ref):
    a = jnp.dot(ya_ref[...], wa_ref[...], preferred_element_type=F32)
    b = jnp.dot(yb_ref[...], wb_ref[...], preferred_element_type=F32)
    c = jnp.dot(yc_ref[...], wc_ref[...], preferred_element_type=F32)
    o = _sigmoid(ga_ref[...]) * a + _sigmoid(gb_ref[...]) * b + _sigmoid(gc_ref[...]) * c
    o_ref[...] = o.astype(o_ref.dtype)


def merge_branches(ya, yb, yc, wa, wb, wc, layer, proj, gcol0, d_model, *, bm, bn):
    m = ya.shape[0]
    gb0 = gcol0 // bn
    gstep = d_model // bn
    assert gcol0 % bn == 0 and d_model % bn == 0 and m % bm == 0

    def gspec(t):
        return pl.BlockSpec((bm, bn), lambda i, j: (i, gb0 + t * gstep + j))

    def wspec(kdim):
        return pl.BlockSpec((None, kdim, bn), lambda i, j: (layer, 0, j))

    return pl.pallas_call(
        _merge_kernel,
        out_shape=SDS((m, d_model), BF16),
        grid=(m // bm, d_model // bn),
        in_specs=[pl.BlockSpec((bm, DN_W), lambda i, j: (i, 0)), pl.BlockSpec((bm, POOL_W), lambda i, j: (i, 0)),
                  pl.BlockSpec((bm, SWA_W), lambda i, j: (i, 0)),
                  wspec(DN_W), wspec(POOL_W), wspec(SWA_W),
                  gspec(0), gspec(1), gspec(2)],
        out_specs=pl.BlockSpec((bm, bn), lambda i, j: (i, j)),
        compiler_params=_cparams(("parallel", "parallel")),
        name="merge_branches",
    )(ya, yb, yc, wa, wb, wc, proj, proj, proj)


def _ffn_act_kernel(g_ref, halo_ref, v_ref, cw_ref, cb_ref, o_ref, xx_scr, *, tb, ls):
    i = pl.program_id(0)
    bc = g_ref.shape[1]
    row = lax.broadcasted_iota(jnp.int32, (tb, bc), 0)
    pos = (i * tb + row) % ls
    xx_scr[0:SUBLANES, :] = halo_ref[...]
    xx_scr[SUBLANES:SUBLANES + tb, :] = g_ref[...]
    acc = g_ref[...] * cw_ref[FFN_CONV - 1:FFN_CONV, :]
    for s in range(1, FFN_CONV):
        tap = xx_scr[SUBLANES - s:SUBLANES - s + tb, :]
        acc = acc + jnp.where(pos >= s, tap, 0.0) * cw_ref[FFN_CONV - 1 - s:FFN_CONV - s, :]
    o_ref[...] = (_silu(acc + cb_ref[...]) * v_ref[...]).astype(o_ref.dtype)


def ffn_activation(gate, gcol0, val, vcol0, conv_w, conv_b, d_ff, *, rows, ls, tb, bc):
    assert rows % tb == 0 and d_ff % bc == 0 and gcol0 % bc == 0 and vcol0 % bc == 0
    hb = tb // SUBLANES
    g0 = gcol0 // bc
    v0 = vcol0 // bc
    return pl.pallas_call(
        functools.partial(_ffn_act_kernel, tb=tb, ls=ls),
        out_shape=SDS((rows, d_ff), BF16),
        grid=(rows // tb, d_ff // bc),
        in_specs=[pl.BlockSpec((tb, bc), lambda i, j: (i, g0 + j)),
                  pl.BlockSpec((SUBLANES, bc), lambda i, j: (jnp.maximum(i * hb - 1, 0), g0 + j)),
                  pl.BlockSpec((tb, bc), lambda i, j: (i, v0 + j)),
                  pl.BlockSpec((FFN_CONV, bc), lambda i, j: (0, j)),
                  pl.BlockSpec((1, bc), lambda i, j: (0, j))],
        out_specs=pl.BlockSpec((tb, bc), lambda i, j: (i, j)),
        scratch_shapes=[pltpu.VMEM((tb + SUBLANES, bc), F32)],
        compiler_params=_cparams(("parallel", "parallel")),
        name="ffn_activation",
    )(gate, gate, val, conv_w.astype(F32), conv_b.reshape(1, d_ff).astype(F32))


IN_PAD = 512


def _in_layout(d_model):
    off = {}
    c = 0
    for name, width in (("dq", DN_W), ("dk", DN_W), ("dv", DN_W), ("dz", DN_W),
                        ("pin", POOL_W), ("sq", SWA_W), ("sk", SWA_KW), ("sv", SWA_KW),
                        ("ga", d_model), ("gb", d_model), ("gc", d_model), ("ba", LANES)):
        off[name] = c
        c += width
    return off, -(-c // IN_PAD) * IN_PAD


def _ext(hist, new, ls):
    nseq, hr, c = hist.shape
    ln = new.shape[1]
    z = jnp.zeros((nseq, ls - hr - ln, c), new.dtype)
    return jnp.concatenate([z, hist.astype(new.dtype), new], axis=1).reshape(nseq * ls, c)


def _layer(x, nseq, seqlen, states, layer, wts, cfg):
    (n1, w_in, dconv_w, a_log, dt_bias, dn_norm_w, pool_w, pool_scale, q_norm_w, k_norm_w, sinks,
     w_ba, w_bb, w_bc, w_out, n2, w_up, fconv_w, fconv_b, w_down, state_delta) = wts
    rows, d_model = x.shape
    d_ff = w_down.shape[1]
    off = cfg["off"]
    fresh = states is None
    bm = _pick(seqlen if fresh else rows, cfg["bm"])
    bm_big = _pick(seqlen if fresh else rows, cfg["bm_big"])

    h = rmsnorm_bf16(x, n1)
    proj = matmul(h, w_in, layer, bm=bm_big, bn=cfg["bn_in"], name="in_proj")
    pb = proj
    offb = off
    proj3 = proj.reshape(nseq, seqlen, -1)

    def cols_a(name, width):
        return proj3[:, :, off[name]:off[name] + width]

    cols_b = cols_a

    if fresh:
        ls, chunk, vfrom = seqlen, cfg["chunk"], 0
        tbp = _pick(seqlen, cfg["tb_prep"], chunk)
        q, k, v, g, b = delta_prep(proj, 0, proj, off["ba"] // LANES, dconv_w, a_log, dt_bias,
                                   rows=rows, ls=ls, chunk=chunk, valid_from=0, tb=tbp)
        ya, s_new = delta_recurrence(q, k, v, g, b, proj, off["dz"] // LANES, None, layer, dn_norm_w,
                                     nseq=nseq, ls=ls, chunk=chunk, heads=cfg["heads"])
        conv_new = cols_a("dq", 3 * DN_W)[:, seqlen - (DN_CONV - 1):]
    else:
        conv_h, pool_h, k_h, v_h, ffn_h = states
        ls = SAMPLE_SEQ_ROWS
        vfrom = ls - seqlen
        xe = _ext(conv_h, cols_a("dq", 3 * DN_W), ls)
        de = _ext(jnp.zeros((nseq, 0, LANES), F32), cols_a("ba", LANES), ls)
        ze = _ext(jnp.zeros((nseq, 0, DN_W), F32), cols_a("dz", DN_W), ls)
        erows = nseq * ls
        q, k, v, g, b = delta_prep(xe, 0, de, 0, dconv_w, a_log, dt_bias,
                                   rows=erows, ls=ls, chunk=ls, valid_from=vfrom, tb=_pick(erows, 1024, ls))
        ya_e, s_new = delta_recurrence(q, k, v, g, b, ze, 0, state_delta, layer, dn_norm_w,
                                       nseq=nseq, ls=ls, chunk=ls, heads=cfg["heads"])
        ya = ya_e.reshape(nseq, ls, DN_W)[:, vfrom:].reshape(rows, DN_W)
        conv_new = xe.reshape(nseq, ls, 3 * DN_W)[:, ls - (DN_CONV - 1):]

    if fresh:
        yb = pool_mixer(pb, offb["pin"] // POOL_W, pool_w, pool_scale, rows=rows, ls=seqlen,
                        tb=_pick(seqlen, 512, 2 * SUBLANES))
        pool_new = cols_b("pin", POOL_W)[:, seqlen - POOL_HIST:]
    else:
        pe = _ext(pool_h, cols_b("pin", POOL_W), SAMPLE_SEQ_ROWS)
        erows = nseq * SAMPLE_SEQ_ROWS
        yb_e = pool_mixer(pe, 0, pool_w, pool_scale, rows=erows, ls=SAMPLE_SEQ_ROWS, tb=_pick(erows, 512, SAMPLE_SEQ_ROWS))
        yb = yb_e.reshape(nseq, SAMPLE_SEQ_ROWS, POOL_W)[:, SAMPLE_SEQ_ROWS - seqlen:].reshape(rows, POOL_W)
        pool_new = pe.reshape(nseq, SAMPLE_SEQ_ROWS, POOL_W)[:, SAMPLE_SEQ_ROWS - POOL_HIST:]

    qn, kn, kx, vx = swa_prep(pb, offb["sq"] // SWA_W, offb["sk"] // SWA_KW, offb["sv"] // SWA_KW,
                              q_norm_w, k_norm_w, rows=rows, tb=_pick(rows, 512, SUBLANES))
    vraw = cols_b("sv", SWA_KW)
    kn3 = kn.reshape(nseq, seqlen, SWA_KW)
    if fresh:
        yc = swa_attention(qn, kx, kx, vx, vx, sinks, nseq=nseq, nq=seqlen // WINDOW, qb=WINDOW,
                           prev_shift=1, first_has_prev=False)
        keep = min(WINDOW, seqlen)
        k_new = kn3[:, seqlen - keep:].reshape(nseq, keep, SWA_KV, SWA_HD)
        v_new = vraw[:, seqlen - keep:].reshape(nseq, keep, SWA_KV, SWA_HD)
    else:
        wb = k_h.shape[1]
        qpad = jnp.zeros((nseq, SUBLANES - seqlen, SWA_W), BF16)
        qe = jnp.concatenate([qn.reshape(nseq, seqlen, SWA_W), qpad], axis=1).reshape(nseq * SUBLANES, SWA_W)
        kpad = jnp.zeros((nseq, WINDOW - seqlen, SWA_W), BF16)
        kc = jnp.concatenate([kx.reshape(nseq, seqlen, SWA_W), kpad], axis=1).reshape(nseq * WINDOW, SWA_W)
        vc = jnp.concatenate([vx.reshape(nseq, seqlen, SWA_W), kpad], axis=1).reshape(nseq * WINDOW, SWA_W)
        kp = _two_slot(k_h.reshape(nseq * wb, SWA_KW))
        vp = _two_slot(v_h.reshape(nseq * wb, SWA_KW))
        yc_e = swa_attention(qe, kp, kc, vp, vc, sinks, nseq=nseq, nq=1, qb=SUBLANES,
                             prev_shift=0, first_has_prev=True)
        yc = yc_e.reshape(nseq, SUBLANES, SWA_W)[:, :seqlen].reshape(rows, SWA_W)
        k_new = jnp.concatenate([k_h, kn3.reshape(nseq, seqlen, SWA_KV, SWA_HD).astype(k_h.dtype)], axis=1)[:, -wb:]
        v_new = jnp.concatenate([v_h, vraw.reshape(nseq, seqlen, SWA_KV, SWA_HD).astype(v_h.dtype)], axis=1)[:, -wb:]

    merged = merge_branches(ya, yb, yc, w_ba, w_bb, w_bc, layer, pb, offb["ga"], d_model, bm=bm, bn=cfg["bn_merge"])
    x = matmul(merged, w_out, layer, bm=bm, bn=cfg["bn_out"], residual=x, name="out_proj")

    h2 = rmsnorm_bf16(x, n2)
    bc = cfg["bc_ffn"]
    if fresh:
        hmid, tail = up_ffn_fused(h2, w_up, layer, fconv_w, fconv_b, d_ff, bm=bm_big, bc=bc, seqlen=seqlen)
        tps = seqlen // bm_big
        tail4 = tail.reshape(nseq, tps, SUBLANES, d_ff)
        ffn_new = tail4[:, tps - 1, SUBLANES - (FFN_CONV - 1):]
    else:
        up = matmul(h2, w_up, layer, bm=bm, bn=cfg["bn_up"], name="up_proj")
        up3 = up.reshape(nseq, seqlen, 2 * d_ff)
        ls = SUBLANES
        ge = _ext(ffn_h, up3[:, :, :d_ff], ls)
        ve = _ext(jnp.zeros((nseq, 0, d_ff), F32), up3[:, :, d_ff:], ls)
        hm_e = ffn_activation(ge, 0, ve, 0, fconv_w, fconv_b, d_ff, rows=nseq * ls, ls=ls, tb=nseq * ls, bc=bc)
        hmid = hm_e.reshape(nseq, ls, d_ff)[:, ls - seqlen:].reshape(rows, d_ff)
        ffn_new = ge.reshape(nseq, ls, d_ff)[:, ls - (FFN_CONV - 1):]
    x = matmul(hmid, w_down, layer, bm=min(bm, cfg["bm_down"]), bn=cfg["bn_down"], residual=x, name="down_proj")
    return x, (s_new, conv_new, pool_new, k_new, v_new, ffn_new)


def _config(d_model, d_ff):
    off, n_pad = _in_layout(d_model)
    assert off["ga"] % IN_PAD == 0 and d_model % IN_PAD == 0
    return {
        "off": off,
        "bm": 1024,
        "bm_big": 2048,
        "bn_in": IN_PAD,
        "bn_merge": IN_PAD,
        "bn_out": _pick(d_model, 512, LANES),
        "bn_up": _pick(2 * d_ff, 512, LANES),
        "bm_down": 512,
        "bn_down": _pick(d_model, 512, LANES),
        "bc_ffn": _pick(d_ff, 256, LANES),
        "chunk": DELTA_CHUNK,
        "tb_prep": 2048,
        "heads": REC_HEADS,
    }


def kernel(x_prompt, x_sample, state_delta, state_dconv, state_pool, cache_swa_k, cache_swa_v, state_ffn_conv, norm1_w, w_in, dconv_w, dn_a_log, dn_dt_bias, dn_norm_w, pool_w, pool_scale, q_norm_w, k_norm_w, sinks, w_branch_a, w_branch_b, w_branch_c, w_out, norm2_w, w_up, ffn_conv_w, ffn_conv_b, w_down):
    depth = w_in.shape[0]
    bp, lp, d_model = x_prompt.shape
    bs, lsm, _ = x_sample.shape
    d_ff = w_down.shape[1]
    cfg = _config(d_model, d_ff)
    xp = x_prompt.reshape(bp * lp, d_model)
    xs = x_sample.reshape(bs * lsm, d_model)
    w_in_bf = prep_w_in(w_in, d_model)
    w_ba_bf, w_bb_bf, w_bc_bf, w_out_bf, w_up_bf, w_down_bf = (
        cast_bf16(w) for w in (w_branch_a, w_branch_b, w_branch_c, w_out, w_up, w_down))
    outs_p = [[] for _ in range(6)]
    outs_s = [[] for _ in range(6)]
    for l in range(depth):
        wts = (norm1_w[l], w_in_bf, dconv_w[l].astype(F32), dn_a_log[l], dn_dt_bias[l],
               dn_norm_w[l].astype(F32), pool_w[l], pool_scale[l], q_norm_w[l], k_norm_w[l], sinks[l],
               w_ba_bf, w_bb_bf, w_bc_bf, w_out_bf, norm2_w[l], w_up_bf, ffn_conv_w[l], ffn_conv_b[l],
               w_down_bf, state_delta.astype(F32))
        xp, st_p = _layer(xp, bp, lp, None, l, wts, cfg)
        st_in = (state_dconv[l], state_pool[l], cache_swa_k[l], cache_swa_v[l], state_ffn_conv[l])
        xs, st_s = _layer(xs, bs, lsm, st_in, l, wts, cfg)
        for lst, a in zip(outs_p, st_p):
            lst.append(a)
        for lst, a in zip(outs_s, st_s):
            lst.append(a)
    dt = x_prompt.dtype
    res = [xp.reshape(bp, lp, d_model), xs.reshape(bs, lsm, d_model)]
    for outs in (outs_p, outs_s):
        res.append(jnp.stack(outs[0]).astype(state_delta.dtype))
        for t in outs[1:]:
            res.append(jnp.stack(t).astype(dt))
    return tuple(res)
```

```python
import functools
import math

import jax
import jax.numpy as jnp
from jax import lax
from jax.experimental import pallas as pl
from jax.experimental.pallas import tpu as pltpu

F32 = jnp.float32
BF16 = jnp.bfloat16
EPS = 1e-6
NEG = -1e30

LANES = 128
SUBLANES = 8
VMEM_LIMIT = 56 * 1024 * 1024

DN_HEADS = 16
DN_D = 128
DN_W = DN_HEADS * DN_D
DN_CONV = 4
POOL_WINDOWS = (2, 4, 8, 16)
POOL_GROUP = 256
POOL_W = 1024
POOL_HIST = 15
SWA_HEADS = 16
SWA_KV = 4
SWA_HD = 64
SWA_W = SWA_HEADS * SWA_HD
SWA_KW = SWA_KV * SWA_HD
WINDOW = 128
FFN_CONV = 3

DELTA_CHUNK = 128
SAMPLE_SEQ_ROWS = 32
SAMPLE_DELTA_ROWS = 8
REC_HEADS = 16

SDS = jax.ShapeDtypeStruct


def _cparams(sem, vmem=VMEM_LIMIT):
    return pltpu.CompilerParams(dimension_semantics=sem, vmem_limit_bytes=vmem)


def _pick(n, target, mult=16):
    best = None
    for d in range(mult, min(n, target) + 1, mult):
        if n % d == 0:
            best = d
    assert best is not None, (n, target, mult)
    return best


def _sigmoid(x):
    return 1.0 / (1.0 + jnp.exp(-x))


def _silu(x):
    return x * _sigmoid(x)


def _dot(a, b):
    return jnp.dot(a.astype(BF16), b.astype(BF16), preferred_element_type=F32)


def _bdot(a, b):
    return lax.dot_general(a.astype(BF16), b.astype(BF16), (((2,), (1,)), ((0,), (0,))),
                           preferred_element_type=F32)


def _bdot_nt(a, b):
    return lax.dot_general(a.astype(BF16), b.astype(BF16), (((2,), (2,)), ((0,), (0,))),
                           preferred_element_type=F32)


def _bdot_tn(a, b):
    return lax.dot_general(a.astype(BF16), b.astype(BF16), (((1,), (1,)), ((0,), (0,))),
                           preferred_element_type=F32)


def _rmsnorm_kernel(x_ref, w_ref, o_ref):
    x = x_ref[...]
    ms = jnp.mean(x * x, axis=-1, keepdims=True)
    o_ref[...] = (x * lax.rsqrt(ms + EPS) * w_ref[...]).astype(o_ref.dtype)


def rmsnorm_bf16(x, w):
    rows, d = x.shape
    br = _pick(rows, 256, SUBLANES)
    return pl.pallas_call(
        _rmsnorm_kernel,
        out_shape=SDS((rows, d), BF16),
        grid=(rows // br,),
        in_specs=[pl.BlockSpec((br, d), lambda i: (i, 0)), pl.BlockSpec((1, d), lambda i: (0, 0))],
        out_specs=pl.BlockSpec((br, d), lambda i: (i, 0)),
        compiler_params=_cparams(("parallel",)),
        name="rmsnorm",
    )(x, w.reshape(1, d))


def _mm_kernel(*refs, nk, has_res):
    if has_res:
        a_ref, w_ref, r_ref, o_ref = refs
    else:
        a_ref, w_ref, o_ref = refs
        r_ref = None
    acc = jnp.dot(a_ref[...], w_ref[...], preferred_element_type=F32)
    if nk == 1:
        if has_res:
            acc = acc + r_ref[...]
        o_ref[...] = acc.astype(o_ref.dtype)
    else:
        k = pl.program_id(2)

        @pl.when(k == 0)
        def _():
            o_ref[...] = (acc + r_ref[...]) if has_res else acc

        @pl.when(k > 0)
        def _():
            o_ref[...] += acc


def matmul(a, w, layer, *, bm, bn, bk=None, residual=None, out_dtype=F32, name="matmul"):
    m, kdim = a.shape
    n = w.shape[2]
    bk = kdim if bk is None else bk
    nk = kdim // bk
    assert m % bm == 0 and n % bn == 0 and kdim % bk == 0
    assert nk == 1 or out_dtype == F32
    in_specs = [pl.BlockSpec((bm, bk), lambda i, j, k: (i, k)),
                pl.BlockSpec((None, bk, bn), lambda i, j, k: (layer, k, j))]
    args = [a, w]
    if residual is not None:
        in_specs.append(pl.BlockSpec((bm, bn), lambda i, j, k: (i, j)))
        args.append(residual)
    return pl.pallas_call(
        functools.partial(_mm_kernel, nk=nk, has_res=residual is not None),
        out_shape=SDS((m, n), out_dtype),
        grid=(m // bm, n // bn, nk),
        in_specs=in_specs,
        out_specs=pl.BlockSpec((bm, bn), lambda i, j, k: (i, j)),
        compiler_params=_cparams(("parallel", "parallel", "arbitrary")),
        name=name,
    )(*args)


def _cast_kernel(x_ref, o_ref):
    o_ref[...] = x_ref[...].astype(o_ref.dtype)


CAST_BLOCK_ELEMS = 2 * 1024 * 1024


def cast_bf16(w):
    depth, kdim, n = w.shape
    bc = _pick(n, 8192, LANES)
    br = _pick(kdim, max(SUBLANES, CAST_BLOCK_ELEMS // bc), SUBLANES)
    spec = pl.BlockSpec((1, br, bc), lambda l, i, j: (l, i, j))
    return pl.pallas_call(
        _cast_kernel,
        out_shape=SDS(w.shape, BF16),
        grid=(depth, kdim // br, n // bc),
        in_specs=[spec],
        out_specs=spec,
        compiler_params=_cparams(("parallel", "parallel", "parallel")),
        name="cast_bf16",
    )(w)


W_IN_BLOCK = 256


def _w_in_kernel(w1_ref, w2_ref, o_ref, *, n_a, n_b, shift):
    j = pl.program_id(1)

    def emit(rows):
        o_ref[...] = rows.T.astype(o_ref.dtype)

    @pl.when(jnp.logical_or(j < n_a, j == n_a + n_b))
    def _():
        emit(w1_ref[...])

    @pl.when(jnp.logical_and(j >= n_a, j < n_a + n_b))
    def _():
        emit(jnp.concatenate([w1_ref[shift:, :], w2_ref[...]], axis=0))

    @pl.when(j > n_a + n_b)
    def _():
        o_ref[...] = jnp.zeros(o_ref.shape, o_ref.dtype)


def prep_w_in(w_in, d_model):
    depth, kdim, n_raw = w_in.shape
    offs, n_pad = _in_layout(d_model)
    blk = W_IN_BLOCK
    shift = 2 * DN_HEADS
    n_a = 4 * DN_W // blk
    n_b = (offs["ba"] - 4 * DN_W) // blk
    assert 4 * DN_W % blk == 0 and (offs["ba"] - 4 * DN_W) % blk == 0 and n_pad % blk == 0
    assert n_raw == 4 * DN_W + shift + n_b * blk and blk % shift == 0
    wt = jnp.swapaxes(w_in, 1, 2)

    def src(j):
        return jnp.where(j < n_a + n_b, j, n_a)

    return pl.pallas_call(
        functools.partial(_w_in_kernel, n_a=n_a, n_b=n_b, shift=shift),
        out_shape=SDS((depth, kdim, n_pad), BF16),
        grid=(depth, n_pad // blk),
        in_specs=[pl.BlockSpec((None, blk, kdim), lambda l, j: (l, src(j), 0)),
                  pl.BlockSpec((None, shift, kdim),
                               lambda l, j: (l, jnp.minimum(src(j) + 1, n_a + n_b) * (blk // shift), 0))],
        out_specs=pl.BlockSpec((None, kdim, blk), lambda l, j: (l, 0, j)),
        compiler_params=_cparams(("parallel", "parallel")),
        name="prep_w_in",
    )(wt, wt)


def _upffn_kernel(a_ref, wg_ref, wv_ref, cw_ref, cb_ref, o_ref, tail_ref, g_scr, carry_scr, *, bm, tiles_per_seq):
    i = pl.program_id(0)
    j = pl.program_id(1)
    a = a_ref[...]
    gate = jnp.dot(a, wg_ref[...], preferred_element_type=F32)
    val = jnp.dot(a, wv_ref[...], preferred_element_type=F32)

    @pl.when(i % tiles_per_seq == 0)
    def _():
        g_scr[0:SUBLANES, :] = jnp.zeros((SUBLANES, g_scr.shape[1]), F32)

    @pl.when(i % tiles_per_seq != 0)
    def _():
        g_scr[0:SUBLANES, :] = carry_scr[j]

    g_scr[SUBLANES:SUBLANES + bm, :] = gate
    conv = gate * cw_ref[FFN_CONV - 1:FFN_CONV, :]
    for s in range(1, FFN_CONV):
        conv = conv + g_scr[SUBLANES - s:SUBLANES - s + bm, :] * cw_ref[FFN_CONV - 1 - s:FFN_CONV - s, :]
    o_ref[...] = (_silu(conv + cb_ref[...]) * val).astype(o_ref.dtype)
    tail = gate[bm - SUBLANES:bm, :]
    tail_ref[...] = tail
    carry_scr[j] = tail


def up_ffn_fused(a, w_up, layer, conv_w, conv_b, d_ff, *, bm, bc, seqlen):
    m, kdim = a.shape
    assert m % bm == 0 and seqlen % bm == 0 and d_ff % bc == 0
    nb = d_ff // bc
    return pl.pallas_call(
        functools.partial(_upffn_kernel, bm=bm, tiles_per_seq=seqlen // bm),
        out_shape=(SDS((m, d_ff), BF16), SDS((m // bm * SUBLANES, d_ff), F32)),
        grid=(m // bm, nb),
        in_specs=[pl.BlockSpec((bm, kdim), lambda i, j: (i, 0)),
                  pl.BlockSpec((None, kdim, bc), lambda i, j: (layer, 0, j)),
                  pl.BlockSpec((None, kdim, bc), lambda i, j: (layer, 0, nb + j)),
                  pl.BlockSpec((FFN_CONV, bc), lambda i, j: (0, j)),
                  pl.BlockSpec((1, bc), lambda i, j: (0, j))],
        out_specs=(pl.BlockSpec((bm, bc), lambda i, j: (i, j)), pl.BlockSpec((SUBLANES, bc), lambda i, j: (i, j))),
        scratch_shapes=[pltpu.VMEM((bm + SUBLANES, bc), F32), pltpu.VMEM((nb, SUBLANES, bc), F32)],
        compiler_params=_cparams(("arbitrary", "arbitrary")),
        name="up_ffn",
    )(a, w_up, w_up, conv_w.astype(F32), conv_b.reshape(1, d_ff).astype(F32))


def _dprep_kernel(xq_ref, xk_ref, xv_ref, hq_ref, hk_ref, hv_ref, d_ref, cq_ref, ck_ref, cv_ref, aneg_ref, dtb_ref,
                  q_ref, k_ref, v_ref, g_ref, b_ref, xx_scr, gam_scr, beta_scr, *, tb, ls, chunk, valid_from):
    i = pl.program_id(0)
    h = pl.program_id(1)
    row = lax.broadcasted_iota(jnp.int32, (tb, LANES), 0)
    lane = lax.broadcasted_iota(jnp.int32, (tb, LANES), 1)
    pos = (i * tb + row) % ls
    valid = pos >= valid_from

    @pl.when(h == 0)
    def _():
        d = d_ref[...]
        beta = jnp.where(valid, jnp.where(lane < DN_HEADS, _sigmoid(d), 0.0), 0.0)
        sp = d + dtb_ref[...]
        sp = jnp.maximum(sp, 0.0) + jnp.log(1.0 + jnp.exp(-jnp.abs(sp)))
        is_alpha = jnp.where(lane >= DN_HEADS, jnp.where(lane < 2 * DN_HEADS, 1, 0), 0)
        g = jnp.where(valid, jnp.where(is_alpha == 1, aneg_ref[...] * sp, 0.0), 0.0)
        pc = pos % chunk
        s = 1
        while s < chunk:
            g = g + jnp.where(pc >= s, pltpu.roll(g, s, 0), 0.0)
            s *= 2
        for scr, t in ((gam_scr, g), (beta_scr, beta)):
            for piece in range(3):
                tb16 = t.astype(BF16)
                scr[piece] = tb16
                t = t - tb16.astype(F32)

    fresh_tile = (i * tb) % ls == 0

    def conv_silu(x_ref, halo_ref, cw_ref):
        xx_scr[0:SUBLANES, :] = jnp.where(fresh_tile, 0.0, halo_ref[...])
        xx_scr[SUBLANES:SUBLANES + tb, :] = x_ref[...]
        acc = x_ref[...] * cw_ref[DN_CONV - 1:DN_CONV, :]
        for s in range(1, DN_CONV):
            acc = acc + xx_scr[SUBLANES - s:SUBLANES - s + tb, :] * cw_ref[DN_CONV - 1 - s:DN_CONV - s, :]
        return _silu(acc)

    def put(ref, t):
        ref[...] = t.reshape(ref.shape)

    def mask(t):
        return t if valid_from == 0 else jnp.where(valid, t, 0.0)

    q = conv_silu(xq_ref, hq_ref, cq_ref)
    put(q_ref, mask(q * (lax.rsqrt(jnp.sum(q * q, axis=-1, keepdims=True) + EPS) * (DN_D ** -0.5))))
    k = conv_silu(xk_ref, hk_ref, ck_ref)
    put(k_ref, mask(k * lax.rsqrt(jnp.sum(k * k, axis=-1, keepdims=True) + EPS)))
    put(v_ref, mask(conv_silu(xv_ref, hv_ref, cv_ref)))

    sel_r = lax.broadcasted_iota(jnp.int32, (LANES, LANES), 0)

    def lane_broadcast(scr, col):
        onehot = jnp.where(sel_r == col, 1.0, 0.0).astype(BF16)
        out = jnp.dot(scr[2], onehot, preferred_element_type=F32)
        out = out + jnp.dot(scr[1], onehot, preferred_element_type=F32)
        return out + jnp.dot(scr[0], onehot, preferred_element_type=F32)

    put(b_ref, lane_broadcast(beta_scr, h))
    put(g_ref, lane_broadcast(gam_scr, h + DN_HEADS))


def delta_prep(x, xcol0, dt, dcol, conv_w, a_log, dt_bias, *, rows, ls, chunk, valid_from, tb):
    assert rows % tb == 0 and tb % chunk == 0 and tb % SUBLANES == 0
    assert (valid_from == 0 and ls % tb == 0) or (valid_from >= DN_CONV - 1 and tb % ls == 0)
    hb = tb // SUBLANES
    aneg = jnp.zeros((1, LANES), F32).at[0, DN_HEADS:2 * DN_HEADS].set(-jnp.exp(a_log.astype(F32)))
    dtb = jnp.zeros((1, LANES), F32).at[0, DN_HEADS:2 * DN_HEADS].set(dt_bias.astype(F32))

    def xspec(off):
        return pl.BlockSpec((tb, LANES), lambda i, h: (i, xcol0 + off + h))

    def hspec(off):
        return pl.BlockSpec((SUBLANES, LANES), lambda i, h: (jnp.maximum(i * hb - 1, 0), xcol0 + off + h))

    def cspec(off):
        return pl.BlockSpec((DN_CONV, LANES), lambda i, h: (0, off + h))

    ospec = pl.BlockSpec((tb // chunk, 1, chunk, LANES), lambda i, h: (i, h, 0, 0))
    out = SDS((rows // chunk, DN_HEADS, chunk, DN_D), F32)
    return pl.pallas_call(
        functools.partial(_dprep_kernel, tb=tb, ls=ls, chunk=chunk, valid_from=valid_from),
        out_shape=(out,) * 5,
        grid=(rows // tb, DN_HEADS),
        in_specs=[xspec(0), xspec(DN_HEADS), xspec(2 * DN_HEADS), hspec(0), hspec(DN_HEADS), hspec(2 * DN_HEADS),
                  pl.BlockSpec((tb, LANES), lambda i, h: (i, dcol)),
                  cspec(0), cspec(DN_HEADS), cspec(2 * DN_HEADS),
                  pl.BlockSpec((1, LANES), lambda i, h: (0, 0)), pl.BlockSpec((1, LANES), lambda i, h: (0, 0))],
        out_specs=(ospec,) * 5,
        scratch_shapes=[pltpu.VMEM((tb + SUBLANES, LANES), F32), pltpu.VMEM((3, tb, LANES), BF16),
                        pltpu.VMEM((3, tb, LANES), BF16)],
        compiler_params=_cparams(("parallel", "arbitrary")),
        name="delta_prep",
    )(x, x, x, x, x, x, dt, conv_w, conv_w, conv_w, aneg, dtb)


def _drec_kernel(*refs, chunk, heads, nsteps, zero_init):
    if zero_init:
        q_ref, k_ref, v_ref, g_ref, b_ref, z_ref, nw_ref, y_ref, sf_ref, s_scr = refs
        s0_ref = None
    else:
        q_ref, k_ref, v_ref, g_ref, b_ref, z_ref, s0_ref, nw_ref, y_ref, sf_ref, s_scr = refs
    n = pl.program_id(2)

    @pl.when(n == 0)
    def _():
        s_scr[...] = jnp.zeros(s_scr.shape, F32) if zero_init else s0_ref[0]

    ri = lax.broadcasted_iota(jnp.int32, (chunk, chunk), 0)
    ci = lax.broadcasted_iota(jnp.int32, (chunk, chunk), 1)
    incl = ri >= ci
    strict = ri > ci
    eye = jnp.where(ri == ci, 1.0, 0.0)
    levels = int(math.log2(chunk))

    q = q_ref[0]
    k = k_ref[0]
    v = v_ref[0]
    gb = g_ref[0]
    bb = b_ref[0]
    s = s_scr[...]
    eg = jnp.exp(gb)
    glast = gb[:, chunk - 1:chunk, :]
    ek = jnp.exp(glast - gb)
    tot = jnp.exp(glast)
    gc = gb[:, :, :chunk]
    bc = bb[:, :, :chunk]
    diff = gc - jnp.swapaxes(gc, 1, 2)
    decay = jnp.where(incl, jnp.exp(jnp.where(incl, diff, 0.0)), 0.0)
    kb = k.astype(BF16)
    kk = _bdot_nt(kb, kb)
    qk = _bdot_nt(q, kb)
    x = jnp.where(strict, -(bc * kk * decay), 0.0)
    p = eye + x
    xb = x.astype(BF16)
    x = _bdot(xb, xb)
    for lvl in range(1, levels):
        xb = x.astype(BF16)
        pn = p + _bdot(xb, p)
        if lvl < levels - 1:
            x = _bdot(xb, xb)
        p = pn
    rhs = jnp.concatenate([bb * eg * k, bb * v], axis=2)
    sol = _bdot(p, rhs)
    wk = sol[:, :, :DN_D]
    ub = sol[:, :, DN_D:]
    sb = s.astype(BF16)
    u = ub - _bdot(wk, sb)
    o = _bdot(eg * q, sb) + _bdot(qk * decay, u)
    s_scr[...] = tot * s + _bdot_tn(ek * k, u)
    ms = jnp.mean(o * o, axis=-1, keepdims=True)
    on = o * lax.rsqrt(ms + EPS) * nw_ref[...]
    for j in range(heads):
        sl = slice(j * DN_D, (j + 1) * DN_D)
        y_ref[:, sl] = (on[j] * _silu(z_ref[:, sl])).astype(y_ref.dtype)

    @pl.when(n == nsteps - 1)
    def _():
        sf_ref[0] = s_scr[...]


def delta_recurrence(q, k, v, g, b, z, zcol0, s0, layer, norm_w, *, nseq, ls, chunk, heads):
    rows = nseq * ls
    nsteps = ls // chunk
    hw = heads * DN_D
    ispec = pl.BlockSpec((1, heads, chunk, DN_D), lambda bi, hg, n: (bi * nsteps + n, hg, 0, 0))
    yspec = pl.BlockSpec((chunk, hw), lambda bi, hg, n: (bi * nsteps + n, hg))
    zc = zcol0 // heads
    assert zcol0 % heads == 0
    sspec = pl.BlockSpec((1, heads, DN_D, DN_D), lambda bi, hg, n: (bi, hg, 0, 0))
    in_specs = [ispec, ispec, ispec, ispec, ispec,
                pl.BlockSpec((chunk, hw), lambda bi, hg, n: (bi * nsteps + n, zc + hg))]
    args = [q, k, v, g, b, z]
    if s0 is not None:
        in_specs.append(pl.BlockSpec((None, 1, heads, DN_D, DN_D), lambda bi, hg, n: (layer, bi, hg, 0, 0)))
        args.append(s0)
    in_specs.append(pl.BlockSpec((1, DN_D), lambda bi, hg, n: (0, 0)))
    args.append(norm_w.reshape(1, DN_D))
    return pl.pallas_call(
        functools.partial(_drec_kernel, chunk=chunk, heads=heads, nsteps=nsteps, zero_init=s0 is None),
        out_shape=(SDS((rows, DN_W), BF16), SDS((nseq, DN_HEADS, DN_D, DN_D), F32)),
        grid=(nseq, DN_HEADS // heads, nsteps),
        in_specs=in_specs,
        out_specs=(yspec, sspec),
        scratch_shapes=[pltpu.VMEM((heads, DN_D, DN_D), F32)],
        compiler_params=_cparams(("parallel", "parallel", "arbitrary")),
        name="delta_recurrence",
    )(*args)


def _pool_kernel(x_ref, halo_ref, w_ref, sc_ref, y_ref, xx_scr, *, tb, ls):
    i = pl.program_id(0)
    hist = 2 * SUBLANES
    row = lax.broadcasted_iota(jnp.int32, (tb, POOL_GROUP), 0)
    pos = (i * tb + row) % ls
    xx_scr[0:hist, :] = halo_ref[...]
    xx_scr[hist:hist + tb, :] = x_ref[...]
    for gi, w in enumerate(POOL_WINDOWS):
        cs = slice(gi * POOL_GROUP, (gi + 1) * POOL_GROUP)
        x = x_ref[:, cs]
        tot = x
        for s in range(1, w):
            tap = xx_scr[hist - s:hist - s + tb, cs]
            tot = tot + jnp.where(pos >= s, tap, 0.0)
        cnt = jnp.minimum(pos + 1, w).astype(F32)
        pooled = tot / cnt - x
        y = _dot(pooled, w_ref[gi]) * sc_ref[:, cs]
        y_ref[:, cs] = y.astype(y_ref.dtype)


def pool_mixer(x, xcol, w_pool, scale, *, rows, ls, tb):
    assert rows % tb == 0 and tb % (2 * SUBLANES) == 0
    hb = tb // (2 * SUBLANES)
    return pl.pallas_call(
        functools.partial(_pool_kernel, tb=tb, ls=ls),
        out_shape=SDS((rows, POOL_W), BF16),
        grid=(rows // tb,),
        in_specs=[pl.BlockSpec((tb, POOL_W), lambda i: (i, xcol)),
                  pl.BlockSpec((2 * SUBLANES, POOL_W), lambda i: (jnp.maximum(i * hb - 1, 0), xcol)),
                  pl.BlockSpec((len(POOL_WINDOWS), POOL_GROUP, POOL_GROUP), lambda i: (0, 0, 0)),
                  pl.BlockSpec((1, POOL_W), lambda i: (0, 0))],
        out_specs=pl.BlockSpec((tb, POOL_W), lambda i: (i, 0)),
        scratch_shapes=[pltpu.VMEM((tb + 2 * SUBLANES, POOL_W), F32)],
        compiler_params=_cparams(("arbitrary",)),
        name="pool_mixer",
    )(x, x, w_pool.astype(BF16), scale.reshape(1, POOL_W).astype(F32))


def _group_rms(x, w, width):
    rows = x.shape[0]
    lane = lax.broadcasted_iota(jnp.int32, (rows, LANES), 1)
    lo = lane < SWA_HD
    outs = []
    for c in range(width // LANES):
        t = x[:, c * LANES:(c + 1) * LANES]
        t2 = t * t
        tot = jnp.sum(t2, axis=-1, keepdims=True)
        slo = jnp.sum(jnp.where(lo, t2, 0.0), axis=-1, keepdims=True)
        ms = jnp.where(lo, slo, tot - slo) * (1.0 / SWA_HD)
        outs.append(t * lax.rsqrt(ms + EPS))
    return jnp.concatenate(outs, axis=1) * w


def _swa_prep_kernel(q_ref, k_ref, v_ref, qw_ref, kw_ref, pm_ref, qo_ref, ko_ref, kx_ref, vx_ref):
    qo_ref[...] = _group_rms(q_ref[...], qw_ref[...], SWA_W).astype(qo_ref.dtype)
    kn = _group_rms(k_ref[...], kw_ref[...], SWA_KW)
    ko_ref[...] = kn
    kx_ref[...] = jnp.dot(kn.astype(BF16), pm_ref[...], preferred_element_type=F32).astype(BF16)
    vx = jnp.dot(v_ref[...].astype(BF16), pm_ref[...], preferred_element_type=F32)
    vx_ref[...] = (vx + _value_ones(vx.shape[0])).astype(BF16)


def _value_ones(rows):
    lane = lax.broadcasted_iota(jnp.int32, (rows, 4 * SWA_KW), 1) % (4 * SWA_HD)
    return jnp.where(jnp.logical_or(lane == SWA_HD, lane == 2 * SWA_HD), 1.0, 0.0)


def _two_slot_matrix():
    src = jnp.arange(SWA_KW)
    grp, d = src // SWA_HD, src % SWA_HD
    dst = jnp.arange(4 * SWA_KW)
    dgrp, slot, dd = dst // (4 * SWA_HD), (dst % (4 * SWA_HD)) // SWA_HD, dst % SWA_HD
    keep = (slot == 0) | (slot == 3)
    hit = (grp[:, None] == dgrp[None, :]) & (d[:, None] == dd[None, :]) & keep[None, :]
    return hit.astype(BF16)


def swa_prep(x, qcol, kcol, vcol, q_norm_w, k_norm_w, *, rows, tb):
    qw = (jnp.tile(q_norm_w.astype(F32), SWA_HEADS) * (SWA_HD ** -0.5)).reshape(1, SWA_W)
    kw = jnp.tile(k_norm_w.astype(F32), SWA_KV).reshape(1, SWA_KW)
    wide = pl.BlockSpec((tb, SWA_W), lambda i: (i, 0))
    return pl.pallas_call(
        _swa_prep_kernel,
        out_shape=(SDS((rows, SWA_W), BF16), SDS((rows, SWA_KW), F32), SDS((rows, SWA_W), BF16),
                   SDS((rows, SWA_W), BF16)),
        grid=(rows // tb,),
        in_specs=[pl.BlockSpec((tb, SWA_W), lambda i: (i, qcol)), pl.BlockSpec((tb, SWA_KW), lambda i: (i, kcol)),
                  pl.BlockSpec((tb, SWA_KW), lambda i: (i, vcol)),
                  pl.BlockSpec((1, SWA_W), lambda i: (0, 0)), pl.BlockSpec((1, SWA_KW), lambda i: (0, 0)),
                  pl.BlockSpec((SWA_KW, SWA_W), lambda i: (0, 0))],
        out_specs=(wide, pl.BlockSpec((tb, SWA_KW), lambda i: (i, 0)), wide, wide),
        compiler_params=_cparams(("parallel",)),
        name="swa_prep",
    )(x, x, x, qw, kw, _two_slot_matrix())


def _swa_kernel(slope_ref, sink_ref, q_ref, kp_ref, kc_ref, vp_ref, vc_ref, o_ref, *, qb, first_has_prev):
    i = pl.program_id(1)
    r = lax.broadcasted_iota(jnp.int32, (qb, WINDOW), 0)
    c = lax.broadcasted_iota(jnp.int32, (qb, WINDOW), 1)
    dist_p = WINDOW + r - c
    dist_c = r - c
    ok_p = dist_p <= WINDOW
    if not first_has_prev:
        ok_p = jnp.logical_and(ok_p, i > 0)
    ok_c = dist_c >= 0
    dpf = dist_p.astype(F32)
    dcf = dist_c.astype(F32)
    gh = SWA_HEADS // SWA_KV

    def per_head(ref, lane_block):
        return jnp.stack([ref[:, lane_block(h) * LANES:(lane_block(h) + 1) * LANES] for h in range(SWA_HEADS)], axis=0)

    slot = lambda h: (h // gh) * 2 + h % 2
    q = per_head(q_ref, lambda h: h // 2)
    kp = per_head(kp_ref, slot)
    kc = per_head(kc_ref, slot)
    vp = per_head(vp_ref, slot)
    vc = per_head(vc_ref, slot)
    slope = slope_ref[...]
    sink = sink_ref[...][:, :, :1]
    sp = lax.dot_general(q, kp, (((2,), (2,)), ((0,), (0,))), preferred_element_type=F32)
    sc = lax.dot_general(q, kc, (((2,), (2,)), ((0,), (0,))), preferred_element_type=F32)
    sp = jnp.where(ok_p, sp - slope * dpf, NEG)
    sc = jnp.where(ok_c, sc - slope * dcf, NEG)
    m = jnp.maximum(jnp.maximum(jnp.max(sp, axis=-1, keepdims=True), jnp.max(sc, axis=-1, keepdims=True)), sink)
    ep = jnp.exp(sp - m)
    ec = jnp.exp(sc - m)
    o = _bdot(ep, vp) + _bdot(ec, vc)
    lane = lax.broadcasted_iota(jnp.int32, (qb, LANES), 1)
    low = lane < SWA_HD
    for t in range(SWA_HEADS // 2):
        oe, oo = o[2 * t], o[2 * t + 1]
        den_e = oe[:, SWA_HD:SWA_HD + 1] + jnp.exp(sink[2 * t] - m[2 * t])
        den_o = oo[:, 0:1] + jnp.exp(sink[2 * t + 1] - m[2 * t + 1])
        o_ref[:, t * LANES:(t + 1) * LANES] = jnp.where(low, oe / den_e, oo / den_o).astype(o_ref.dtype)


def swa_attention(q, kx_prev, kx_cur, vx_prev, vx_cur, sinks, *, nseq, nq, qb, prev_shift, first_has_prev):
    slopes = jnp.exp2(-8.0 * jnp.arange(1, SWA_HEADS + 1, dtype=F32) / SWA_HEADS)
    rep = lambda t: jnp.broadcast_to(t.astype(F32).reshape(SWA_HEADS, 1, 1), (SWA_HEADS, 1, LANES))
    qspec = pl.BlockSpec((qb, SWA_W), lambda b, i: (b * nq + i, 0))
    pspec = pl.BlockSpec((WINDOW, SWA_W), lambda b, i: (jnp.maximum(b * nq + i - prev_shift, 0), 0))
    cspec = pl.BlockSpec((WINDOW, SWA_W), lambda b, i: (b * nq + i, 0))
    tspec = pl.BlockSpec((SWA_HEADS, 1, LANES), lambda b, i: (0, 0, 0))
    return pl.pallas_call(
        functools.partial(_swa_kernel, qb=qb, first_has_prev=first_has_prev),
        out_shape=SDS((nseq * nq * qb, SWA_W), BF16),
        grid=(nseq, nq),
        in_specs=[tspec, tspec, qspec, pspec, cspec, pspec, cspec],
        out_specs=qspec,
        compiler_params=_cparams(("parallel", "arbitrary")),
        name="swa_attention",
    )(rep(slopes), rep(sinks), q, kx_prev, kx_cur, vx_prev, vx_cur)


def _two_slot(t, ones=False):
    rows = t.shape[0]
    t = t.astype(BF16).reshape(rows, SWA_KV, SWA_HD)
    z = jnp.zeros_like(t)
    if ones:
        z = z.at[:, :, 0].set(1.0)
    return jnp.concatenate([t, z, z, t], axis=-1).reshape(rows, SWA_KV * 4 * SWA_HD)


def _merge_kernel(ya_ref, yb_ref, yc_ref, wa_ref, wb_ref, wc_ref, ga_ref, gb_ref, gc_ref, o_ref):
    a = jnp.dot(ya_ref[...], wa_ref[...], preferred_element_type=F32)
    b = jnp.dot(yb_ref[...], wb_ref[...], preferred_element_type=F32)
    c = jnp.dot(yc_ref[...], wc_ref[...], preferred_element_type=F32)
    o = _sigmoid(ga_ref[...]) * a + _sigmoid(gb_ref[...]) * b + _sigmoid(gc_ref[...]) * c
    o_ref[...] = o.astype(o_ref.dtype)


def merge_branches(ya, yb, yc, wa, wb, wc, layer, proj, gcol0, d_model, *, bm, bn):
    m = ya.shape[0]
    gb0 = gcol0 // bn
    gstep = d_model // bn
    assert gcol0 % bn == 0 and d_model % bn == 0 and m % bm == 0

    def gspec(t):
        return pl.BlockSpec((bm, bn), lambda i, j: (i, gb0 + t * gstep + j))

    def wspec(kdim):
        return pl.BlockSpec((None, kdim, bn), lambda i, j: (layer, 0, j))

    return pl.pallas_call(
        _merge_kernel,
        out_shape=SDS((m, d_model), BF16),
        grid=(m // bm, d_model // bn),
        in_specs=[pl.BlockSpec((bm, DN_W), lambda i, j: (i, 0)), pl.BlockSpec((bm, POOL_W), lambda i, j: (i, 0)),
                  pl.BlockSpec((bm, SWA_W), lambda i, j: (i, 0)),
                  wspec(DN_W), wspec(POOL_W), wspec(SWA_W),
                  gspec(0), gspec(1), gspec(2)],
        out_specs=pl.BlockSpec((bm, bn), lambda i, j: (i, j)),
        compiler_params=_cparams(("parallel", "parallel")),
        name="merge_branches",
    )(ya, yb, yc, wa, wb, wc, proj, proj, proj)


def _ffn_act_kernel(g_ref, halo_ref, v_ref, cw_ref, cb_ref, o_ref, xx_scr, *, tb, ls):
    i = pl.program_id(0)
    bc = g_ref.shape[1]
    row = lax.broadcasted_iota(jnp.int32, (tb, bc), 0)
    pos = (i * tb + row) % ls
    xx_scr[0:SUBLANES, :] = halo_ref[...]
    xx_scr[SUBLANES:SUBLANES + tb, :] = g_ref[...]
    acc = g_ref[...] * cw_ref[FFN_CONV - 1:FFN_CONV, :]
    for s in range(1, FFN_CONV):
        tap = xx_scr[SUBLANES - s:SUBLANES - s + tb, :]
        acc = acc + jnp.where(pos >= s, tap, 0.0) * cw_ref[FFN_CONV - 1 - s:FFN_CONV - s, :]
    o_ref[...] = (_silu(acc + cb_ref[...]) * v_ref[...]).astype(o_ref.dtype)


def ffn_activation(gate, gcol0, val, vcol0, conv_w, conv_b, d_ff, *, rows, ls, tb, bc):
    assert rows % tb == 0 and d_ff % bc == 0 and gcol0 % bc == 0 and vcol0 % bc == 0
    hb = tb // SUBLANES
    g0 = gcol0 // bc
    v0 = vcol0 // bc
    return pl.pallas_call(
        functools.partial(_ffn_act_kernel, tb=tb, ls=ls),
        out_shape=SDS((rows, d_ff), BF16),
        grid=(rows // tb, d_ff // bc),
        in_specs=[pl.BlockSpec((tb, bc), lambda i, j: (i, g0 + j)),
                  pl.BlockSpec((SUBLANES, bc), lambda i, j: (jnp.maximum(i * hb - 1, 0), g0 + j)),
                  pl.BlockSpec((tb, bc), lambda i, j: (i, v0 + j)),
                  pl.BlockSpec((FFN_CONV, bc), lambda i, j: (0, j)),
                  pl.BlockSpec((1, bc), lambda i, j: (0, j))],
        out_specs=pl.BlockSpec((tb, bc), lambda i, j: (i, j)),
        scratch_shapes=[pltpu.VMEM((tb + SUBLANES, bc), F32)],
        compiler_params=_cparams(("parallel", "parallel")),
        name="ffn_activation",
    )(gate, gate, val, conv_w.astype(F32), conv_b.reshape(1, d_ff).astype(F32))


IN_PAD = 512


def _in_layout(d_model):
    off = {}
    c = 0
    for name, width in (("dq", DN_W), ("dk", DN_W), ("dv", DN_W), ("dz", DN_W),
                        ("pin", POOL_W), ("sq", SWA_W), ("sk", SWA_KW), ("sv", SWA_KW),
                        ("ga", d_model), ("gb", d_model), ("gc", d_model), ("ba", LANES)):
        off[name] = c
        c += width
    return off, -(-c // IN_PAD) * IN_PAD


def _ext(hist, new, ls):
    nseq, hr, c = hist.shape
    ln = new.shape[1]
    z = jnp.zeros((nseq, ls - hr - ln, c), new.dtype)
    return jnp.concatenate([z, hist.astype(new.dtype), new], axis=1).reshape(nseq * ls, c)


def _layer(x, nseq, seqlen, states, layer, wts, cfg):
    (n1, w_in, dconv_w, a_log, dt_bias, dn_norm_w, pool_w, pool_scale, q_norm_w, k_norm_w, sinks,
     w_ba, w_bb, w_bc, w_out, n2, w_up, fconv_w, fconv_b, w_down, state_delta) = wts
    rows, d_model = x.shape
    d_ff = w_down.shape[1]
    off = cfg["off"]
    fresh = states is None
    bm = _pick(seqlen if fresh else rows, cfg["bm"])
    bm_big = _pick(seqlen if fresh else rows, cfg["bm_big"])

    h = rmsnorm_bf16(x, n1)
    proj = matmul(h, w_in, layer, bm=bm_big, bn=cfg["bn_in"], name="in_proj")
    pb = proj
    offb = off
    proj3 = proj.reshape(nseq, seqlen, -1)

    def cols_a(name, width):
        return proj3[:, :, off[name]:off[name] + width]

    cols_b = cols_a

    if fresh:
        ls, chunk, vfrom = seqlen, cfg["chunk"], 0
        tbp = _pick(seqlen, cfg["tb_prep"], chunk)
        q, k, v, g, b = delta_prep(proj, 0, proj, off["ba"] // LANES, dconv_w, a_log, dt_bias,
                                   rows=rows, ls=ls, chunk=chunk, valid_from=0, tb=tbp)
        ya, s_new = delta_recurrence(q, k, v, g, b, proj, off["dz"] // LANES, None, layer, dn_norm_w,
                                     nseq=nseq, ls=ls, chunk=chunk, heads=cfg["heads"])
        conv_new = cols_a("dq", 3 * DN_W)[:, seqlen - (DN_CONV - 1):]
    else:
        conv_h, pool_h, k_h, v_h, ffn_h = states
        ls = SAMPLE_DELTA_ROWS
        vfrom = ls - seqlen
        xe = _ext(conv_h, cols_a("dq", 3 * DN_W), ls)
        de = _ext(jnp.zeros((nseq, 0, LANES), F32), cols_a("ba", LANES), ls)
        ze = _ext(jnp.zeros((nseq, 0, DN_W), F32), cols_a("dz", DN_W), ls)
        erows = nseq * ls
        q, k, v, g, b = delta_prep(xe, 0, de, 0, dconv_w, a_log, dt_bias,
                                   rows=erows, ls=ls, chunk=ls, valid_from=vfrom, tb=_pick(erows, 1024, ls))
        ya_e, s_new = delta_recurrence(q, k, v, g, b, ze, 0, state_delta, layer, dn_norm_w,
                                       nseq=nseq, ls=ls, chunk=ls, heads=cfg["heads"])
        ya = ya_e.reshape(nseq, ls, DN_W)[:, vfrom:].reshape(rows, DN_W)
        conv_new = xe.reshape(nseq, ls, 3 * DN_W)[:, ls - (DN_CONV - 1):]

    if fresh:
        yb = pool_mixer(pb, offb["pin"] // POOL_W, pool_w, pool_scale, rows=rows, ls=seqlen,
                        tb=_pick(seqlen, 512, 2 * SUBLANES))
        pool_new = cols_b("pin", POOL_W)[:, seqlen - POOL_HIST:]
    else:
        pe = _ext(pool_h, cols_b("pin", POOL_W), SAMPLE_SEQ_ROWS)
        erows = nseq * SAMPLE_SEQ_ROWS
        yb_e = pool_mixer(pe, 0, pool_w, pool_scale, rows=erows, ls=SAMPLE_SEQ_ROWS, tb=_pick(erows, 512, SAMPLE_SEQ_ROWS))
        yb = yb_e.reshape(nseq, SAMPLE_SEQ_ROWS, POOL_W)[:, SAMPLE_SEQ_ROWS - seqlen:].reshape(rows, POOL_W)
        pool_new = pe.reshape(nseq, SAMPLE_SEQ_ROWS, POOL_W)[:, SAMPLE_SEQ_ROWS - POOL_HIST:]

    qn, kn, kx, vx = swa_prep(pb, offb["sq"] // SWA_W, offb["sk"] // SWA_KW, offb["sv"] // SWA_KW,
                              q_norm_w, k_norm_w, rows=rows, tb=_pick(rows, 512, SUBLANES))
    vraw = cols_b("sv", SWA_KW)
    kn3 = kn.reshape(nseq, seqlen, SWA_KW)
    if fresh:
        yc = swa_attention(qn, kx, kx, vx, vx, sinks, nseq=nseq, nq=seqlen // WINDOW, qb=WINDOW,
                           prev_shift=1, first_has_prev=False)
        keep = min(WINDOW, seqlen)
        k_new = kn3[:, seqlen - keep:].reshape(nseq, keep, SWA_KV, SWA_HD)
        v_new = vraw[:, seqlen - keep:].reshape(nseq, keep, SWA_KV, SWA_HD)
    else:
        wb = k_h.shape[1]
        qpad = jnp.zeros((nseq, SUBLANES - seqlen, SWA_W), BF16)
        qe = jnp.concatenate([qn.reshape(nseq, seqlen, SWA_W), qpad], axis=1).reshape(nseq * SUBLANES, SWA_W)
        kpad = jnp.zeros((nseq, WINDOW - seqlen, SWA_W), BF16)
        kc = jnp.concatenate([kx.reshape(nseq, seqlen, SWA_W), kpad], axis=1).reshape(nseq * WINDOW, SWA_W)
        vc = jnp.concatenate([vx.reshape(nseq, seqlen, SWA_W), kpad], axis=1).reshape(nseq * WINDOW, SWA_W)
        kp = _two_slot(k_h.reshape(nseq * wb, SWA_KW))
        vp = _two_slot(v_h.reshape(nseq * wb, SWA_KW), ones=True)
        yc_e = swa_attention(qe, kp, kc, vp, vc, sinks, nseq=nseq, nq=1, qb=SUBLANES,
                             prev_shift=0, first_has_prev=True)
        yc = yc_e.reshape(nseq, SUBLANES, SWA_W)[:, :seqlen].reshape(rows, SWA_W)
        k_new = jnp.concatenate([k_h, kn3.reshape(nseq, seqlen, SWA_KV, SWA_HD).astype(k_h.dtype)], axis=1)[:, -wb:]
        v_new = jnp.concatenate([v_h, vraw.reshape(nseq, seqlen, SWA_KV, SWA_HD).astype(v_h.dtype)], axis=1)[:, -wb:]

    merged = merge_branches(ya, yb, yc, w_ba, w_bb, w_bc, layer, pb, offb["ga"], d_model, bm=bm, bn=cfg["bn_merge"])
    x = matmul(merged, w_out, layer, bm=bm, bn=cfg["bn_out"], residual=x, name="out_proj")

    h2 = rmsnorm_bf16(x, n2)
    bc = cfg["bc_ffn"]
    if fresh:
        hmid, tail = up_ffn_fused(h2, w_up, layer, fconv_w, fconv_b, d_ff, bm=bm_big, bc=bc, seqlen=seqlen)
        tps = seqlen // bm_big
        tail4 = tail.reshape(nseq, tps, SUBLANES, d_ff)
        ffn_new = tail4[:, tps - 1, SUBLANES - (FFN_CONV - 1):]
    else:
        up = matmul(h2, w_up, layer, bm=bm, bn=cfg["bn_up"], name="up_proj")
        up3 = up.reshape(nseq, seqlen, 2 * d_ff)
        ls = SUBLANES
        ge = _ext(ffn_h, up3[:, :, :d_ff], ls)
        ve = _ext(jnp.zeros((nseq, 0, d_ff), F32), up3[:, :, d_ff:], ls)
        hm_e = ffn_activation(ge, 0, ve, 0, fconv_w, fconv_b, d_ff, rows=nseq * ls, ls=ls, tb=nseq * ls, bc=bc)
        hmid = hm_e.reshape(nseq, ls, d_ff)[:, ls - seqlen:].reshape(rows, d_ff)
        ffn_new = ge.reshape(nseq, ls, d_ff)[:, ls - (FFN_CONV - 1):]
    x = matmul(hmid, w_down, layer, bm=min(bm, cfg["bm_down"]), bn=cfg["bn_down"], residual=x, name="down_proj")
    return x, (s_new, conv_new, pool_new, k_new, v_new, ffn_new)


def _config(d_model, d_ff):
    off, n_pad = _in_layout(d_model)
    assert off["ga"] % IN_PAD == 0 and d_model % IN_PAD == 0
    return {
        "off": off,
        "bm": 1024,
        "bm_big": 2048,
        "bn_in": IN_PAD,
        "bn_merge": IN_PAD,
        "bn_out": _pick(d_model, 512, LANES),
        "bn_up": _pick(2 * d_ff, 512, LANES),
        "bm_down": 512,
        "bn_down": _pick(d_model, 512, LANES),
        "bc_ffn": _pick(d_ff, 256, LANES),
        "chunk": DELTA_CHUNK,
        "tb_prep": 2048,
        "heads": REC_HEADS,
    }


def kernel(x_prompt, x_sample, state_delta, state_dconv, state_pool, cache_swa_k, cache_swa_v, state_ffn_conv, norm1_w, w_in, dconv_w, dn_a_log, dn_dt_bias, dn_norm_w, pool_w, pool_scale, q_norm_w, k_norm_w, sinks, w_branch_a, w_branch_b, w_branch_c, w_out, norm2_w, w_up, ffn_conv_w, ffn_conv_b, w_down):
    depth = w_in.shape[0]
    bp, lp, d_model = x_prompt.shape
    bs, lsm, _ = x_sample.shape
    d_ff = w_down.shape[1]
    cfg = _config(d_model, d_ff)
    xp = x_prompt.reshape(bp * lp, d_model)
    xs = x_sample.reshape(bs * lsm, d_model)
    w_in_bf = prep_w_in(w_in, d_model)
    w_ba_bf, w_bb_bf, w_bc_bf, w_out_bf, w_up_bf, w_down_bf = (
        cast_bf16(w) for w in (w_branch_a, w_branch_b, w_branch_c, w_out, w_up, w_down))
    outs_p = [[] for _ in range(6)]
    outs_s = [[] for _ in range(6)]
    for l in range(depth):
        wts = (norm1_w[l], w_in_bf, dconv_w[l].astype(F32), dn_a_log[l], dn_dt_bias[l],
               dn_norm_w[l].astype(F32), pool_w[l], pool_scale[l], q_norm_w[l], k_norm_w[l], sinks[l],
               w_ba_bf, w_bb_bf, w_bc_bf, w_out_bf, norm2_w[l], w_up_bf, ffn_conv_w[l], ffn_conv_b[l],
               w_down_bf, state_delta.astype(F32))
        xp, st_p = _layer(xp, bp, lp, None, l, wts, cfg)
        st_in = (state_dconv[l], state_pool[l], cache_swa_k[l], cache_swa_v[l], state_ffn_conv[l])
        xs, st_s = _layer(xs, bs, lsm, st_in, l, wts, cfg)
        for lst, a in zip(outs_p, st_p):
            lst.append(a)
        for lst, a in zip(outs_s, st_s):
            lst.append(a)
    dt = x_prompt.dtype
    res = [xp.reshape(bp, lp, d_model), xs.reshape(bs, lsm, d_model)]
    for outs in (outs_p, outs_s):
        res.append(jnp.stack(outs[0]).astype(state_delta.dtype))
        for t in outs[1:]:
            res.append(jnp.stack(t).astype(dt))
    return tuple(res)
```

```python
import functools
import math

import jax
import jax.numpy as jnp
from jax import lax
from jax.experimental import pallas as pl
from jax.experimental.pallas import tpu as pltpu

F32 = jnp.float32
BF16 = jnp.bfloat16
EPS = 1e-6
NEG = -1e30

LANES = 128
SUBLANES = 8
VMEM_LIMIT = 56 * 1024 * 1024

DN_HEADS = 16
DN_D = 128
DN_W = DN_HEADS * DN_D
DN_CONV = 4
POOL_WINDOWS = (2, 4, 8, 16)
POOL_GROUP = 256
POOL_W = 1024
POOL_HIST = 15
SWA_HEADS = 16
SWA_KV = 4
SWA_HD = 64
SWA_W = SWA_HEADS * SWA_HD
SWA_KW = SWA_KV * SWA_HD
WINDOW = 128
FFN_CONV = 3

DELTA_CHUNK = 128
SAMPLE_SEQ_ROWS = 32
SAMPLE_DELTA_ROWS = 8
REC_HEADS = 16

SDS = jax.ShapeDtypeStruct


def _cparams(sem, vmem=VMEM_LIMIT):
    return pltpu.CompilerParams(dimension_semantics=sem, vmem_limit_bytes=vmem)


def _pick(n, target, mult=16):
    best = None
    for d in range(mult, min(n, target) + 1, mult):
        if n % d == 0:
            best = d
    assert best is not None, (n, target, mult)
    return best


def _sigmoid(x):
    return 1.0 / (1.0 + jnp.exp(-x))


def _silu(x):
    return x * _sigmoid(x)


def _dot(a, b):
    return jnp.dot(a.astype(BF16), b.astype(BF16), preferred_element_type=F32)


def _bdot(a, b):
    return lax.dot_general(a.astype(BF16), b.astype(BF16), (((2,), (1,)), ((0,), (0,))),
                           preferred_element_type=F32)


def _bdot_nt(a, b):
    return lax.dot_general(a.astype(BF16), b.astype(BF16), (((2,), (2,)), ((0,), (0,))),
                           preferred_element_type=F32)


def _bdot_tn(a, b):
    return lax.dot_general(a.astype(BF16), b.astype(BF16), (((1,), (1,)), ((0,), (0,))),
                           preferred_element_type=F32)


def _rmsnorm_kernel(x_ref, w_ref, o_ref):
    x = x_ref[...]
    ms = jnp.mean(x * x, axis=-1, keepdims=True)
    o_ref[...] = (x * lax.rsqrt(ms + EPS) * w_ref[...]).astype(o_ref.dtype)


def rmsnorm_bf16(x, w):
    rows, d = x.shape
    br = _pick(rows, 512, SUBLANES)
    return pl.pallas_call(
        _rmsnorm_kernel,
        out_shape=SDS((rows, d), BF16),
        grid=(rows // br,),
        in_specs=[pl.BlockSpec((br, d), lambda i: (i, 0)), pl.BlockSpec((1, d), lambda i: (0, 0))],
        out_specs=pl.BlockSpec((br, d), lambda i: (i, 0)),
        compiler_params=_cparams(("parallel",)),
        name="rmsnorm",
    )(x, w.reshape(1, d))


def _mm_kernel(*refs, nk, has_res):
    if has_res:
        a_ref, w_ref, r_ref, o_ref = refs
    else:
        a_ref, w_ref, o_ref = refs
        r_ref = None
    acc = jnp.dot(a_ref[...], w_ref[...], preferred_element_type=F32)
    if nk == 1:
        if has_res:
            acc = acc + r_ref[...]
        o_ref[...] = acc.astype(o_ref.dtype)
    else:
        k = pl.program_id(2)

        @pl.when(k == 0)
        def _():
            o_ref[...] = (acc + r_ref[...]) if has_res else acc

        @pl.when(k > 0)
        def _():
            o_ref[...] += acc


def matmul(a, w, layer, *, bm, bn, bk=None, residual=None, out_dtype=F32, name="matmul"):
    m, kdim = a.shape
    n = w.shape[2]
    bk = kdim if bk is None else bk
    nk = kdim // bk
    assert m % bm == 0 and n % bn == 0 and kdim % bk == 0
    assert nk == 1 or out_dtype == F32
    in_specs = [pl.BlockSpec((bm, bk), lambda i, j, k: (i, k)),
                pl.BlockSpec((None, bk, bn), lambda i, j, k: (layer, k, j))]
    args = [a, w]
    if residual is not None:
        in_specs.append(pl.BlockSpec((bm, bn), lambda i, j, k: (i, j)))
        args.append(residual)
    return pl.pallas_call(
        functools.partial(_mm_kernel, nk=nk, has_res=residual is not None),
        out_shape=SDS((m, n), out_dtype),
        grid=(m // bm, n // bn, nk),
        in_specs=in_specs,
        out_specs=pl.BlockSpec((bm, bn), lambda i, j, k: (i, j)),
        compiler_params=_cparams(("parallel", "parallel", "arbitrary")),
        name=name,
    )(*args)


def _cast_kernel(x_ref, o_ref):
    o_ref[...] = x_ref[...].astype(o_ref.dtype)


CAST_BLOCK_ELEMS = 2 * 1024 * 1024


def cast_bf16(w):
    depth, kdim, n = w.shape
    bc = _pick(n, 8192, LANES)
    br = _pick(kdim, max(SUBLANES, CAST_BLOCK_ELEMS // bc), SUBLANES)
    spec = pl.BlockSpec((1, br, bc), lambda l, i, j: (l, i, j))
    return pl.pallas_call(
        _cast_kernel,
        out_shape=SDS(w.shape, BF16),
        grid=(depth, kdim // br, n // bc),
        in_specs=[spec],
        out_specs=spec,
        compiler_params=_cparams(("parallel", "parallel", "parallel")),
        name="cast_bf16",
    )(w)


W_IN_BLOCK = 256


def _w_in_kernel(w1_ref, w2_ref, o_ref, *, n_a, n_b, shift):
    j = pl.program_id(1)

    def emit(rows):
        o_ref[...] = rows.T.astype(o_ref.dtype)

    @pl.when(jnp.logical_or(j < n_a, j == n_a + n_b))
    def _():
        emit(w1_ref[...])

    @pl.when(jnp.logical_and(j >= n_a, j < n_a + n_b))
    def _():
        emit(jnp.concatenate([w1_ref[shift:, :], w2_ref[...]], axis=0))

    @pl.when(j > n_a + n_b)
    def _():
        o_ref[...] = jnp.zeros(o_ref.shape, o_ref.dtype)


def prep_w_in(w_in, d_model):
    depth, kdim, n_raw = w_in.shape
    offs, n_pad = _in_layout(d_model)
    blk = W_IN_BLOCK
    shift = 2 * DN_HEADS
    n_a = 4 * DN_W // blk
    n_b = (offs["ba"] - 4 * DN_W) // blk
    assert 4 * DN_W % blk == 0 and (offs["ba"] - 4 * DN_W) % blk == 0 and n_pad % blk == 0
    assert n_raw == 4 * DN_W + shift + n_b * blk and blk % shift == 0
    wt = jnp.swapaxes(w_in, 1, 2)

    def src(j):
        return jnp.where(j < n_a + n_b, j, n_a)

    return pl.pallas_call(
        functools.partial(_w_in_kernel, n_a=n_a, n_b=n_b, shift=shift),
        out_shape=SDS((depth, kdim, n_pad), BF16),
        grid=(depth, n_pad // blk),
        in_specs=[pl.BlockSpec((None, blk, kdim), lambda l, j: (l, src(j), 0)),
                  pl.BlockSpec((None, shift, kdim),
                               lambda l, j: (l, jnp.minimum(src(j) + 1, n_a + n_b) * (blk // shift), 0))],
        out_specs=pl.BlockSpec((None, kdim, blk), lambda l, j: (l, 0, j)),
        compiler_params=_cparams(("parallel", "parallel")),
        name="prep_w_in",
    )(wt, wt)


def _upffn_kernel(a_ref, wg_ref, wv_ref, cw_ref, cb_ref, o_ref, tail_ref, g_scr, carry_scr, *, bm, tiles_per_seq):
    i = pl.program_id(0)
    j = pl.program_id(1)
    a = a_ref[...]
    gate = jnp.dot(a, wg_ref[...], preferred_element_type=F32)
    val = jnp.dot(a, wv_ref[...], preferred_element_type=F32)

    @pl.when(i % tiles_per_seq == 0)
    def _():
        g_scr[0:SUBLANES, :] = jnp.zeros((SUBLANES, g_scr.shape[1]), F32)

    @pl.when(i % tiles_per_seq != 0)
    def _():
        g_scr[0:SUBLANES, :] = carry_scr[j]

    g_scr[SUBLANES:SUBLANES + bm, :] = gate
    conv = gate * cw_ref[FFN_CONV - 1:FFN_CONV, :]
    for s in range(1, FFN_CONV):
        conv = conv + g_scr[SUBLANES - s:SUBLANES - s + bm, :] * cw_ref[FFN_CONV - 1 - s:FFN_CONV - s, :]
    o_ref[...] = (_silu(conv + cb_ref[...]) * val).astype(o_ref.dtype)
    tail = gate[bm - SUBLANES:bm, :]
    tail_ref[...] = tail
    carry_scr[j] = tail


def up_ffn_fused(a, w_up, layer, conv_w, conv_b, d_ff, *, bm, bc, seqlen):
    m, kdim = a.shape
    assert m % bm == 0 and seqlen % bm == 0 and d_ff % bc == 0
    nb = d_ff // bc
    return pl.pallas_call(
        functools.partial(_upffn_kernel, bm=bm, tiles_per_seq=seqlen // bm),
        out_shape=(SDS((m, d_ff), BF16), SDS((m // bm * SUBLANES, d_ff), F32)),
        grid=(m // bm, nb),
        in_specs=[pl.BlockSpec((bm, kdim), lambda i, j: (i, 0)),
                  pl.BlockSpec((None, kdim, bc), lambda i, j: (layer, 0, j)),
                  pl.BlockSpec((None, kdim, bc), lambda i, j: (layer, 0, nb + j)),
                  pl.BlockSpec((FFN_CONV, bc), lambda i, j: (0, j)),
                  pl.BlockSpec((1, bc), lambda i, j: (0, j))],
        out_specs=(pl.BlockSpec((bm, bc), lambda i, j: (i, j)), pl.BlockSpec((SUBLANES, bc), lambda i, j: (i, j))),
        scratch_shapes=[pltpu.VMEM((bm + SUBLANES, bc), F32), pltpu.VMEM((nb, SUBLANES, bc), F32)],
        compiler_params=_cparams(("arbitrary", "arbitrary")),
        name="up_ffn",
    )(a, w_up, w_up, conv_w.astype(F32), conv_b.reshape(1, d_ff).astype(F32))


def _dprep_kernel(xq_ref, xk_ref, xv_ref, hq_ref, hk_ref, hv_ref, d_ref, cq_ref, ck_ref, cv_ref, aneg_ref, dtb_ref,
                  q_ref, k_ref, v_ref, g_ref, b_ref, xx_scr, gam_scr, beta_scr, *, tb, ls, chunk, valid_from):
    i = pl.program_id(0)
    h = pl.program_id(1)
    row = lax.broadcasted_iota(jnp.int32, (tb, LANES), 0)
    lane = lax.broadcasted_iota(jnp.int32, (tb, LANES), 1)
    pos = (i * tb + row) % ls
    valid = pos >= valid_from

    @pl.when(h == 0)
    def _():
        d = d_ref[...]
        beta = jnp.where(valid, jnp.where(lane < DN_HEADS, _sigmoid(d), 0.0), 0.0)
        sp = d + dtb_ref[...]
        sp = jnp.maximum(sp, 0.0) + jnp.log(1.0 + jnp.exp(-jnp.abs(sp)))
        is_alpha = jnp.where(lane >= DN_HEADS, jnp.where(lane < 2 * DN_HEADS, 1, 0), 0)
        g = jnp.where(valid, jnp.where(is_alpha == 1, aneg_ref[...] * sp, 0.0), 0.0)
        pc = pos % chunk
        s = 1
        while s < chunk:
            g = g + jnp.where(pc >= s, pltpu.roll(g, s, 0), 0.0)
            s *= 2
        for scr, t in ((gam_scr, g), (beta_scr, beta)):
            for piece in range(3):
                tb16 = t.astype(BF16)
                scr[piece] = tb16
                t = t - tb16.astype(F32)

    fresh_tile = (i * tb) % ls == 0

    def conv_silu(x_ref, halo_ref, cw_ref):
        xx_scr[0:SUBLANES, :] = jnp.where(fresh_tile, 0.0, halo_ref[...])
        xx_scr[SUBLANES:SUBLANES + tb, :] = x_ref[...]
        acc = x_ref[...] * cw_ref[DN_CONV - 1:DN_CONV, :]
        for s in range(1, DN_CONV):
            acc = acc + xx_scr[SUBLANES - s:SUBLANES - s + tb, :] * cw_ref[DN_CONV - 1 - s:DN_CONV - s, :]
        return _silu(acc)

    def put(ref, t):
        ref[...] = t.reshape(ref.shape)

    def mask(t):
        return t if valid_from == 0 else jnp.where(valid, t, 0.0)

    q = conv_silu(xq_ref, hq_ref, cq_ref)
    put(q_ref, mask(q * (lax.rsqrt(jnp.sum(q * q, axis=-1, keepdims=True) + EPS) * (DN_D ** -0.5))))
    k = conv_silu(xk_ref, hk_ref, ck_ref)
    put(k_ref, mask(k * lax.rsqrt(jnp.sum(k * k, axis=-1, keepdims=True) + EPS)))
    put(v_ref, mask(conv_silu(xv_ref, hv_ref, cv_ref)))

    sel_r = lax.broadcasted_iota(jnp.int32, (LANES, LANES), 0)

    def lane_broadcast(scr, col):
        onehot = jnp.where(sel_r == col, 1.0, 0.0).astype(BF16)
        out = jnp.dot(scr[2], onehot, preferred_element_type=F32)
        out = out + jnp.dot(scr[1], onehot, preferred_element_type=F32)
        return out + jnp.dot(scr[0], onehot, preferred_element_type=F32)

    put(b_ref, lane_broadcast(beta_scr, h))
    put(g_ref, lane_broadcast(gam_scr, h + DN_HEADS))


def delta_prep(x, xcol0, dt, dcol, conv_w, a_log, dt_bias, *, rows, ls, chunk, valid_from, tb):
    assert rows % tb == 0 and tb % chunk == 0 and tb % SUBLANES == 0
    assert (valid_from == 0 and ls % tb == 0) or (valid_from >= DN_CONV - 1 and tb % ls == 0)
    hb = tb // SUBLANES
    aneg = jnp.zeros((1, LANES), F32).at[0, DN_HEADS:2 * DN_HEADS].set(-jnp.exp(a_log.astype(F32)))
    dtb = jnp.zeros((1, LANES), F32).at[0, DN_HEADS:2 * DN_HEADS].set(dt_bias.astype(F32))

    def xspec(off):
        return pl.BlockSpec((tb, LANES), lambda i, h: (i, xcol0 + off + h))

    def hspec(off):
        return pl.BlockSpec((SUBLANES, LANES), lambda i, h: (jnp.maximum(i * hb - 1, 0), xcol0 + off + h))

    def cspec(off):
        return pl.BlockSpec((DN_CONV, LANES), lambda i, h: (0, off + h))

    ospec = pl.BlockSpec((tb // chunk, 1, chunk, LANES), lambda i, h: (i, h, 0, 0))
    out = SDS((rows // chunk, DN_HEADS, chunk, DN_D), F32)
    return pl.pallas_call(
        functools.partial(_dprep_kernel, tb=tb, ls=ls, chunk=chunk, valid_from=valid_from),
        out_shape=(out,) * 5,
        grid=(rows // tb, DN_HEADS),
        in_specs=[xspec(0), xspec(DN_HEADS), xspec(2 * DN_HEADS), hspec(0), hspec(DN_HEADS), hspec(2 * DN_HEADS),
                  pl.BlockSpec((tb, LANES), lambda i, h: (i, dcol)),
                  cspec(0), cspec(DN_HEADS), cspec(2 * DN_HEADS),
                  pl.BlockSpec((1, LANES), lambda i, h: (0, 0)), pl.BlockSpec((1, LANES), lambda i, h: (0, 0))],
        out_specs=(ospec,) * 5,
        scratch_shapes=[pltpu.VMEM((tb + SUBLANES, LANES), F32), pltpu.VMEM((3, tb, LANES), BF16),
                        pltpu.VMEM((3, tb, LANES), BF16)],
        compiler_params=_cparams(("parallel", "arbitrary")),
        name="delta_prep",
    )(x, x, x, x, x, x, dt, conv_w, conv_w, conv_w, aneg, dtb)


def _drec_kernel(*refs, chunk, heads, nsteps, zero_init):
    if zero_init:
        q_ref, k_ref, v_ref, g_ref, b_ref, z_ref, nw_ref, y_ref, sf_ref, s_scr = refs
        s0_ref = None
    else:
        q_ref, k_ref, v_ref, g_ref, b_ref, z_ref, s0_ref, nw_ref, y_ref, sf_ref, s_scr = refs
    n = pl.program_id(2)

    @pl.when(n == 0)
    def _():
        s_scr[...] = jnp.zeros(s_scr.shape, F32) if zero_init else s0_ref[0]

    ri = lax.broadcasted_iota(jnp.int32, (chunk, chunk), 0)
    ci = lax.broadcasted_iota(jnp.int32, (chunk, chunk), 1)
    incl = ri >= ci
    strict = ri > ci
    eye = jnp.where(ri == ci, 1.0, 0.0)
    levels = int(math.log2(chunk))

    q = q_ref[0]
    k = k_ref[0]
    v = v_ref[0]
    gb = g_ref[0]
    bb = b_ref[0]
    s = s_scr[...]
    eg = jnp.exp(gb)
    glast = gb[:, chunk - 1:chunk, :]
    ek = jnp.exp(glast - gb)
    tot = jnp.exp(glast)
    gc = gb[:, :, :chunk]
    bc = bb[:, :, :chunk]
    diff = gc - jnp.swapaxes(gc, 1, 2)
    decay = jnp.where(incl, jnp.exp(jnp.where(incl, diff, 0.0)), 0.0)
    kb = k.astype(BF16)
    qkk = _bdot_nt(jnp.concatenate([q.astype(BF16), kb], axis=1), kb)
    qk = qkk[:, :chunk, :]
    kk = qkk[:, chunk:, :]
    x = jnp.where(strict, -(bc * kk * decay), 0.0)
    p = eye + x
    xb = x.astype(BF16)
    x = _bdot(xb, xb)
    for lvl in range(1, levels):
        xb = x.astype(BF16)
        pn = p + _bdot(xb, p)
        if lvl < levels - 1:
            x = _bdot(xb, xb)
        p = pn
    rhs = jnp.concatenate([bb * eg * k, bb * v], axis=2)
    sol = _bdot(p, rhs)
    wk = sol[:, :, :DN_D]
    ub = sol[:, :, DN_D:]
    sb = s.astype(BF16)
    ws = _bdot(jnp.concatenate([wk, eg * q], axis=1), sb)
    u = ub - ws[:, :chunk, :]
    o = ws[:, chunk:, :] + _bdot(qk * decay, u)
    s_scr[...] = tot * s + _bdot_tn(ek * k, u)
    ms = jnp.mean(o * o, axis=-1, keepdims=True)
    on = o * lax.rsqrt(ms + EPS) * nw_ref[...]
    for j in range(heads):
        sl = slice(j * DN_D, (j + 1) * DN_D)
        y_ref[:, sl] = (on[j] * _silu(z_ref[:, sl])).astype(y_ref.dtype)

    @pl.when(n == nsteps - 1)
    def _():
        sf_ref[0] = s_scr[...]


def delta_recurrence(q, k, v, g, b, z, zcol0, s0, layer, norm_w, *, nseq, ls, chunk, heads):
    rows = nseq * ls
    nsteps = ls // chunk
    hw = heads * DN_D
    ispec = pl.BlockSpec((1, heads, chunk, DN_D), lambda bi, hg, n: (bi * nsteps + n, hg, 0, 0))
    yspec = pl.BlockSpec((chunk, hw), lambda bi, hg, n: (bi * nsteps + n, hg))
    zc = zcol0 // heads
    assert zcol0 % heads == 0
    sspec = pl.BlockSpec((1, heads, DN_D, DN_D), lambda bi, hg, n: (bi, hg, 0, 0))
    in_specs = [ispec, ispec, ispec, ispec, ispec,
                pl.BlockSpec((chunk, hw), lambda bi, hg, n: (bi * nsteps + n, zc + hg))]
    args = [q, k, v, g, b, z]
    if s0 is not None:
        in_specs.append(pl.BlockSpec((None, 1, heads, DN_D, DN_D), lambda bi, hg, n: (layer, bi, hg, 0, 0)))
        args.append(s0)
    in_specs.append(pl.BlockSpec((1, DN_D), lambda bi, hg, n: (0, 0)))
    args.append(norm_w.reshape(1, DN_D))
    return pl.pallas_call(
        functools.partial(_drec_kernel, chunk=chunk, heads=heads, nsteps=nsteps, zero_init=s0 is None),
        out_shape=(SDS((rows, DN_W), BF16), SDS((nseq, DN_HEADS, DN_D, DN_D), F32)),
        grid=(nseq, DN_HEADS // heads, nsteps),
        in_specs=in_specs,
        out_specs=(yspec, sspec),
        scratch_shapes=[pltpu.VMEM((heads, DN_D, DN_D), F32)],
        compiler_params=_cparams(("parallel", "parallel", "arbitrary")),
        name="delta_recurrence",
    )(*args)


def _pool_kernel(x_ref, halo_ref, w_ref, sc_ref, y_ref, xx_scr, *, tb, ls):
    i = pl.program_id(0)
    hist = 2 * SUBLANES
    row = lax.broadcasted_iota(jnp.int32, (tb, POOL_GROUP), 0)
    pos = (i * tb + row) % ls
    xx_scr[0:hist, :] = halo_ref[...]
    xx_scr[hist:hist + tb, :] = x_ref[...]
    for gi, w in enumerate(POOL_WINDOWS):
        cs = slice(gi * POOL_GROUP, (gi + 1) * POOL_GROUP)
        x = x_ref[:, cs]
        tot = x
        for s in range(1, w):
            tap = xx_scr[hist - s:hist - s + tb, cs]
            tot = tot + jnp.where(pos >= s, tap, 0.0)
        cnt = jnp.minimum(pos + 1, w).astype(F32)
        pooled = tot / cnt - x
        y = _dot(pooled, w_ref[gi]) * sc_ref[:, cs]
        y_ref[:, cs] = y.astype(y_ref.dtype)


def pool_mixer(x, xcol, w_pool, scale, *, rows, ls, tb):
    assert rows % tb == 0 and tb % (2 * SUBLANES) == 0
    hb = tb // (2 * SUBLANES)
    return pl.pallas_call(
        functools.partial(_pool_kernel, tb=tb, ls=ls),
        out_shape=SDS((rows, POOL_W), BF16),
        grid=(rows // tb,),
        in_specs=[pl.BlockSpec((tb, POOL_W), lambda i: (i, xcol)),
                  pl.BlockSpec((2 * SUBLANES, POOL_W), lambda i: (jnp.maximum(i * hb - 1, 0), xcol)),
                  pl.BlockSpec((len(POOL_WINDOWS), POOL_GROUP, POOL_GROUP), lambda i: (0, 0, 0)),
                  pl.BlockSpec((1, POOL_W), lambda i: (0, 0))],
        out_specs=pl.BlockSpec((tb, POOL_W), lambda i: (i, 0)),
        scratch_shapes=[pltpu.VMEM((tb + 2 * SUBLANES, POOL_W), F32)],
        compiler_params=_cparams(("arbitrary",)),
        name="pool_mixer",
    )(x, x, w_pool.astype(BF16), scale.reshape(1, POOL_W).astype(F32))


def _group_rms(x, w, width):
    rows = x.shape[0]
    lane = lax.broadcasted_iota(jnp.int32, (rows, LANES), 1)
    lo = lane < SWA_HD
    outs = []
    for c in range(width // LANES):
        t = x[:, c * LANES:(c + 1) * LANES]
        t2 = t * t
        tot = jnp.sum(t2, axis=-1, keepdims=True)
        slo = jnp.sum(jnp.where(lo, t2, 0.0), axis=-1, keepdims=True)
        ms = jnp.where(lo, slo, tot - slo) * (1.0 / SWA_HD)
        outs.append(t * lax.rsqrt(ms + EPS))
    return jnp.concatenate(outs, axis=1) * w


def _swa_prep_kernel(q_ref, k_ref, v_ref, qw_ref, kw_ref, pm_ref, qo_ref, ko_ref, kx_ref, vx_ref):
    qo_ref[...] = _group_rms(q_ref[...], qw_ref[...], SWA_W).astype(qo_ref.dtype)
    kn = _group_rms(k_ref[...], kw_ref[...], SWA_KW)
    ko_ref[...] = kn
    kx_ref[...] = jnp.dot(kn.astype(BF16), pm_ref[...], preferred_element_type=F32).astype(BF16)
    vx = jnp.dot(v_ref[...].astype(BF16), pm_ref[...], preferred_element_type=F32)
    vx_ref[...] = (vx + _value_ones(vx.shape[0])).astype(BF16)


def _value_ones(rows):
    lane = lax.broadcasted_iota(jnp.int32, (rows, 4 * SWA_KW), 1) % (4 * SWA_HD)
    return jnp.where(jnp.logical_or(lane == SWA_HD, lane == 2 * SWA_HD), 1.0, 0.0)


def _two_slot_matrix():
    src = jnp.arange(SWA_KW)
    grp, d = src // SWA_HD, src % SWA_HD
    dst = jnp.arange(4 * SWA_KW)
    dgrp, slot, dd = dst // (4 * SWA_HD), (dst % (4 * SWA_HD)) // SWA_HD, dst % SWA_HD
    keep = (slot == 0) | (slot == 3)
    hit = (grp[:, None] == dgrp[None, :]) & (d[:, None] == dd[None, :]) & keep[None, :]
    return hit.astype(BF16)


def swa_prep(x, qcol, kcol, vcol, q_norm_w, k_norm_w, *, rows, tb):
    qw = (jnp.tile(q_norm_w.astype(F32), SWA_HEADS) * (SWA_HD ** -0.5)).reshape(1, SWA_W)
    kw = jnp.tile(k_norm_w.astype(F32), SWA_KV).reshape(1, SWA_KW)
    wide = pl.BlockSpec((tb, SWA_W), lambda i: (i, 0))
    return pl.pallas_call(
        _swa_prep_kernel,
        out_shape=(SDS((rows, SWA_W), BF16), SDS((rows, SWA_KW), F32), SDS((rows, SWA_W), BF16),
                   SDS((rows, SWA_W), BF16)),
        grid=(rows // tb,),
        in_specs=[pl.BlockSpec((tb, SWA_W), lambda i: (i, qcol)), pl.BlockSpec((tb, SWA_KW), lambda i: (i, kcol)),
                  pl.BlockSpec((tb, SWA_KW), lambda i: (i, vcol)),
                  pl.BlockSpec((1, SWA_W), lambda i: (0, 0)), pl.BlockSpec((1, SWA_KW), lambda i: (0, 0)),
                  pl.BlockSpec((SWA_KW, SWA_W), lambda i: (0, 0))],
        out_specs=(wide, pl.BlockSpec((tb, SWA_KW), lambda i: (i, 0)), wide, wide),
        compiler_params=_cparams(("parallel",)),
        name="swa_prep",
    )(x, x, x, qw, kw, _two_slot_matrix())


def _swa_kernel(slope_ref, sink_ref, q_ref, kp_ref, kc_ref, vp_ref, vc_ref, o_ref, *, qb, first_has_prev):
    i = pl.program_id(1)
    r = lax.broadcasted_iota(jnp.int32, (qb, WINDOW), 0)
    c = lax.broadcasted_iota(jnp.int32, (qb, WINDOW), 1)
    dist_p = WINDOW + r - c
    dist_c = r - c
    ok_p = dist_p <= WINDOW
    if not first_has_prev:
        ok_p = jnp.logical_and(ok_p, i > 0)
    ok_c = dist_c >= 0
    dpf = dist_p.astype(F32)
    dcf = dist_c.astype(F32)
    gh = SWA_HEADS // SWA_KV

    def per_head(ref, lane_block):
        return jnp.stack([ref[:, lane_block(h) * LANES:(lane_block(h) + 1) * LANES] for h in range(SWA_HEADS)], axis=0)

    slot = lambda h: (h // gh) * 2 + h % 2
    q = per_head(q_ref, lambda h: h // 2)
    kp = per_head(kp_ref, slot)
    kc = per_head(kc_ref, slot)
    vp = per_head(vp_ref, slot)
    vc = per_head(vc_ref, slot)
    slope = slope_ref[...]
    sink = sink_ref[...][:, :, :1]
    sp = lax.dot_general(q, kp, (((2,), (2,)), ((0,), (0,))), preferred_element_type=F32)
    sc = lax.dot_general(q, kc, (((2,), (2,)), ((0,), (0,))), preferred_element_type=F32)
    sp = jnp.where(ok_p, sp - slope * dpf, NEG)
    sc = jnp.where(ok_c, sc - slope * dcf, NEG)
    m = jnp.maximum(jnp.maximum(jnp.max(sp, axis=-1, keepdims=True), jnp.max(sc, axis=-1, keepdims=True)), sink)
    ep = jnp.exp(sp - m)
    ec = jnp.exp(sc - m)
    o = _bdot(ep, vp) + _bdot(ec, vc)
    lane = lax.broadcasted_iota(jnp.int32, (qb, LANES), 1)
    low = lane < SWA_HD
    for t in range(SWA_HEADS // 2):
        oe, oo = o[2 * t], o[2 * t + 1]
        den_e = oe[:, SWA_HD:SWA_HD + 1] + jnp.exp(sink[2 * t] - m[2 * t])
        den_o = oo[:, 0:1] + jnp.exp(sink[2 * t + 1] - m[2 * t + 1])
        o_ref[:, t * LANES:(t + 1) * LANES] = jnp.where(low, oe / den_e, oo / den_o).astype(o_ref.dtype)


def swa_attention(q, kx_prev, kx_cur, vx_prev, vx_cur, sinks, *, nseq, nq, qb, prev_shift, first_has_prev):
    slopes = jnp.exp2(-8.0 * jnp.arange(1, SWA_HEADS + 1, dtype=F32) / SWA_HEADS)
    rep = lambda t: jnp.broadcast_to(t.astype(F32).reshape(SWA_HEADS, 1, 1), (SWA_HEADS, 1, LANES))
    qspec = pl.BlockSpec((qb, SWA_W), lambda b, i: (b * nq + i, 0))
    pspec = pl.BlockSpec((WINDOW, SWA_W), lambda b, i: (jnp.maximum(b * nq + i - prev_shift, 0), 0))
    cspec = pl.BlockSpec((WINDOW, SWA_W), lambda b, i: (b * nq + i, 0))
    tspec = pl.BlockSpec((SWA_HEADS, 1, LANES), lambda b, i: (0, 0, 0))
    return pl.pallas_call(
        functools.partial(_swa_kernel, qb=qb, first_has_prev=first_has_prev),
        out_shape=SDS((nseq * nq * qb, SWA_W), BF16),
        grid=(nseq, nq),
        in_specs=[tspec, tspec, qspec, pspec, cspec, pspec, cspec],
        out_specs=qspec,
        compiler_params=_cparams(("parallel", "arbitrary")),
        name="swa_attention",
    )(rep(slopes), rep(sinks), q, kx_prev, kx_cur, vx_prev, vx_cur)


def _two_slot(t, ones=False):
    rows = t.shape[0]
    t = t.astype(BF16).reshape(rows, SWA_KV, SWA_HD)
    z = jnp.zeros_like(t)
    if ones:
        z = z.at[:, :, 0].set(1.0)
    return jnp.concatenate([t, z, z, t], axis=-1).reshape(rows, SWA_KV * 4 * SWA_HD)


def _merge_kernel(ya_ref, yb_ref, yc_ref, wa_ref, wb_ref, wc_ref, ga_ref, gb_ref, gc_ref, o_ref):
    a = jnp.dot(ya_ref[...], wa_ref[...], preferred_element_type=F32)
    b = jnp.dot(yb_ref[...], wb_ref[...], preferred_element_type=F32)
    c = jnp.dot(yc_ref[...], wc_ref[...], preferred_element_type=F32)
    o = _sigmoid(ga_ref[...]) * a + _sigmoid(gb_ref[...]) * b + _sigmoid(gc_ref[...]) * c
    o_ref[...] = o.astype(o_ref.dtype)


def merge_branches(ya, yb, yc, wa, wb, wc, layer, proj, gcol0, d_model, *, bm, bn):
    m = ya.shape[0]
    gb0 = gcol0 // bn
    gstep = d_model // bn
    assert gcol0 % bn == 0 and d_model % bn == 0 and m % bm == 0

    def gspec(t):
        return pl.BlockSpec((bm, bn), lambda i, j: (i, gb0 + t * gstep + j))

    def wspec(kdim):
        return pl.BlockSpec((None, kdim, bn), lambda i, j: (layer, 0, j))

    return pl.pallas_call(
        _merge_kernel,
        out_shape=SDS((m, d_model), BF16),
        grid=(m // bm, d_model // bn),
        in_specs=[pl.BlockSpec((bm, DN_W), lambda i, j: (i, 0)), pl.BlockSpec((bm, POOL_W), lambda i, j: (i, 0)),
                  pl.BlockSpec((bm, SWA_W), lambda i, j: (i, 0)),
                  wspec(DN_W), wspec(POOL_W), wspec(SWA_W),
                  gspec(0), gspec(1), gspec(2)],
        out_specs=pl.BlockSpec((bm, bn), lambda i, j: (i, j)),
        compiler_params=_cparams(("parallel", "parallel")),
        name="merge_branches",
    )(ya, yb, yc, wa, wb, wc, proj, proj, proj)


def _ffn_act_kernel(g_ref, halo_ref, v_ref, cw_ref, cb_ref, o_ref, xx_scr, *, tb, ls):
    i = pl.program_id(0)
    bc = g_ref.shape[1]
    row = lax.broadcasted_iota(jnp.int32, (tb, bc), 0)
    pos = (i * tb + row) % ls
    xx_scr[0:SUBLANES, :] = halo_ref[...]
    xx_scr[SUBLANES:SUBLANES + tb, :] = g_ref[...]
    acc = g_ref[...] * cw_ref[FFN_CONV - 1:FFN_CONV, :]
    for s in range(1, FFN_CONV):
        tap = xx_scr[SUBLANES - s:SUBLANES - s + tb, :]
        acc = acc + jnp.where(pos >= s, tap, 0.0) * cw_ref[FFN_CONV - 1 - s:FFN_CONV - s, :]
    o_ref[...] = (_silu(acc + cb_ref[...]) * v_ref[...]).astype(o_ref.dtype)


def ffn_activation(gate, gcol0, val, vcol0, conv_w, conv_b, d_ff, *, rows, ls, tb, bc):
    assert rows % tb == 0 and d_ff % bc == 0 and gcol0 % bc == 0 and vcol0 % bc == 0
    hb = tb // SUBLANES
    g0 = gcol0 // bc
    v0 = vcol0 // bc
    return pl.pallas_call(
        functools.partial(_ffn_act_kernel, tb=tb, ls=ls),
        out_shape=SDS((rows, d_ff), BF16),
        grid=(rows // tb, d_ff // bc),
        in_specs=[pl.BlockSpec((tb, bc), lambda i, j: (i, g0 + j)),
                  pl.BlockSpec((SUBLANES, bc), lambda i, j: (jnp.maximum(i * hb - 1, 0), g0 + j)),
                  pl.BlockSpec((tb, bc), lambda i, j: (i, v0 + j)),
                  pl.BlockSpec((FFN_CONV, bc), lambda i, j: (0, j)),
                  pl.BlockSpec((1, bc), lambda i, j: (0, j))],
        out_specs=pl.BlockSpec((tb, bc), lambda i, j: (i, j)),
        scratch_shapes=[pltpu.VMEM((tb + SUBLANES, bc), F32)],
        compiler_params=_cparams(("parallel", "parallel")),
        name="ffn_activation",
    )(gate, gate, val, conv_w.astype(F32), conv_b.reshape(1, d_ff).astype(F32))


IN_PAD = 512


def _in_layout(d_model):
    off = {}
    c = 0
    for name, width in (("dq", DN_W), ("dk", DN_W), ("dv", DN_W), ("dz", DN_W),
                        ("pin", POOL_W), ("sq", SWA_W), ("sk", SWA_KW), ("sv", SWA_KW),
                        ("ga", d_model), ("gb", d_model), ("gc", d_model), ("ba", LANES)):
        off[name] = c
        c += width
    return off, -(-c // IN_PAD) * IN_PAD


def _ext(hist, new, ls):
    nseq, hr, c = hist.shape
    ln = new.shape[1]
    z = jnp.zeros((nseq, ls - hr - ln, c), new.dtype)
    return jnp.concatenate([z, hist.astype(new.dtype), new], axis=1).reshape(nseq * ls, c)


def _layer(x, nseq, seqlen, states, layer, wts, cfg):
    (n1, w_in, dconv_w, a_log, dt_bias, dn_norm_w, pool_w, pool_scale, q_norm_w, k_norm_w, sinks,
     w_ba, w_bb, w_bc, w_out, n2, w_up, fconv_w, fconv_b, w_down, state_delta) = wts
    rows, d_model = x.shape
    d_ff = w_down.shape[1]
    off = cfg["off"]
    fresh = states is None
    bm = _pick(seqlen if fresh else rows, cfg["bm"])
    bm_big = _pick(seqlen if fresh else rows, cfg["bm_big"])

    h = rmsnorm_bf16(x, n1)
    proj = matmul(h, w_in, layer, bm=bm_big, bn=cfg["bn_in"], name="in_proj")
    pb = proj
    offb = off
    proj3 = proj.reshape(nseq, seqlen, -1)

    def cols_a(name, width):
        return proj3[:, :, off[name]:off[name] + width]

    cols_b = cols_a

    if fresh:
        ls, chunk, vfrom = seqlen, cfg["chunk"], 0
        tbp = _pick(seqlen, cfg["tb_prep"], chunk)
        q, k, v, g, b = delta_prep(proj, 0, proj, off["ba"] // LANES, dconv_w, a_log, dt_bias,
                                   rows=rows, ls=ls, chunk=chunk, valid_from=0, tb=tbp)
        ya, s_new = delta_recurrence(q, k, v, g, b, proj, off["dz"] // LANES, None, layer, dn_norm_w,
                                     nseq=nseq, ls=ls, chunk=chunk, heads=cfg["heads"])
        conv_new = cols_a("dq", 3 * DN_W)[:, seqlen - (DN_CONV - 1):]
    else:
        conv_h, pool_h, k_h, v_h, ffn_h = states
        ls = SAMPLE_DELTA_ROWS
        vfrom = ls - seqlen
        xe = _ext(conv_h, cols_a("dq", 3 * DN_W), ls)
        de = _ext(jnp.zeros((nseq, 0, LANES), F32), cols_a("ba", LANES), ls)
        ze = _ext(jnp.zeros((nseq, 0, DN_W), F32), cols_a("dz", DN_W), ls)
        erows = nseq * ls
        q, k, v, g, b = delta_prep(xe, 0, de, 0, dconv_w, a_log, dt_bias,
                                   rows=erows, ls=ls, chunk=ls, valid_from=vfrom, tb=_pick(erows, 1024, ls))
        ya_e, s_new = delta_recurrence(q, k, v, g, b, ze, 0, state_delta, layer, dn_norm_w,
                                       nseq=nseq, ls=ls, chunk=ls, heads=cfg["heads"])
        ya = ya_e.reshape(nseq, ls, DN_W)[:, vfrom:].reshape(rows, DN_W)
        conv_new = xe.reshape(nseq, ls, 3 * DN_W)[:, ls - (DN_CONV - 1):]

    if fresh:
        yb = pool_mixer(pb, offb["pin"] // POOL_W, pool_w, pool_scale, rows=rows, ls=seqlen,
                        tb=_pick(seqlen, 512, 2 * SUBLANES))
        pool_new = cols_b("pin", POOL_W)[:, seqlen - POOL_HIST:]
    else:
        pe = _ext(pool_h, cols_b("pin", POOL_W), SAMPLE_SEQ_ROWS)
        erows = nseq * SAMPLE_SEQ_ROWS
        yb_e = pool_mixer(pe, 0, pool_w, pool_scale, rows=erows, ls=SAMPLE_SEQ_ROWS, tb=_pick(erows, 512, SAMPLE_SEQ_ROWS))
        yb = yb_e.reshape(nseq, SAMPLE_SEQ_ROWS, POOL_W)[:, SAMPLE_SEQ_ROWS - seqlen:].reshape(rows, POOL_W)
        pool_new = pe.reshape(nseq, SAMPLE_SEQ_ROWS, POOL_W)[:, SAMPLE_SEQ_ROWS - POOL_HIST:]

    qn, kn, kx, vx = swa_prep(pb, offb["sq"] // SWA_W, offb["sk"] // SWA_KW, offb["sv"] // SWA_KW,
                              q_norm_w, k_norm_w, rows=rows, tb=_pick(rows, 512, SUBLANES))
    vraw = cols_b("sv", SWA_KW)
    kn3 = kn.reshape(nseq, seqlen, SWA_KW)
    if fresh:
        yc = swa_attention(qn, kx, kx, vx, vx, sinks, nseq=nseq, nq=seqlen // WINDOW, qb=WINDOW,
                           prev_shift=1, first_has_prev=False)
        keep = min(WINDOW, seqlen)
        k_new = kn3[:, seqlen - keep:].reshape(nseq, keep, SWA_KV, SWA_HD)
        v_new = vraw[:, seqlen - keep:].reshape(nseq, keep, SWA_KV, SWA_HD)
    else:
        wb = k_h.shape[1]
        qpad = jnp.zeros((nseq, SUBLANES - seqlen, SWA_W), BF16)
        qe = jnp.concatenate([qn.reshape(nseq, seqlen, SWA_W), qpad], axis=1).reshape(nseq * SUBLANES, SWA_W)
        kpad = jnp.zeros((nseq, WINDOW - seqlen, SWA_W), BF16)
        kc = jnp.concatenate([kx.reshape(nseq, seqlen, SWA_W), kpad], axis=1).reshape(nseq * WINDOW, SWA_W)
        vc = jnp.concatenate([vx.reshape(nseq, seqlen, SWA_W), kpad], axis=1).reshape(nseq * WINDOW, SWA_W)
        kp = _two_slot(k_h.reshape(nseq * wb, SWA_KW))
        vp = _two_slot(v_h.reshape(nseq * wb, SWA_KW), ones=True)
        yc_e = swa_attention(qe, kp, kc, vp, vc, sinks, nseq=nseq, nq=1, qb=SUBLANES,
                             prev_shift=0, first_has_prev=True)
        yc = yc_e.reshape(nseq, SUBLANES, SWA_W)[:, :seqlen].reshape(rows, SWA_W)
        k_new = jnp.concatenate([k_h, kn3.reshape(nseq, seqlen, SWA_KV, SWA_HD).astype(k_h.dtype)], axis=1)[:, -wb:]
        v_new = jnp.concatenate([v_h, vraw.reshape(nseq, seqlen, SWA_KV, SWA_HD).astype(v_h.dtype)], axis=1)[:, -wb:]

    merged = merge_branches(ya, yb, yc, w_ba, w_bb, w_bc, layer, pb, offb["ga"], d_model, bm=bm, bn=cfg["bn_merge"])
    x = matmul(merged, w_out, layer, bm=bm, bn=cfg["bn_out"], residual=x, name="out_proj")

    h2 = rmsnorm_bf16(x, n2)
    bc = cfg["bc_ffn"]
    if fresh:
        hmid, tail = up_ffn_fused(h2, w_up, layer, fconv_w, fconv_b, d_ff, bm=bm_big, bc=bc, seqlen=seqlen)
        tps = seqlen // bm_big
        tail4 = tail.reshape(nseq, tps, SUBLANES, d_ff)
        ffn_new = tail4[:, tps - 1, SUBLANES - (FFN_CONV - 1):]
    else:
        up = matmul(h2, w_up, layer, bm=bm, bn=cfg["bn_up"], name="up_proj")
        up3 = up.reshape(nseq, seqlen, 2 * d_ff)
        ls = SUBLANES
        ge = _ext(ffn_h, up3[:, :, :d_ff], ls)
        ve = _ext(jnp.zeros((nseq, 0, d_ff), F32), up3[:, :, d_ff:], ls)
        hm_e = ffn_activation(ge, 0, ve, 0, fconv_w, fconv_b, d_ff, rows=nseq * ls, ls=ls, tb=nseq * ls, bc=bc)
        hmid = hm_e.reshape(nseq, ls, d_ff)[:, ls - seqlen:].reshape(rows, d_ff)
        ffn_new = ge.reshape(nseq, ls, d_ff)[:, ls - (FFN_CONV - 1):]
    x = matmul(hmid, w_down, layer, bm=min(bm, cfg["bm_down"]), bn=cfg["bn_down"], residual=x, name="down_proj")
    return x, (s_new, conv_new, pool_new, k_new, v_new, ffn_new)


def _config(d_model, d_ff):
    off, n_pad = _in_layout(d_model)
    assert off["ga"] % IN_PAD == 0 and d_model % IN_PAD == 0
    return {
        "off": off,
        "bm": 1024,
        "bm_big": 2048,
        "bn_in": IN_PAD,
        "bn_merge": IN_PAD,
        "bn_out": _pick(d_model, 512, LANES),
        "bn_up": _pick(2 * d_ff, 512, LANES),
        "bm_down": 512,
        "bn_down": _pick(d_model, 512, LANES),
        "bc_ffn": _pick(d_ff, 256, LANES),
        "chunk": DELTA_CHUNK,
        "tb_prep": 2048,
        "heads": REC_HEADS,
    }


def kernel(x_prompt, x_sample, state_delta, state_dconv, state_pool, cache_swa_k, cache_swa_v, state_ffn_conv, norm1_w, w_in, dconv_w, dn_a_log, dn_dt_bias, dn_norm_w, pool_w, pool_scale, q_norm_w, k_norm_w, sinks, w_branch_a, w_branch_b, w_branch_c, w_out, norm2_w, w_up, ffn_conv_w, ffn_conv_b, w_down):
    depth = w_in.shape[0]
    bp, lp, d_model = x_prompt.shape
    bs, lsm, _ = x_sample.shape
    d_ff = w_down.shape[1]
    cfg = _config(d_model, d_ff)
    xp = x_prompt.reshape(bp * lp, d_model)
    xs = x_sample.reshape(bs * lsm, d_model)
    w_in_bf = prep_w_in(w_in, d_model)
    w_ba_bf, w_bb_bf, w_bc_bf, w_out_bf, w_up_bf, w_down_bf = (
        cast_bf16(w) for w in (w_branch_a, w_branch_b, w_branch_c, w_out, w_up, w_down))
    outs_p = [[] for _ in range(6)]
    outs_s = [[] for _ in range(6)]
    for l in range(depth):
        wts = (norm1_w[l], w_in_bf, dconv_w[l].astype(F32), dn_a_log[l], dn_dt_bias[l],
               dn_norm_w[l].astype(F32), pool_w[l], pool_scale[l], q_norm_w[l], k_norm_w[l], sinks[l],
               w_ba_bf, w_bb_bf, w_bc_bf, w_out_bf, norm2_w[l], w_up_bf, ffn_conv_w[l], ffn_conv_b[l],
               w_down_bf, state_delta.astype(F32))
        xp, st_p = _layer(xp, bp, lp, None, l, wts, cfg)
        st_in = (state_dconv[l], state_pool[l], cache_swa_k[l], cache_swa_v[l], state_ffn_conv[l])
        xs, st_s = _layer(xs, bs, lsm, st_in, l, wts, cfg)
        for lst, a in zip(outs_p, st_p):
            lst.append(a)
        for lst, a in zip(outs_s, st_s):
            lst.append(a)
    dt = x_prompt.dtype
    res = [xp.reshape(bp, lp, d_model), xs.reshape(bs, lsm, d_model)]
    for outs in (outs_p, outs_s):
        res.append(jnp.stack(outs[0]).astype(state_delta.dtype))
        for t in outs[1:]:
            res.append(jnp.stack(t).astype(dt))
    return tuple(res)
```

```python
import functools
import math

import jax
import jax.numpy as jnp
from jax import lax
from jax.experimental import pallas as pl
from jax.experimental.pallas import tpu as pltpu

F32 = jnp.float32
BF16 = jnp.bfloat16
EPS = 1e-6
NEG = -1e30

LANES = 128
SUBLANES = 8
VMEM_LIMIT = 56 * 1024 * 1024

DN_HEADS = 16
DN_D = 128
DN_W = DN_HEADS * DN_D
DN_CONV = 4
POOL_WINDOWS = (2, 4, 8, 16)
POOL_GROUP = 256
POOL_W = 1024
POOL_HIST = 15
SWA_HEADS = 16
SWA_KV = 4
SWA_HD = 64
SWA_W = SWA_HEADS * SWA_HD
SWA_KW = SWA_KV * SWA_HD
WINDOW = 128
FFN_CONV = 3

DELTA_CHUNK = 128
SAMPLE_SEQ_ROWS = 32
SAMPLE_DELTA_ROWS = 8
REC_HEADS = 16

SDS = jax.ShapeDtypeStruct


def _cparams(sem, vmem=VMEM_LIMIT):
    return pltpu.CompilerParams(dimension_semantics=sem, vmem_limit_bytes=vmem)


def _pick(n, target, mult=16):
    best = None
    for d in range(mult, min(n, target) + 1, mult):
        if n % d == 0:
            best = d
    assert best is not None, (n, target, mult)
    return best


def _sigmoid(x):
    return 0.5 * jnp.tanh(0.5 * x) + 0.5


def _silu(x):
    h = 0.5 * x
    return h * jnp.tanh(h) + h


def _dot(a, b):
    return jnp.dot(a.astype(BF16), b.astype(BF16), preferred_element_type=F32)


def _bdot(a, b):
    return lax.dot_general(a.astype(BF16), b.astype(BF16), (((2,), (1,)), ((0,), (0,))),
                           preferred_element_type=F32)


def _bdot_nt(a, b):
    return lax.dot_general(a.astype(BF16), b.astype(BF16), (((2,), (2,)), ((0,), (0,))),
                           preferred_element_type=F32)


def _bdot_tn(a, b):
    return lax.dot_general(a.astype(BF16), b.astype(BF16), (((1,), (1,)), ((0,), (0,))),
                           preferred_element_type=F32)


def _rmsnorm_kernel(x_ref, w_ref, o_ref):
    x = x_ref[...]
    ms = jnp.mean(x * x, axis=-1, keepdims=True)
    o_ref[...] = (x * lax.rsqrt(ms + EPS) * w_ref[...]).astype(o_ref.dtype)


def rmsnorm_bf16(x, w):
    rows, d = x.shape
    br = _pick(rows, 512, SUBLANES)
    return pl.pallas_call(
        _rmsnorm_kernel,
        out_shape=SDS((rows, d), BF16),
        grid=(rows // br,),
        in_specs=[pl.BlockSpec((br, d), lambda i: (i, 0)), pl.BlockSpec((1, d), lambda i: (0, 0))],
        out_specs=pl.BlockSpec((br, d), lambda i: (i, 0)),
        compiler_params=_cparams(("parallel",)),
        name="rmsnorm",
    )(x, w.reshape(1, d))


def _mm_kernel(*refs, nk, has_res):
    if has_res:
        a_ref, w_ref, r_ref, o_ref = refs
    else:
        a_ref, w_ref, o_ref = refs
        r_ref = None
    acc = jnp.dot(a_ref[...], w_ref[...], preferred_element_type=F32)
    if nk == 1:
        if has_res:
            acc = acc + r_ref[...]
        o_ref[...] = acc.astype(o_ref.dtype)
    else:
        k = pl.program_id(2)

        @pl.when(k == 0)
        def _():
            o_ref[...] = (acc + r_ref[...]) if has_res else acc

        @pl.when(k > 0)
        def _():
            o_ref[...] += acc


def matmul(a, w, layer, *, bm, bn, bk=None, residual=None, out_dtype=F32, name="matmul"):
    m, kdim = a.shape
    n = w.shape[2]
    bk = kdim if bk is None else bk
    nk = kdim // bk
    assert m % bm == 0 and n % bn == 0 and kdim % bk == 0
    assert nk == 1 or out_dtype == F32
    in_specs = [pl.BlockSpec((bm, bk), lambda i, j, k: (i, k)),
                pl.BlockSpec((None, bk, bn), lambda i, j, k: (layer, k, j))]
    args = [a, w]
    if residual is not None:
        in_specs.append(pl.BlockSpec((bm, bn), lambda i, j, k: (i, j)))
        args.append(residual)
    return pl.pallas_call(
        functools.partial(_mm_kernel, nk=nk, has_res=residual is not None),
        out_shape=SDS((m, n), out_dtype),
        grid=(m // bm, n // bn, nk),
        in_specs=in_specs,
        out_specs=pl.BlockSpec((bm, bn), lambda i, j, k: (i, j)),
        compiler_params=_cparams(("parallel", "parallel", "arbitrary")),
        name=name,
    )(*args)


def _cast_kernel(x_ref, o_ref):
    o_ref[...] = x_ref[...].astype(o_ref.dtype)


CAST_BLOCK_ELEMS = 2 * 1024 * 1024


def cast_bf16(w):
    depth, kdim, n = w.shape
    bc = _pick(n, 8192, LANES)
    br = _pick(kdim, max(SUBLANES, CAST_BLOCK_ELEMS // bc), SUBLANES)
    spec = pl.BlockSpec((1, br, bc), lambda l, i, j: (l, i, j))
    return pl.pallas_call(
        _cast_kernel,
        out_shape=SDS(w.shape, BF16),
        grid=(depth, kdim // br, n // bc),
        in_specs=[spec],
        out_specs=spec,
        compiler_params=_cparams(("parallel", "parallel", "parallel")),
        name="cast_bf16",
    )(w)


W_IN_BLOCK = 256


def _w_in_kernel(w1_ref, w2_ref, o_ref, *, n_a, n_b, shift):
    j = pl.program_id(1)

    def emit(rows):
        o_ref[...] = rows.T.astype(o_ref.dtype)

    @pl.when(jnp.logical_or(j < n_a, j == n_a + n_b))
    def _():
        emit(w1_ref[...])

    @pl.when(jnp.logical_and(j >= n_a, j < n_a + n_b))
    def _():
        emit(jnp.concatenate([w1_ref[shift:, :], w2_ref[...]], axis=0))

    @pl.when(j > n_a + n_b)
    def _():
        o_ref[...] = jnp.zeros(o_ref.shape, o_ref.dtype)


def prep_w_in(w_in, d_model):
    depth, kdim, n_raw = w_in.shape
    offs, n_pad = _in_layout(d_model)
    blk = W_IN_BLOCK
    shift = 2 * DN_HEADS
    n_a = 4 * DN_W // blk
    n_b = (offs["ba"] - 4 * DN_W) // blk
    assert 4 * DN_W % blk == 0 and (offs["ba"] - 4 * DN_W) % blk == 0 and n_pad % blk == 0
    assert n_raw == 4 * DN_W + shift + n_b * blk and blk % shift == 0
    wt = jnp.swapaxes(w_in, 1, 2)

    def src(j):
        return jnp.where(j < n_a + n_b, j, n_a)

    return pl.pallas_call(
        functools.partial(_w_in_kernel, n_a=n_a, n_b=n_b, shift=shift),
        out_shape=SDS((depth, kdim, n_pad), BF16),
        grid=(depth, n_pad // blk),
        in_specs=[pl.BlockSpec((None, blk, kdim), lambda l, j: (l, src(j), 0)),
                  pl.BlockSpec((None, shift, kdim),
                               lambda l, j: (l, jnp.minimum(src(j) + 1, n_a + n_b) * (blk // shift), 0))],
        out_specs=pl.BlockSpec((None, kdim, blk), lambda l, j: (l, 0, j)),
        compiler_params=_cparams(("parallel", "parallel")),
        name="prep_w_in",
    )(wt, wt)


def _upffn_kernel(a_ref, wg_ref, wv_ref, cw_ref, cb_ref, o_ref, tail_ref, g_scr, carry_scr, *, bm, tiles_per_seq):
    i = pl.program_id(0)
    j = pl.program_id(1)
    a = a_ref[...]
    gate = jnp.dot(a, wg_ref[...], preferred_element_type=F32)
    val = jnp.dot(a, wv_ref[...], preferred_element_type=F32)

    @pl.when(i % tiles_per_seq == 0)
    def _():
        g_scr[0:SUBLANES, :] = jnp.zeros((SUBLANES, g_scr.shape[1]), F32)

    @pl.when(i % tiles_per_seq != 0)
    def _():
        g_scr[0:SUBLANES, :] = carry_scr[j]

    g_scr[SUBLANES:SUBLANES + bm, :] = gate
    conv = gate * cw_ref[FFN_CONV - 1:FFN_CONV, :]
    for s in range(1, FFN_CONV):
        conv = conv + g_scr[SUBLANES - s:SUBLANES - s + bm, :] * cw_ref[FFN_CONV - 1 - s:FFN_CONV - s, :]
    o_ref[...] = (_silu(conv + cb_ref[...]) * val).astype(o_ref.dtype)
    tail = gate[bm - SUBLANES:bm, :]
    tail_ref[...] = tail
    carry_scr[j] = tail


def up_ffn_fused(a, w_up, layer, conv_w, conv_b, d_ff, *, bm, bc, seqlen):
    m, kdim = a.shape
    assert m % bm == 0 and seqlen % bm == 0 and d_ff % bc == 0
    nb = d_ff // bc
    return pl.pallas_call(
        functools.partial(_upffn_kernel, bm=bm, tiles_per_seq=seqlen // bm),
        out_shape=(SDS((m, d_ff), BF16), SDS((m // bm * SUBLANES, d_ff), F32)),
        grid=(m // bm, nb),
        in_specs=[pl.BlockSpec((bm, kdim), lambda i, j: (i, 0)),
                  pl.BlockSpec((None, kdim, bc), lambda i, j: (layer, 0, j)),
                  pl.BlockSpec((None, kdim, bc), lambda i, j: (layer, 0, nb + j)),
                  pl.BlockSpec((FFN_CONV, bc), lambda i, j: (0, j)),
                  pl.BlockSpec((1, bc), lambda i, j: (0, j))],
        out_specs=(pl.BlockSpec((bm, bc), lambda i, j: (i, j)), pl.BlockSpec((SUBLANES, bc), lambda i, j: (i, j))),
        scratch_shapes=[pltpu.VMEM((bm + SUBLANES, bc), F32), pltpu.VMEM((nb, SUBLANES, bc), F32)],
        compiler_params=_cparams(("arbitrary", "arbitrary")),
        name="up_ffn",
    )(a, w_up, w_up, conv_w.astype(F32), conv_b.reshape(1, d_ff).astype(F32))


def _dprep_kernel(xq_ref, xk_ref, xv_ref, hq_ref, hk_ref, hv_ref, d_ref, cq_ref, ck_ref, cv_ref, aneg_ref, dtb_ref,
                  q_ref, k_ref, v_ref, g_ref, b_ref, xx_scr, gam_scr, beta_scr, *, tb, ls, chunk, valid_from):
    i = pl.program_id(0)
    h = pl.program_id(1)
    row = lax.broadcasted_iota(jnp.int32, (tb, LANES), 0)
    lane = lax.broadcasted_iota(jnp.int32, (tb, LANES), 1)
    pos = (i * tb + row) % ls
    valid = pos >= valid_from

    @pl.when(h == 0)
    def _():
        d = d_ref[...]
        beta = jnp.where(valid, jnp.where(lane < DN_HEADS, _sigmoid(d), 0.0), 0.0)
        sp = d + dtb_ref[...]
        sp = jnp.maximum(sp, 0.0) + jnp.log(1.0 + jnp.exp(-jnp.abs(sp)))
        is_alpha = jnp.where(lane >= DN_HEADS, jnp.where(lane < 2 * DN_HEADS, 1, 0), 0)
        g = jnp.where(valid, jnp.where(is_alpha == 1, aneg_ref[...] * sp, 0.0), 0.0)
        pc = pos % chunk
        s = 1
        while s < chunk:
            g = g + jnp.where(pc >= s, pltpu.roll(g, s, 0), 0.0)
            s *= 2
        for scr, t in ((gam_scr, g), (beta_scr, beta)):
            for piece in range(3):
                tb16 = t.astype(BF16)
                scr[piece] = tb16
                t = t - tb16.astype(F32)

    fresh_tile = (i * tb) % ls == 0

    def conv_silu(x_ref, halo_ref, cw_ref):
        xx_scr[0:SUBLANES, :] = jnp.where(fresh_tile, 0.0, halo_ref[...])
        xx_scr[SUBLANES:SUBLANES + tb, :] = x_ref[...]
        acc = x_ref[...] * cw_ref[DN_CONV - 1:DN_CONV, :]
        for s in range(1, DN_CONV):
            acc = acc + xx_scr[SUBLANES - s:SUBLANES - s + tb, :] * cw_ref[DN_CONV - 1 - s:DN_CONV - s, :]
        return _silu(acc)

    def put(ref, t):
        ref[...] = t.reshape(ref.shape)

    def mask(t):
        return t if valid_from == 0 else jnp.where(valid, t, 0.0)

    q = conv_silu(xq_ref, hq_ref, cq_ref)
    put(q_ref, mask(q * (lax.rsqrt(jnp.sum(q * q, axis=-1, keepdims=True) + EPS) * (DN_D ** -0.5))))
    k = conv_silu(xk_ref, hk_ref, ck_ref)
    put(k_ref, mask(k * lax.rsqrt(jnp.sum(k * k, axis=-1, keepdims=True) + EPS)))
    put(v_ref, mask(conv_silu(xv_ref, hv_ref, cv_ref)))

    sel_r = lax.broadcasted_iota(jnp.int32, (LANES, LANES), 0)

    def lane_broadcast(scr, col):
        onehot = jnp.where(sel_r == col, 1.0, 0.0).astype(BF16)
        out = jnp.dot(scr[2], onehot, preferred_element_type=F32)
        out = out + jnp.dot(scr[1], onehot, preferred_element_type=F32)
        return out + jnp.dot(scr[0], onehot, preferred_element_type=F32)

    put(b_ref, lane_broadcast(beta_scr, h))
    put(g_ref, lane_broadcast(gam_scr, h + DN_HEADS))


def delta_prep(x, xcol0, dt, dcol, conv_w, a_log, dt_bias, *, rows, ls, chunk, valid_from, tb):
    assert rows % tb == 0 and tb % chunk == 0 and tb % SUBLANES == 0
    assert (valid_from == 0 and ls % tb == 0) or (valid_from >= DN_CONV - 1 and tb % ls == 0)
    hb = tb // SUBLANES
    aneg = jnp.zeros((1, LANES), F32).at[0, DN_HEADS:2 * DN_HEADS].set(-jnp.exp(a_log.astype(F32)))
    dtb = jnp.zeros((1, LANES), F32).at[0, DN_HEADS:2 * DN_HEADS].set(dt_bias.astype(F32))

    def xspec(off):
        return pl.BlockSpec((tb, LANES), lambda i, h: (i, xcol0 + off + h))

    def hspec(off):
        return pl.BlockSpec((SUBLANES, LANES), lambda i, h: (jnp.maximum(i * hb - 1, 0), xcol0 + off + h))

    def cspec(off):
        return pl.BlockSpec((DN_CONV, LANES), lambda i, h: (0, off + h))

    ospec = pl.BlockSpec((tb // chunk, 1, chunk, LANES), lambda i, h: (i, h, 0, 0))
    out = SDS((rows // chunk, DN_HEADS, chunk, DN_D), F32)
    return pl.pallas_call(
        functools.partial(_dprep_kernel, tb=tb, ls=ls, chunk=chunk, valid_from=valid_from),
        out_shape=(out,) * 5,
        grid=(rows // tb, DN_HEADS),
        in_specs=[xspec(0), xspec(DN_HEADS), xspec(2 * DN_HEADS), hspec(0), hspec(DN_HEADS), hspec(2 * DN_HEADS),
                  pl.BlockSpec((tb, LANES), lambda i, h: (i, dcol)),
                  cspec(0), cspec(DN_HEADS), cspec(2 * DN_HEADS),
                  pl.BlockSpec((1, LANES), lambda i, h: (0, 0)), pl.BlockSpec((1, LANES), lambda i, h: (0, 0))],
        out_specs=(ospec,) * 5,
        scratch_shapes=[pltpu.VMEM((tb + SUBLANES, LANES), F32), pltpu.VMEM((3, tb, LANES), BF16),
                        pltpu.VMEM((3, tb, LANES), BF16)],
        compiler_params=_cparams(("parallel", "arbitrary")),
        name="delta_prep",
    )(x, x, x, x, x, x, dt, conv_w, conv_w, conv_w, aneg, dtb)


def _drec_kernel(*refs, chunk, heads, nsteps, zero_init):
    if zero_init:
        q_ref, k_ref, v_ref, g_ref, b_ref, z_ref, nw_ref, y_ref, sf_ref, s_scr = refs
        s0_ref = None
    else:
        q_ref, k_ref, v_ref, g_ref, b_ref, z_ref, s0_ref, nw_ref, y_ref, sf_ref, s_scr = refs
    n = pl.program_id(2)

    @pl.when(n == 0)
    def _():
        s_scr[...] = jnp.zeros(s_scr.shape, F32) if zero_init else s0_ref[0]

    ri = lax.broadcasted_iota(jnp.int32, (chunk, chunk), 0)
    ci = lax.broadcasted_iota(jnp.int32, (chunk, chunk), 1)
    incl = ri >= ci
    strict = ri > ci
    eye = jnp.where(ri == ci, 1.0, 0.0)
    levels = int(math.log2(chunk))

    q = q_ref[0]
    k = k_ref[0]
    v = v_ref[0]
    gb = g_ref[0]
    bb = b_ref[0]
    s = s_scr[...]
    eg = jnp.exp(gb)
    glast = gb[:, chunk - 1:chunk, :]
    ek = jnp.exp(glast - gb)
    tot = jnp.exp(glast)
    gc = gb[:, :, :chunk]
    bc = bb[:, :, :chunk]
    diff = gc - jnp.swapaxes(gc, 1, 2)
    decay = jnp.where(incl, jnp.exp(jnp.where(incl, diff, 0.0)), 0.0)
    kb = k.astype(BF16)
    qkk = _bdot_nt(jnp.concatenate([q.astype(BF16), kb], axis=1), kb)
    qk = qkk[:, :chunk, :]
    kk = qkk[:, chunk:, :]
    x = jnp.where(strict, -(bc * kk * decay), 0.0)
    p = eye + x
    xb = x.astype(BF16)
    x = _bdot(xb, xb)
    for lvl in range(1, levels):
        xb = x.astype(BF16)
        pn = p + _bdot(xb, p)
        if lvl < levels - 1:
            x = _bdot(xb, xb)
        p = pn
    rhs = jnp.concatenate([bb * eg * k, bb * v], axis=2)
    sol = _bdot(p, rhs)
    wk = sol[:, :, :DN_D]
    ub = sol[:, :, DN_D:]
    sb = s.astype(BF16)
    ws = _bdot(jnp.concatenate([wk, eg * q], axis=1), sb)
    u = ub - ws[:, :chunk, :]
    o = ws[:, chunk:, :] + _bdot(qk * decay, u)
    s_scr[...] = tot * s + _bdot_tn(ek * k, u)
    ms = jnp.mean(o * o, axis=-1, keepdims=True)
    on = o * lax.rsqrt(ms + EPS) * nw_ref[...]
    for j in range(heads):
        sl = slice(j * DN_D, (j + 1) * DN_D)
        y_ref[:, sl] = (on[j] * _silu(z_ref[:, sl])).astype(y_ref.dtype)

    @pl.when(n == nsteps - 1)
    def _():
        sf_ref[0] = s_scr[...]


def delta_recurrence(q, k, v, g, b, z, zcol0, s0, layer, norm_w, *, nseq, ls, chunk, heads):
    rows = nseq * ls
    nsteps = ls // chunk
    hw = heads * DN_D
    ispec = pl.BlockSpec((1, heads, chunk, DN_D), lambda bi, hg, n: (bi * nsteps + n, hg, 0, 0))
    yspec = pl.BlockSpec((chunk, hw), lambda bi, hg, n: (bi * nsteps + n, hg))
    zc = zcol0 // heads
    assert zcol0 % heads == 0
    sspec = pl.BlockSpec((1, heads, DN_D, DN_D), lambda bi, hg, n: (bi, hg, 0, 0))
    in_specs = [ispec, ispec, ispec, ispec, ispec,
                pl.BlockSpec((chunk, hw), lambda bi, hg, n: (bi * nsteps + n, zc + hg))]
    args = [q, k, v, g, b, z]
    if s0 is not None:
        in_specs.append(pl.BlockSpec((None, 1, heads, DN_D, DN_D), lambda bi, hg, n: (layer, bi, hg, 0, 0)))
        args.append(s0)
    in_specs.append(pl.BlockSpec((1, DN_D), lambda bi, hg, n: (0, 0)))
    args.append(norm_w.reshape(1, DN_D))
    return pl.pallas_call(
        functools.partial(_drec_kernel, chunk=chunk, heads=heads, nsteps=nsteps, zero_init=s0 is None),
        out_shape=(SDS((rows, DN_W), BF16), SDS((nseq, DN_HEADS, DN_D, DN_D), F32)),
        grid=(nseq, DN_HEADS // heads, nsteps),
        in_specs=in_specs,
        out_specs=(yspec, sspec),
        scratch_shapes=[pltpu.VMEM((heads, DN_D, DN_D), F32)],
        compiler_params=_cparams(("parallel", "parallel", "arbitrary")),
        name="delta_recurrence",
    )(*args)


def _pool_kernel(x_ref, halo_ref, w_ref, sc_ref, y_ref, xx_scr, *, tb, ls):
    i = pl.program_id(0)
    hist = 2 * SUBLANES
    row = lax.broadcasted_iota(jnp.int32, (tb, POOL_GROUP), 0)
    pos = (i * tb + row) % ls
    xx_scr[0:hist, :] = halo_ref[...]
    xx_scr[hist:hist + tb, :] = x_ref[...]
    for gi, w in enumerate(POOL_WINDOWS):
        cs = slice(gi * POOL_GROUP, (gi + 1) * POOL_GROUP)
        x = x_ref[:, cs]
        tot = x
        for s in range(1, w):
            tap = xx_scr[hist - s:hist - s + tb, cs]
            tot = tot + jnp.where(pos >= s, tap, 0.0)
        cnt = jnp.minimum(pos + 1, w).astype(F32)
        pooled = tot / cnt - x
        y = _dot(pooled, w_ref[gi]) * sc_ref[:, cs]
        y_ref[:, cs] = y.astype(y_ref.dtype)


def pool_mixer(x, xcol, w_pool, scale, *, rows, ls, tb):
    assert rows % tb == 0 and tb % (2 * SUBLANES) == 0
    hb = tb // (2 * SUBLANES)
    return pl.pallas_call(
        functools.partial(_pool_kernel, tb=tb, ls=ls),
        out_shape=SDS((rows, POOL_W), BF16),
        grid=(rows // tb,),
        in_specs=[pl.BlockSpec((tb, POOL_W), lambda i: (i, xcol)),
                  pl.BlockSpec((2 * SUBLANES, POOL_W), lambda i: (jnp.maximum(i * hb - 1, 0), xcol)),
                  pl.BlockSpec((len(POOL_WINDOWS), POOL_GROUP, POOL_GROUP), lambda i: (0, 0, 0)),
                  pl.BlockSpec((1, POOL_W), lambda i: (0, 0))],
        out_specs=pl.BlockSpec((tb, POOL_W), lambda i: (i, 0)),
        scratch_shapes=[pltpu.VMEM((tb + 2 * SUBLANES, POOL_W), F32)],
        compiler_params=_cparams(("arbitrary",)),
        name="pool_mixer",
    )(x, x, w_pool.astype(BF16), scale.reshape(1, POOL_W).astype(F32))


def _group_rms(x, w, width):
    rows = x.shape[0]
    lane = lax.broadcasted_iota(jnp.int32, (rows, LANES), 1)
    lo = lane < SWA_HD
    outs = []
    for c in range(width // LANES):
        t = x[:, c * LANES:(c + 1) * LANES]
        t2 = t * t
        tot = jnp.sum(t2, axis=-1, keepdims=True)
        slo = jnp.sum(jnp.where(lo, t2, 0.0), axis=-1, keepdims=True)
        ms = jnp.where(lo, slo, tot - slo) * (1.0 / SWA_HD)
        outs.append(t * lax.rsqrt(ms + EPS))
    return jnp.concatenate(outs, axis=1) * w


def _swa_prep_kernel(q_ref, k_ref, v_ref, qw_ref, kw_ref, pm_ref, qo_ref, ko_ref, kx_ref, vx_ref):
    qo_ref[...] = _group_rms(q_ref[...], qw_ref[...], SWA_W).astype(qo_ref.dtype)
    kn = _group_rms(k_ref[...], kw_ref[...], SWA_KW)
    ko_ref[...] = kn
    kx_ref[...] = jnp.dot(kn.astype(BF16), pm_ref[...], preferred_element_type=F32).astype(BF16)
    vx = jnp.dot(v_ref[...].astype(BF16), pm_ref[...], preferred_element_type=F32)
    vx_ref[...] = (vx + _value_ones(vx.shape[0])).astype(BF16)


def _value_ones(rows):
    lane = lax.broadcasted_iota(jnp.int32, (rows, 4 * SWA_KW), 1) % (4 * SWA_HD)
    return jnp.where(jnp.logical_or(lane == SWA_HD, lane == 2 * SWA_HD), 1.0, 0.0)


def _two_slot_matrix():
    src = jnp.arange(SWA_KW)
    grp, d = src // SWA_HD, src % SWA_HD
    dst = jnp.arange(4 * SWA_KW)
    dgrp, slot, dd = dst // (4 * SWA_HD), (dst % (4 * SWA_HD)) // SWA_HD, dst % SWA_HD
    keep = (slot == 0) | (slot == 3)
    hit = (grp[:, None] == dgrp[None, :]) & (d[:, None] == dd[None, :]) & keep[None, :]
    return hit.astype(BF16)


def swa_prep(x, qcol, kcol, vcol, q_norm_w, k_norm_w, *, rows, tb):
    qw = (jnp.tile(q_norm_w.astype(F32), SWA_HEADS) * (SWA_HD ** -0.5)).reshape(1, SWA_W)
    kw = jnp.tile(k_norm_w.astype(F32), SWA_KV).reshape(1, SWA_KW)
    wide = pl.BlockSpec((tb, SWA_W), lambda i: (i, 0))
    return pl.pallas_call(
        _swa_prep_kernel,
        out_shape=(SDS((rows, SWA_W), BF16), SDS((rows, SWA_KW), F32), SDS((rows, SWA_W), BF16),
                   SDS((rows, SWA_W), BF16)),
        grid=(rows // tb,),
        in_specs=[pl.BlockSpec((tb, SWA_W), lambda i: (i, qcol)), pl.BlockSpec((tb, SWA_KW), lambda i: (i, kcol)),
                  pl.BlockSpec((tb, SWA_KW), lambda i: (i, vcol)),
                  pl.BlockSpec((1, SWA_W), lambda i: (0, 0)), pl.BlockSpec((1, SWA_KW), lambda i: (0, 0)),
                  pl.BlockSpec((SWA_KW, SWA_W), lambda i: (0, 0))],
        out_specs=(wide, pl.BlockSpec((tb, SWA_KW), lambda i: (i, 0)), wide, wide),
        compiler_params=_cparams(("parallel",)),
        name="swa_prep",
    )(x, x, x, qw, kw, _two_slot_matrix())


def _swa_kernel(slope_ref, sink_ref, q_ref, kp_ref, kc_ref, vp_ref, vc_ref, o_ref, *, qb, first_has_prev):
    i = pl.program_id(1)
    r = lax.broadcasted_iota(jnp.int32, (qb, WINDOW), 0)
    c = lax.broadcasted_iota(jnp.int32, (qb, WINDOW), 1)
    dist_p = WINDOW + r - c
    dist_c = r - c
    ok_p = dist_p <= WINDOW
    if not first_has_prev:
        ok_p = jnp.logical_and(ok_p, i > 0)
    ok_c = dist_c >= 0
    dpf = dist_p.astype(F32)
    dcf = dist_c.astype(F32)
    gh = SWA_HEADS // SWA_KV

    def per_head(ref, lane_block):
        return jnp.stack([ref[:, lane_block(h) * LANES:(lane_block(h) + 1) * LANES] for h in range(SWA_HEADS)], axis=0)

    slot = lambda h: (h // gh) * 2 + h % 2
    q = per_head(q_ref, lambda h: h // 2)
    kp = per_head(kp_ref, slot)
    kc = per_head(kc_ref, slot)
    vp = per_head(vp_ref, slot)
    vc = per_head(vc_ref, slot)
    slope = slope_ref[...]
    sink = sink_ref[...][:, :, :1]
    sp = lax.dot_general(q, kp, (((2,), (2,)), ((0,), (0,))), preferred_element_type=F32)
    sc = lax.dot_general(q, kc, (((2,), (2,)), ((0,), (0,))), preferred_element_type=F32)
    sp = jnp.where(ok_p, sp - slope * dpf, NEG)
    sc = jnp.where(ok_c, sc - slope * dcf, NEG)
    m = jnp.maximum(jnp.maximum(jnp.max(sp, axis=-1, keepdims=True), jnp.max(sc, axis=-1, keepdims=True)), sink)
    ep = jnp.exp(sp - m)
    ec = jnp.exp(sc - m)
    o = _bdot(ep, vp) + _bdot(ec, vc)
    lane = lax.broadcasted_iota(jnp.int32, (qb, LANES), 1)
    low = lane < SWA_HD
    for t in range(SWA_HEADS // 2):
        oe, oo = o[2 * t], o[2 * t + 1]
        den_e = oe[:, SWA_HD:SWA_HD + 1] + jnp.exp(sink[2 * t] - m[2 * t])
        den_o = oo[:, 0:1] + jnp.exp(sink[2 * t + 1] - m[2 * t + 1])
        o_ref[:, t * LANES:(t + 1) * LANES] = jnp.where(low, oe / den_e, oo / den_o).astype(o_ref.dtype)


def swa_attention(q, kx_prev, kx_cur, vx_prev, vx_cur, sinks, *, nseq, nq, qb, prev_shift, first_has_prev):
    slopes = jnp.exp2(-8.0 * jnp.arange(1, SWA_HEADS + 1, dtype=F32) / SWA_HEADS)
    rep = lambda t: jnp.broadcast_to(t.astype(F32).reshape(SWA_HEADS, 1, 1), (SWA_HEADS, 1, LANES))
    qspec = pl.BlockSpec((qb, SWA_W), lambda b, i: (b * nq + i, 0))
    pspec = pl.BlockSpec((WINDOW, SWA_W), lambda b, i: (jnp.maximum(b * nq + i - prev_shift, 0), 0))
    cspec = pl.BlockSpec((WINDOW, SWA_W), lambda b, i: (b * nq + i, 0))
    tspec = pl.BlockSpec((SWA_HEADS, 1, LANES), lambda b, i: (0, 0, 0))
    return pl.pallas_call(
        functools.partial(_swa_kernel, qb=qb, first_has_prev=first_has_prev),
        out_shape=SDS((nseq * nq * qb, SWA_W), BF16),
        grid=(nseq, nq),
        in_specs=[tspec, tspec, qspec, pspec, cspec, pspec, cspec],
        out_specs=qspec,
        compiler_params=_cparams(("parallel", "arbitrary")),
        name="swa_attention",
    )(rep(slopes), rep(sinks), q, kx_prev, kx_cur, vx_prev, vx_cur)


def _two_slot(t, ones=False):
    rows = t.shape[0]
    t = t.astype(BF16).reshape(rows, SWA_KV, SWA_HD)
    z = jnp.zeros_like(t)
    if ones:
        z = z.at[:, :, 0].set(1.0)
    return jnp.concatenate([t, z, z, t], axis=-1).reshape(rows, SWA_KV * 4 * SWA_HD)


def _merge_kernel(ya_ref, yb_ref, yc_ref, wa_ref, wb_ref, wc_ref, ga_ref, gb_ref, gc_ref, o_ref):
    a = jnp.dot(ya_ref[...], wa_ref[...], preferred_element_type=F32)
    b = jnp.dot(yb_ref[...], wb_ref[...], preferred_element_type=F32)
    c = jnp.dot(yc_ref[...], wc_ref[...], preferred_element_type=F32)
    o = _sigmoid(ga_ref[...]) * a + _sigmoid(gb_ref[...]) * b + _sigmoid(gc_ref[...]) * c
    o_ref[...] = o.astype(o_ref.dtype)


def merge_branches(ya, yb, yc, wa, wb, wc, layer, proj, gcol0, d_model, *, bm, bn):
    m = ya.shape[0]
    gb0 = gcol0 // bn
    gstep = d_model // bn
    assert gcol0 % bn == 0 and d_model % bn == 0 and m % bm == 0

    def gspec(t):
        return pl.BlockSpec((bm, bn), lambda i, j: (i, gb0 + t * gstep + j))

    def wspec(kdim):
        return pl.BlockSpec((None, kdim, bn), lambda i, j: (layer, 0, j))

    return pl.pallas_call(
        _merge_kernel,
        out_shape=SDS((m, d_model), BF16),
        grid=(m // bm, d_model // bn),
        in_specs=[pl.BlockSpec((bm, DN_W), lambda i, j: (i, 0)), pl.BlockSpec((bm, POOL_W), lambda i, j: (i, 0)),
                  pl.BlockSpec((bm, SWA_W), lambda i, j: (i, 0)),
                  wspec(DN_W), wspec(POOL_W), wspec(SWA_W),
                  gspec(0), gspec(1), gspec(2)],
        out_specs=pl.BlockSpec((bm, bn), lambda i, j: (i, j)),
        compiler_params=_cparams(("parallel", "parallel")),
        name="merge_branches",
    )(ya, yb, yc, wa, wb, wc, proj, proj, proj)


def _ffn_act_kernel(g_ref, halo_ref, v_ref, cw_ref, cb_ref, o_ref, xx_scr, *, tb, ls):
    i = pl.program_id(0)
    bc = g_ref.shape[1]
    row = lax.broadcasted_iota(jnp.int32, (tb, bc), 0)
    pos = (i * tb + row) % ls
    xx_scr[0:SUBLANES, :] = halo_ref[...]
    xx_scr[SUBLANES:SUBLANES + tb, :] = g_ref[...]
    acc = g_ref[...] * cw_ref[FFN_CONV - 1:FFN_CONV, :]
    for s in range(1, FFN_CONV):
        tap = xx_scr[SUBLANES - s:SUBLANES - s + tb, :]
        acc = acc + jnp.where(pos >= s, tap, 0.0) * cw_ref[FFN_CONV - 1 - s:FFN_CONV - s, :]
    o_ref[...] = (_silu(acc + cb_ref[...]) * v_ref[...]).astype(o_ref.dtype)


def ffn_activation(gate, gcol0, val, vcol0, conv_w, conv_b, d_ff, *, rows, ls, tb, bc):
    assert rows % tb == 0 and d_ff % bc == 0 and gcol0 % bc == 0 and vcol0 % bc == 0
    hb = tb // SUBLANES
    g0 = gcol0 // bc
    v0 = vcol0 // bc
    return pl.pallas_call(
        functools.partial(_ffn_act_kernel, tb=tb, ls=ls),
        out_shape=SDS((rows, d_ff), BF16),
        grid=(rows // tb, d_ff // bc),
        in_specs=[pl.BlockSpec((tb, bc), lambda i, j: (i, g0 + j)),
                  pl.BlockSpec((SUBLANES, bc), lambda i, j: (jnp.maximum(i * hb - 1, 0), g0 + j)),
                  pl.BlockSpec((tb, bc), lambda i, j: (i, v0 + j)),
                  pl.BlockSpec((FFN_CONV, bc), lambda i, j: (0, j)),
                  pl.BlockSpec((1, bc), lambda i, j: (0, j))],
        out_specs=pl.BlockSpec((tb, bc), lambda i, j: (i, j)),
        scratch_shapes=[pltpu.VMEM((tb + SUBLANES, bc), F32)],
        compiler_params=_cparams(("parallel", "parallel")),
        name="ffn_activation",
    )(gate, gate, val, conv_w.astype(F32), conv_b.reshape(1, d_ff).astype(F32))


IN_PAD = 512


def _in_layout(d_model):
    off = {}
    c = 0
    for name, width in (("dq", DN_W), ("dk", DN_W), ("dv", DN_W), ("dz", DN_W),
                        ("pin", POOL_W), ("sq", SWA_W), ("sk", SWA_KW), ("sv", SWA_KW),
                        ("ga", d_model), ("gb", d_model), ("gc", d_model), ("ba", LANES)):
        off[name] = c
        c += width
    return off, -(-c // IN_PAD) * IN_PAD


def _ext(hist, new, ls):
    nseq, hr, c = hist.shape
    ln = new.shape[1]
    z = jnp.zeros((nseq, ls - hr - ln, c), new.dtype)
    return jnp.concatenate([z, hist.astype(new.dtype), new], axis=1).reshape(nseq * ls, c)


def _layer(x, nseq, seqlen, states, layer, wts, cfg):
    (n1, w_in, dconv_w, a_log, dt_bias, dn_norm_w, pool_w, pool_scale, q_norm_w, k_norm_w, sinks,
     w_ba, w_bb, w_bc, w_out, n2, w_up, fconv_w, fconv_b, w_down, state_delta) = wts
    rows, d_model = x.shape
    d_ff = w_down.shape[1]
    off = cfg["off"]
    fresh = states is None
    bm = _pick(seqlen if fresh else rows, cfg["bm"])
    bm_big = _pick(seqlen if fresh else rows, cfg["bm_big"])

    h = rmsnorm_bf16(x, n1)
    proj = matmul(h, w_in, layer, bm=bm_big, bn=cfg["bn_in"], name="in_proj")
    pb = proj
    offb = off
    proj3 = proj.reshape(nseq, seqlen, -1)

    def cols_a(name, width):
        return proj3[:, :, off[name]:off[name] + width]

    cols_b = cols_a

    if fresh:
        ls, chunk, vfrom = seqlen, cfg["chunk"], 0
        tbp = _pick(seqlen, cfg["tb_prep"], chunk)
        q, k, v, g, b = delta_prep(proj, 0, proj, off["ba"] // LANES, dconv_w, a_log, dt_bias,
                                   rows=rows, ls=ls, chunk=chunk, valid_from=0, tb=tbp)
        ya, s_new = delta_recurrence(q, k, v, g, b, proj, off["dz"] // LANES, None, layer, dn_norm_w,
                                     nseq=nseq, ls=ls, chunk=chunk, heads=cfg["heads"])
        conv_new = cols_a("dq", 3 * DN_W)[:, seqlen - (DN_CONV - 1):]
    else:
        conv_h, pool_h, k_h, v_h, ffn_h = states
        ls = SAMPLE_DELTA_ROWS
        vfrom = ls - seqlen
        xe = _ext(conv_h, cols_a("dq", 3 * DN_W), ls)
        de = _ext(jnp.zeros((nseq, 0, LANES), F32), cols_a("ba", LANES), ls)
        ze = _ext(jnp.zeros((nseq, 0, DN_W), F32), cols_a("dz", DN_W), ls)
        erows = nseq * ls
        q, k, v, g, b = delta_prep(xe, 0, de, 0, dconv_w, a_log, dt_bias,
                                   rows=erows, ls=ls, chunk=ls, valid_from=vfrom, tb=_pick(erows, 1024, ls))
        ya_e, s_new = delta_recurrence(q, k, v, g, b, ze, 0, state_delta, layer, dn_norm_w,
                                       nseq=nseq, ls=ls, chunk=ls, heads=cfg["heads"])
        ya = ya_e.reshape(nseq, ls, DN_W)[:, vfrom:].reshape(rows, DN_W)
        conv_new = xe.reshape(nseq, ls, 3 * DN_W)[:, ls - (DN_CONV - 1):]

    if fresh:
        yb = pool_mixer(pb, offb["pin"] // POOL_W, pool_w, pool_scale, rows=rows, ls=seqlen,
                        tb=_pick(seqlen, 512, 2 * SUBLANES))
        pool_new = cols_b("pin", POOL_W)[:, seqlen - POOL_HIST:]
    else:
        pe = _ext(pool_h, cols_b("pin", POOL_W), SAMPLE_SEQ_ROWS)
        erows = nseq * SAMPLE_SEQ_ROWS
        yb_e = pool_mixer(pe, 0, pool_w, pool_scale, rows=erows, ls=SAMPLE_SEQ_ROWS, tb=_pick(erows, 512, SAMPLE_SEQ_ROWS))
        yb = yb_e.reshape(nseq, SAMPLE_SEQ_ROWS, POOL_W)[:, SAMPLE_SEQ_ROWS - seqlen:].reshape(rows, POOL_W)
        pool_new = pe.reshape(nseq, SAMPLE_SEQ_ROWS, POOL_W)[:, SAMPLE_SEQ_ROWS - POOL_HIST:]

    qn, kn, kx, vx = swa_prep(pb, offb["sq"] // SWA_W, offb["sk"] // SWA_KW, offb["sv"] // SWA_KW,
                              q_norm_w, k_norm_w, rows=rows, tb=_pick(rows, 512, SUBLANES))
    vraw = cols_b("sv", SWA_KW)
    kn3 = kn.reshape(nseq, seqlen, SWA_KW)
    if fresh:
        yc = swa_attention(qn, kx, kx, vx, vx, sinks, nseq=nseq, nq=seqlen // WINDOW, qb=WINDOW,
                           prev_shift=1, first_has_prev=False)
        keep = min(WINDOW, seqlen)
        k_new = kn3[:, seqlen - keep:].reshape(nseq, keep, SWA_KV, SWA_HD)
        v_new = vraw[:, seqlen - keep:].reshape(nseq, keep, SWA_KV, SWA_HD)
    else:
        wb = k_h.shape[1]
        qpad = jnp.zeros((nseq, SUBLANES - seqlen, SWA_W), BF16)
        qe = jnp.concatenate([qn.reshape(nseq, seqlen, SWA_W), qpad], axis=1).reshape(nseq * SUBLANES, SWA_W)
        kpad = jnp.zeros((nseq, WINDOW - seqlen, SWA_W), BF16)
        kc = jnp.concatenate([kx.reshape(nseq, seqlen, SWA_W), kpad], axis=1).reshape(nseq * WINDOW, SWA_W)
        vc = jnp.concatenate([vx.reshape(nseq, seqlen, SWA_W), kpad], axis=1).reshape(nseq * WINDOW, SWA_W)
        kp = _two_slot(k_h.reshape(nseq * wb, SWA_KW))
        vp = _two_slot(v_h.reshape(nseq * wb, SWA_KW), ones=True)
        yc_e = swa_attention(qe, kp, kc, vp, vc, sinks, nseq=nseq, nq=1, qb=SUBLANES,
                             prev_shift=0, first_has_prev=True)
        yc = yc_e.reshape(nseq, SUBLANES, SWA_W)[:, :seqlen].reshape(rows, SWA_W)
        k_new = jnp.concatenate([k_h, kn3.reshape(nseq, seqlen, SWA_KV, SWA_HD).astype(k_h.dtype)], axis=1)[:, -wb:]
        v_new = jnp.concatenate([v_h, vraw.reshape(nseq, seqlen, SWA_KV, SWA_HD).astype(v_h.dtype)], axis=1)[:, -wb:]

    merged = merge_branches(ya, yb, yc, w_ba, w_bb, w_bc, layer, pb, offb["ga"], d_model, bm=bm, bn=cfg["bn_merge"])
    x = matmul(merged, w_out, layer, bm=bm, bn=cfg["bn_out"], residual=x, name="out_proj")

    h2 = rmsnorm_bf16(x, n2)
    bc = cfg["bc_ffn"]
    if fresh:
        hmid, tail = up_ffn_fused(h2, w_up, layer, fconv_w, fconv_b, d_ff, bm=bm_big, bc=bc, seqlen=seqlen)
        tps = seqlen // bm_big
        tail4 = tail.reshape(nseq, tps, SUBLANES, d_ff)
        ffn_new = tail4[:, tps - 1, SUBLANES - (FFN_CONV - 1):]
    else:
        up = matmul(h2, w_up, layer, bm=bm, bn=cfg["bn_up"], name="up_proj")
        up3 = up.reshape(nseq, seqlen, 2 * d_ff)
        ls = SUBLANES
        ge = _ext(ffn_h, up3[:, :, :d_ff], ls)
        ve = _ext(jnp.zeros((nseq, 0, d_ff), F32), up3[:, :, d_ff:], ls)
        hm_e = ffn_activation(ge, 0, ve, 0, fconv_w, fconv_b, d_ff, rows=nseq * ls, ls=ls, tb=nseq * ls, bc=bc)
        hmid = hm_e.reshape(nseq, ls, d_ff)[:, ls - seqlen:].reshape(rows, d_ff)
        ffn_new = ge.reshape(nseq, ls, d_ff)[:, ls - (FFN_CONV - 1):]
    x = matmul(hmid, w_down, layer, bm=min(bm, cfg["bm_down"]), bn=cfg["bn_down"], residual=x, name="down_proj")
    return x, (s_new, conv_new, pool_new, k_new, v_new, ffn_new)


def _config(d_model, d_ff):
    off, n_pad = _in_layout(d_model)
    assert off["ga"] % IN_PAD == 0 and d_model % IN_PAD == 0
    return {
        "off": off,
        "bm": 1024,
        "bm_big": 2048,
        "bn_in": IN_PAD,
        "bn_merge": IN_PAD,
        "bn_out": _pick(d_model, 512, LANES),
        "bn_up": _pick(2 * d_ff, 512, LANES),
        "bm_down": 512,
        "bn_down": _pick(d_model, 512, LANES),
        "bc_ffn": _pick(d_ff, 256, LANES),
        "chunk": DELTA_CHUNK,
        "tb_prep": 2048,
        "heads": REC_HEADS,
    }


def kernel(x_prompt, x_sample, state_delta, state_dconv, state_pool, cache_swa_k, cache_swa_v, state_ffn_conv, norm1_w, w_in, dconv_w, dn_a_log, dn_dt_bias, dn_norm_w, pool_w, pool_scale, q_norm_w, k_norm_w, sinks, w_branch_a, w_branch_b, w_branch_c, w_out, norm2_w, w_up, ffn_conv_w, ffn_conv_b, w_down):
    depth = w_in.shape[0]
    bp, lp, d_model = x_prompt.shape
    bs, lsm, _ = x_sample.shape
    d_ff = w_down.shape[1]
    cfg = _config(d_model, d_ff)
    xp = x_prompt.reshape(bp * lp, d_model)
    xs = x_sample.reshape(bs * lsm, d_model)
    w_in_bf = prep_w_in(w_in, d_model)
    w_ba_bf, w_bb_bf, w_bc_bf, w_out_bf, w_up_bf, w_down_bf = (
        cast_bf16(w) for w in (w_branch_a, w_branch_b, w_branch_c, w_out, w_up, w_down))
    outs_p = [[] for _ in range(6)]
    outs_s = [[] for _ in range(6)]
    for l in range(depth):
        wts = (norm1_w[l], w_in_bf, dconv_w[l].astype(F32), dn_a_log[l], dn_dt_bias[l],
               dn_norm_w[l].astype(F32), pool_w[l], pool_scale[l], q_norm_w[l], k_norm_w[l], sinks[l],
               w_ba_bf, w_bb_bf, w_bc_bf, w_out_bf, norm2_w[l], w_up_bf, ffn_conv_w[l], ffn_conv_b[l],
               w_down_bf, state_delta.astype(F32))
        xp, st_p = _layer(xp, bp, lp, None, l, wts, cfg)
        st_in = (state_dconv[l], state_pool[l], cache_swa_k[l], cache_swa_v[l], state_ffn_conv[l])
        xs, st_s = _layer(xs, bs, lsm, st_in, l, wts, cfg)
        for lst, a in zip(outs_p, st_p):
            lst.append(a)
        for lst, a in zip(outs_s, st_s):
            lst.append(a)
    dt = x_prompt.dtype
    res = [xp.reshape(bp, lp, d_model), xs.reshape(bs, lsm, d_model)]
    for outs in (outs_p, outs_s):
        res.append(jnp.stack(outs[0]).astype(state_delta.dtype))
        for t in outs[1:]:
            res.append(jnp.stack(t).astype(dt))
    return tuple(res)
```
